```python
import jax, jax.numpy as jnp
from jax import lax
import numpy as np

D_MODEL = 2048
BATCH = 8
SEQ = 2048
DEPTH = 4

GRID_W = 64
CTX_LEN = 256
HEAD_DIM = 64
MIX_WIDTH = D_MODEL
LRU_WIDTH = MIX_WIDTH // 4
LRU_BLOCKS = LRU_WIDTH // HEAD_DIM
LRU_CONV = 4
LRU_CONV_PAD = 2
LRU_C = 8.0
NA_WIDTH = MIX_WIDTH // 2
NA_HEADS = NA_WIDTH // HEAD_DIM
NA_ROWS = 8
NA_KC = 16
NA_QC = 16
NA_SPAN = NA_QC + NA_KC
ROPE_BASE = 10000.0
ROPE_FREQS = HEAD_DIM // 4
SGU_WIDTH = MIX_WIDTH - LRU_WIDTH - NA_WIDTH
SGU_GROUPS = SGU_WIDTH // HEAD_DIM
SGU_CHUNK = 128
PROJ_WIDTH = 2 * LRU_WIDTH + 3 * NA_WIDTH + 2 * SGU_WIDTH
FFN_HIDDEN = 5504
FFN_CONV = 3
NORM_EPS = 1e-6
NEG_INF = -1e30

kernel_name = 'hybrid_lru_natten_sgu_dit'


def rmsnorm(x, g):
    xf = x.astype(jnp.float32)
    y = xf * lax.rsqrt(jnp.mean(xf * xf, axis=-1, keepdims=True) + NORM_EPS)
    return (y * g.astype(jnp.float32)).astype(x.dtype)


def layernorm(x, g, b):
    xf = x.astype(jnp.float32)
    mu = jnp.mean(xf, axis=-1, keepdims=True)
    var = jnp.mean(jnp.square(xf - mu), axis=-1, keepdims=True)
    return ((xf - mu) * lax.rsqrt(var + NORM_EPS) * g.astype(jnp.float32) + b.astype(jnp.float32)).astype(x.dtype)


def adaln(cvec, w, b):
    return jnp.split(jax.nn.silu(cvec) @ w + b, 6, axis=-1)


def modulate(h, shift, scale):
    return h * (1 + scale) + shift


def dwconv(x, w, b, pad_left):
    K, T = w.shape[0], x.shape[1]
    xp = jnp.pad(x, ((0, 0), (pad_left, K - 1 - pad_left), (0, 0)))
    y = b
    for k in range(K):
        y = y + xp[:, k:k + T] * w[k]
    return y


def split_proj(z):
    a = LRU_WIDTH
    idx = [a, 2 * a, 2 * a + NA_WIDTH, 2 * a + 2 * NA_WIDTH, 2 * a + 3 * NA_WIDTH,
           2 * a + 3 * NA_WIDTH + SGU_WIDTH]
    return jnp.split(z, idx, axis=-1)


def to_heads(t):
    return t.reshape(t.shape[0], t.shape[1], NA_HEADS, HEAD_DIM)


def rglru_coeffs(x, gate_w, gate_b, lam):
    B, T, W = x.shape
    xg = x.reshape(B, T, LRU_BLOCKS, HEAD_DIM)
    g = jnp.einsum('btgi,kgio->kbtgo', xg, gate_w).reshape(2, B, T, W) + gate_b[:, None, None, :]
    r = jax.nn.sigmoid(g[0].astype(jnp.float32))
    i = jax.nn.sigmoid(g[1].astype(jnp.float32))
    log_a = -LRU_C * jax.nn.softplus(-lam.astype(jnp.float32)) * r
    a = jnp.exp(log_a)
    b = jnp.sqrt(-jnp.expm1(2.0 * log_a)) * (i * x.astype(jnp.float32))
    return a, b


def _lin_combine(left, right):
    a_l, b_l = left
    a_r, b_r = right
    return a_l * a_r, a_r * b_l + b_r


def linear_scan(a, b, h0, reverse):
    if reverse:
        a, b = jnp.flip(a, 1), jnp.flip(b, 1)
    b = b.at[:, 0].add(a[:, 0] * h0)
    _, h = lax.associative_scan(_lin_combine, (a, b), axis=1)
    return jnp.flip(h, 1) if reverse else h


def rglru_mixer(xl, yl, xc, yc, conv_w, conv_b, gate_w, gate_b, lam, need_ctx):
    xl = dwconv(xl, conv_w, conv_b, LRU_CONV_PAD)
    xc = dwconv(xc, conv_w, conv_b, LRU_CONV_PAD)
    B, _, W = xc.shape
    h_lat, h_ctx = [], []
    for d in range(2):
        rev = d == 1
        a_c, b_c = rglru_coeffs(xc, gate_w[d], gate_b[d], lam[d])
        hc = linear_scan(a_c, b_c, jnp.zeros((B, W), jnp.float32), rev)
        h_final = hc[:, 0] if rev else hc[:, -1]
        a_l, b_l = rglru_coeffs(xl, gate_w[d], gate_b[d], lam[d])
        h_lat.append(linear_scan(a_l, b_l, h_final, rev))
        h_ctx.append(hc)
    out_l = (h_lat[0] + h_lat[1]).astype(xl.dtype) * jax.nn.gelu(yl)
    if not need_ctx:
        return out_l, None
    out_c = (h_ctx[0] + h_ctx[1]).astype(xc.dtype) * jax.nn.gelu(yc)
    return out_l, out_c


def rope2d_tables(S):
    t = jnp.arange(S)
    pos = jnp.stack([t // GRID_W, t % GRID_W], axis=-1).astype(jnp.float32)
    inv = ROPE_BASE ** (-jnp.arange(ROPE_FREQS, dtype=jnp.float32) / ROPE_FREQS)
    ang = pos[:, :, None] * inv
    return jnp.cos(ang), jnp.sin(ang)


def rope2d(x, cos, sin):
    B, S, H, dh = x.shape
    xr = x.reshape(B, S, H, 2, 2, ROPE_FREQS)
    x1, x2 = xr[..., 0, :], xr[..., 1, :]
    cs = cos[None, :, None].astype(x.dtype)
    sn = sin[None, :, None].astype(x.dtype)
    return jnp.stack([x1 * cs - x2 * sn, x2 * cs + x1 * sn], axis=-2).reshape(B, S, H, dh)


def neighbourhood_attention(q, k, v, k_ctx, v_ctx, rpb, cos, sin):
    B, S, H, dh = q.shape
    rows = S // GRID_W
    kr = min(NA_ROWS, rows)
    scale = dh ** -0.5
    grid = lambda t: t.reshape(B, rows, GRID_W, H, dh)
    q_rot, k_g, v_g, q_raw = grid(rope2d(q, cos, sin)), grid(rope2d(k, cos, sin)), grid(v), grid(q)
    r_idx = jnp.arange(rows)
    row_start = jnp.clip(r_idx - kr // 2, 0, rows - kr)
    key_rows = row_start[:, None] + jnp.arange(kr)[None, :]
    rel_r = key_rows - r_idx[:, None] + (NA_ROWS - 1)
    n_win = kr * NA_SPAN

    def block(j):
        c0 = j * NA_QC
        q_cols = c0 + jnp.arange(NA_QC)
        col_start = jnp.clip(q_cols - NA_KC // 2, 0, GRID_W - NA_KC)
        key_cols = jnp.clip(c0 - NA_KC // 2, 0, GRID_W - NA_SPAN) + jnp.arange(NA_SPAN)
        in_win = (key_cols[None, :] >= col_start[:, None]) & (key_cols[None, :] < col_start[:, None] + NA_KC)
        rel_c = jnp.clip(key_cols[None, :] - q_cols[:, None] + (NA_KC - 1), 0, 2 * NA_KC - 2)
        kb = k_g[:, key_rows[:, :, None], key_cols[None, None, :]]
        vb = v_g[:, key_rows[:, :, None], key_cols[None, None, :]]
        qb = lax.dynamic_slice_in_dim(q_rot, c0, NA_QC, axis=2)
        qb_raw = lax.dynamic_slice_in_dim(q_raw, c0, NA_QC, axis=2)
        bias = rpb[:, rel_r[:, None, :, None], rel_c[None, :, None, :]]
        s_win = jnp.einsum('brqhd,brkmhd->bhrqkm', qb, kb).astype(jnp.float32) * scale
        s_win = s_win + bias[None].astype(jnp.float32)
        s_win = jnp.where(in_win[:, None, :], s_win, NEG_INF).reshape(B, H, rows, NA_QC, n_win)
        s_ctx = jnp.einsum('brqhd,blhd->bhrql', qb_raw, k_ctx).astype(jnp.float32) * scale
        p = jax.nn.softmax(jnp.concatenate([s_win, s_ctx], axis=-1), axis=-1).astype(v.dtype)
        o = jnp.einsum('bhrqk,brkhd->brqhd', p[..., :n_win], vb.reshape(B, rows, n_win, H, dh))
        return o + jnp.einsum('bhrql,blhd->brqhd', p[..., n_win:], v_ctx)

    o = lax.map(block, jnp.arange(GRID_W // NA_QC))
    return jnp.moveaxis(o, 0, 2).reshape(B, S, H * dh)


def context_attention(q, k, v):
    B, L, H, dh = q.shape
    s = jnp.einsum('blhd,bmhd->bhlm', q, k).astype(jnp.float32) * (dh ** -0.5)
    p = jax.nn.softmax(s, axis=-1).astype(v.dtype)
    return jnp.einsum('bhlm,bmhd->blhd', p, v).reshape(B, L, H * dh)


def spatial_gating(u, v, ln_g, ln_b, w_s, b_s):
    u = jax.nn.gelu(u)
    v = layernorm(jax.nn.gelu(v), ln_g, ln_b)
    B, T, _ = v.shape
    vg = v.reshape(B, T // SGU_CHUNK, SGU_CHUNK, SGU_GROUPS, HEAD_DIM)
    mixed = jnp.einsum('gpq,bnqgd->bnpgd', w_s, vg) + b_s.T[None, None, :, :, None]
    return u * mixed.reshape(B, T, SGU_WIDTH)


def hybrid_mixer(hx, hc, w_in, lru_conv_w, lru_conv_b, lru_gate_w, lru_gate_b, lru_lambda,
                 na_rpb, sgu_ln_g, sgu_ln_b, sgu_w, sgu_b, cos, sin, need_ctx):
    ax, ay, q, k, v, su, sv = split_proj(hx @ w_in)
    cax, cay, cq, ck, cv, csu, csv = split_proj(hc @ w_in)
    ck_h, cv_h = to_heads(ck), to_heads(cv)
    out_a, cout_a = rglru_mixer(ax, ay, cax, cay, lru_conv_w, lru_conv_b, lru_gate_w, lru_gate_b,
                                lru_lambda, need_ctx)
    out_b = neighbourhood_attention(to_heads(q), to_heads(k), to_heads(v), ck_h, cv_h, na_rpb, cos, sin)
    out_c = spatial_gating(su, sv, sgu_ln_g, sgu_ln_b, sgu_w, sgu_b)
    mix_x = jnp.concatenate([out_a, out_b, out_c], axis=-1)
    if not need_ctx:
        return mix_x, None
    cout_b = context_attention(to_heads(cq), ck_h, cv_h)
    cout_c = spatial_gating(csu, csv, sgu_ln_g, sgu_ln_b, sgu_w, sgu_b)
    return mix_x, jnp.concatenate([cout_a, cout_b, cout_c], axis=-1)


def conv_ffn(h, w_up, conv_w, conv_b, w_down):
    u = dwconv(h @ w_up, conv_w, conv_b, 1)
    a, g = jnp.split(u, 2, axis=-1)
    return (jax.nn.silu(g) * a) @ w_down


def setup_inputs(seed: int = 0) -> dict:
    key = jax.random.key(seed)
    ks = jax.random.split(key, 26)
    D = D_MODEL
    nrm = lambda k, shape, s: jax.random.normal(k, shape, jnp.float32) * s
    a_pow = jax.random.uniform(ks[14], (DEPTH, 2, LRU_WIDTH), jnp.float32, 0.9, 0.999)
    sig = a_pow ** (1.0 / LRU_C)
    return {
        'x': nrm(ks[0], (BATCH, SEQ, D), 1.0),
        'c': nrm(ks[1], (BATCH, D), 1.0),
        'ctx': nrm(ks[2], (BATCH, CTX_LEN, D), 1.0),
        'c_ctx': nrm(ks[3], (D,), 1.0),
        'w_ada': nrm(ks[4], (DEPTH, D, 6 * D), 0.5 * D ** -0.5),
        'b_ada': nrm(ks[5], (DEPTH, 6 * D), 0.01),
        'norm_mix_g': 1.0 + nrm(ks[6], (DEPTH, D), 0.02),
        'norm_ffn_g': 1.0 + nrm(ks[7], (DEPTH, D), 0.02),
        'w_in': nrm(ks[8], (DEPTH, D, PROJ_WIDTH), D ** -0.5),
        'lru_conv_w': nrm(ks[9], (DEPTH, LRU_CONV, LRU_WIDTH), LRU_CONV ** -0.5),
        'lru_conv_b': nrm(ks[10], (DEPTH, LRU_WIDTH), 0.01),
        'lru_gate_w': nrm(ks[11], (DEPTH, 2, 2, LRU_BLOCKS, HEAD_DIM, HEAD_DIM), HEAD_DIM ** -0.5),
        'lru_gate_b': nrm(ks[12], (DEPTH, 2, 2, LRU_WIDTH), 0.01),
        'lru_lambda': jnp.log(sig) - jnp.log1p(-sig),
        'na_rpb': nrm(ks[15], (DEPTH, NA_HEADS, 2 * NA_ROWS - 1, 2 * NA_KC - 1), 0.1),
        'sgu_ln_g': 1.0 + nrm(ks[16], (DEPTH, SGU_WIDTH), 0.02),
        'sgu_ln_b': nrm(ks[17], (DEPTH, SGU_WIDTH), 0.01),
        'sgu_w': nrm(ks[18], (DEPTH, SGU_GROUPS, SGU_CHUNK, SGU_CHUNK), 0.5 * SGU_CHUNK ** -0.5),
        'sgu_b': 1.0 + nrm(ks[19], (DEPTH, SGU_GROUPS, SGU_CHUNK), 0.01),
        'w_out': nrm(ks[20], (DEPTH, MIX_WIDTH, D), MIX_WIDTH ** -0.5),
        'ffn_up': nrm(ks[21], (DEPTH, D, 2 * FFN_HIDDEN), D ** -0.5),
        'ffn_conv_w': nrm(ks[22], (DEPTH, FFN_CONV, 2 * FFN_HIDDEN), FFN_CONV ** -0.5),
        'ffn_conv_b': nrm(ks[23], (DEPTH, 2 * FFN_HIDDEN), 0.01),
        'ffn_down': nrm(ks[24], (DEPTH, FFN_HIDDEN, D), FFN_HIDDEN ** -0.5),
        'final_norm_g': 1.0 + nrm(ks[25], (D,), 0.02),
    }


def reference(x, c, ctx, c_ctx, w_ada, b_ada, norm_mix_g, norm_ffn_g, w_in, lru_conv_w, lru_conv_b,
              lru_gate_w, lru_gate_b, lru_lambda, na_rpb, sgu_ln_g, sgu_ln_b, sgu_w, sgu_b, w_out,
              ffn_up, ffn_conv_w, ffn_conv_b, ffn_down, final_norm_g):
    cos, sin = rope2d_tables(x.shape[1])
    for i in range(DEPTH):
        need_ctx = i < DEPTH - 1
        sh_a, sc_a, g_a, sh_f, sc_f, g_f = adaln(c, w_ada[i], b_ada[i])
        csh_a, csc_a, cg_a, csh_f, csc_f, cg_f = adaln(c_ctx, w_ada[i], b_ada[i])
        hx = modulate(rmsnorm(x, norm_mix_g[i]), sh_a[:, None], sc_a[:, None])
        hc = modulate(rmsnorm(ctx, norm_mix_g[i]), csh_a, csc_a)
        mix_x, mix_c = hybrid_mixer(hx, hc, w_in[i], lru_conv_w[i], lru_conv_b[i], lru_gate_w[i],
                                    lru_gate_b[i], lru_lambda[i], na_rpb[i], sgu_ln_g[i], sgu_ln_b[i],
                                    sgu_w[i], sgu_b[i], cos, sin, need_ctx)
        x = x + g_a[:, None] * (mix_x @ w_out[i])
        hx = modulate(rmsnorm(x, norm_ffn_g[i]), sh_f[:, None], sc_f[:, None])
        x = x + g_f[:, None] * conv_ffn(hx, ffn_up[i], ffn_conv_w[i], ffn_conv_b[i], ffn_down[i])
        if need_ctx:
            ctx = ctx + cg_a * (mix_c @ w_out[i])
            hc = modulate(rmsnorm(ctx, norm_ffn_g[i]), csh_f, csc_f)
            ctx = ctx + cg_f * conv_ffn(hc, ffn_up[i], ffn_conv_w[i], ffn_conv_b[i], ffn_down[i])
    return rmsnorm(x, final_norm_g)
```

```python
import functools

import jax
import jax.numpy as jnp
from jax import lax
from jax.experimental import pallas as pl
from jax.experimental.pallas import tpu as pltpu

F32 = jnp.float32
BF16 = jnp.bfloat16

D = 2048
B = 8
SEQ = 2048
DEPTH = 4
GRID_W = 64
GRID_H = SEQ // GRID_W
CTX = 256
T = CTX + SEQ
HEAD = 64
LRU_W = 512
LRU_TAPS = 4
LRU_C = 8.0
NA_W = 1024
NA_HEADS = NA_W // HEAD
NA_ROWS = 8
NA_KC = 16
SGU_W = 512
SGU_G = SGU_W // HEAD
SGU_CHUNK = 128
PROJ = 2 * LRU_W + 3 * NA_W + 2 * SGU_W
FFN_H = 5504
EPS = 1e-6
NEG = -1e30
ROPE_BASE = 10000.0
ROPE_F = HEAD // 4
MOD_ROWS = 16
CTX_ROW = B

LANES = 128
SUBLANES = 8
VMEM_BIG = 56 * 1024 * 1024
VMEM_MID = 40 * 1024 * 1024

TM = 768
NT = T // TM
HALO = 16
TH = 512
FFN_HP = 5632
NK = FFN_HP // TH
LRU_HALF = 256
LRU_RB = 128
ATT_RB = 256
NORM_RB = 128


def _cparams(sem, vmem=VMEM_MID):
    return pltpu.CompilerParams(dimension_semantics=sem, vmem_limit_bytes=vmem)


def _dot(a, b):
    return jnp.dot(a, b, preferred_element_type=F32)


def _dot_nt(a, b):
    return lax.dot_general(a, b, (((1,), (1,)), ((), ())), preferred_element_type=F32)


def _mod_spec(layer, which, ctx, width=D, col=None):
    def imap(*ids):
        b = ids[0]
        row = CTX_ROW if ctx else b
        c = 0 if col is None else ids[col]
        return (layer, which, row, 0, c)
    return pl.BlockSpec((None, None, None, 1, width), imap)


def _rows_are_ctx(n, row0):
    return (lax.broadcasted_iota(jnp.int32, (n, 1), 0) + row0) < CTX


def _rms_mod(x, g, scale, shift):
    y = x * lax.rsqrt(jnp.mean(x * x, axis=-1, keepdims=True) + EPS)
    return (y * g) * (1.0 + scale) + shift


def _norm_rows(x_ref, dst_ref, dst_off, row0, g_ref, shb, scb, shc, scc):
    n = x_ref.shape[0]
    rb = min(n, NORM_RB)
    for blk in range(n // rb):
        r0 = blk * rb
        isc = _rows_are_ctx(rb, row0 + r0)
        scale = jnp.where(isc, scc[...], scb[...])
        shift = jnp.where(isc, shc[...], shb[...])
        hx = _rms_mod(x_ref[r0:r0 + rb, :], g_ref[...], scale, shift)
        dst_ref[dst_off + r0:dst_off + r0 + rb, :] = hx.astype(BF16)


def _adaln_kernel(c_ref, w_ref, b_ref, o_ref):
    c = c_ref[...]
    s = (c * jax.nn.sigmoid(c)).astype(BF16)
    o_ref[...] = _dot(s, w_ref[...].astype(BF16)) + b_ref[...]


def _adaln(cc, w_ada, b_ada):
    tn = 1024
    nj = D // tn
    return pl.pallas_call(
        _adaln_kernel,
        grid=(DEPTH, 6 * nj),
        in_specs=[
            pl.BlockSpec((MOD_ROWS, D), lambda l, j: (0, 0)),
            pl.BlockSpec((None, D, tn), lambda l, j: (l, 0, j)),
            pl.BlockSpec((None, 1, tn), lambda l, j: (l, 0, j)),
        ],
        out_specs=pl.BlockSpec((None, None, MOD_ROWS, tn), lambda l, j: (l, j // nj, 0, j % nj)),
        out_shape=jax.ShapeDtypeStruct((DEPTH, 6, MOD_ROWS, D), F32),
        compiler_params=_cparams(("parallel", "parallel")),
        name="adaln",
    )(cc, w_ada, b_ada.reshape(DEPTH, 1, 6 * D))


def _inproj_kernel(x_ref, g_ref, shb, scb, shc, scc, w_ref, o_ref, hx_s):
    i = pl.program_id(1)

    @pl.when(pl.program_id(2) == 0)
    def _():
        _norm_rows(x_ref, hx_s, 0, i * TM, g_ref, shb, scb, shc, scc)

    o_ref[...] = _dot(hx_s[...], w_ref[...])


def _inproj(xall, mod5, layer, g, w_bf):
    tn = 1024
    return pl.pallas_call(
        _inproj_kernel,
        grid=(B, NT, PROJ // tn),
        in_specs=[
            pl.BlockSpec((None, TM, D), lambda b, i, j: (b, i, 0)),
            pl.BlockSpec((1, D), lambda b, i, j: (0, 0)),
            _mod_spec(layer, 0, False), _mod_spec(layer, 1, False),
            _mod_spec(layer, 0, True), _mod_spec(layer, 1, True),
            pl.BlockSpec((D, tn), lambda b, i, j: (0, j)),
        ],
        out_specs=pl.BlockSpec((None, TM, tn), lambda b, i, j: (b, i, j)),
        out_shape=jax.ShapeDtypeStruct((B, T, PROJ), F32),
        scratch_shapes=[pltpu.VMEM((TM, D), BF16)],
        compiler_params=_cparams(("parallel", "parallel", "arbitrary")),
        name="inproj",
    )(xall, g.reshape(1, D), mod5, mod5, mod5, mod5, w_bf)


def _tile_scan(a, b, carry, rev):
    rid = lax.broadcasted_iota(jnp.int32, a.shape, 0)
    for d in (1, 2, 4):
        sh = SUBLANES - d if rev else d
        keep = (rid < SUBLANES - d) if rev else (rid >= d)
        a_s = jnp.where(keep, pltpu.roll(a, sh, 0), 1.0)
        b_s = jnp.where(keep, pltpu.roll(b, sh, 0), 0.0)
        b = a * b_s + b
        a = a * a_s
    h = a * carry + b
    return h, (h[0:1] if rev else h[SUBLANES - 1:SUBLANES])


def _lru_kernel(ax_ref, ay_ref, cw_ref, cb_ref, gw_ref, gb_ref, lam_ref, o_ref, af, bf, ab, bb):
    W = LRU_HALF
    cw = cw_ref[...]
    cb = cb_ref[...]
    gb = gb_ref[...]
    nl = -lam_ref[...]
    softplus = jnp.maximum(nl, 0.0) + jnp.log(1.0 + jnp.exp(-jnp.abs(nl)))
    coef = -LRU_C * softplus
    zeros8 = jnp.zeros((SUBLANES, W), F32)
    n_ext = LRU_RB + 2 * SUBLANES

    for blk in range(T // LRU_RB):
        r0 = blk * LRU_RB
        r1 = r0 + LRU_RB
        prev = zeros8 if r0 in (0, CTX) else ax_ref[r0 - SUBLANES:r0, :]
        nxt = zeros8 if r1 in (CTX, T) else ax_ref[r1:r1 + SUBLANES, :]
        ext = jnp.concatenate([prev, ax_ref[r0:r1, :], nxt], axis=0)
        lo, hi = SUBLANES, SUBLANES + LRU_RB
        xc = cb + pltpu.roll(ext, 2, 0)[lo:hi] * cw[0:1]
        xc = xc + pltpu.roll(ext, 1, 0)[lo:hi] * cw[1:2]
        xc = xc + ext[lo:hi] * cw[2:3]
        xc = xc + pltpu.roll(ext, n_ext - 1, 0)[lo:hi] * cw[3:4]
        xb = xc.astype(BF16)
        for d, (a_s, b_s) in enumerate(((af, bf), (ab, bb))):
            r = jax.nn.sigmoid(_dot(xb, gw_ref[2 * d]) + gb[2 * d:2 * d + 1])
            gi = jax.nn.sigmoid(_dot(xb, gw_ref[2 * d + 1]) + gb[2 * d + 1:2 * d + 2])
            log_a = coef[d:d + 1] * r
            a_s[r0:r1, :] = jnp.exp(log_a)
            th = jnp.tanh(log_a)
            b_s[r0:r1, :] = jnp.sqrt(-2.0 * th / (1.0 - th)) * (gi * xc)

    def make_body(f_base, b_top):
        def body(k, carry):
            cf, cr = carry
            rf = pl.multiple_of((f_base + k) * SUBLANES, SUBLANES)
            hf, cf = _tile_scan(af[pl.ds(rf, SUBLANES), :], bf[pl.ds(rf, SUBLANES), :], cf, False)
            bf[pl.ds(rf, SUBLANES), :] = hf
            rb = pl.multiple_of((b_top - k) * SUBLANES, SUBLANES)
            hb, cr = _tile_scan(ab[pl.ds(rb, SUBLANES), :], bb[pl.ds(rb, SUBLANES), :], cr, True)
            bb[pl.ds(rb, SUBLANES), :] = hb
            return cf, cr
        return body

    z1 = jnp.zeros((1, W), F32)
    n_ctx = CTX // SUBLANES
    n_all = T // SUBLANES
    carry = lax.fori_loop(0, n_ctx, make_body(0, n_ctx - 1), (z1, z1))
    lax.fori_loop(0, n_all - n_ctx, make_body(n_ctx, n_all - 1), carry)

    for blk in range(T // LRU_RB):
        r0 = blk * LRU_RB
        r1 = r0 + LRU_RB
        h = bf[r0:r1, :] + bb[r0:r1, :]
        o_ref[r0:r1, :] = (h * jax.nn.gelu(ay_ref[r0:r1, :])).astype(BF16)


def _lru(z, cw, cb, gw_bd, gb, lam):
    W = LRU_HALF
    nh = LRU_W // W
    return pl.pallas_call(
        _lru_kernel,
        grid=(B, nh),
        in_specs=[
            pl.BlockSpec((None, T, W), lambda b, c: (b, 0, c)),
            pl.BlockSpec((None, T, W), lambda b, c: (b, 0, nh + c)),
            pl.BlockSpec((LRU_TAPS, W), lambda b, c: (0, c)),
            pl.BlockSpec((1, W), lambda b, c: (0, c)),
            pl.BlockSpec((None, 4, W, W), lambda b, c: (c, 0, 0, 0)),
            pl.BlockSpec((4, W), lambda b, c: (0, c)),
            pl.BlockSpec((2, W), lambda b, c: (0, c)),
        ],
        out_specs=pl.BlockSpec((None, T, W), lambda b, c: (b, 0, c)),
        out_shape=jax.ShapeDtypeStruct((B, T, LRU_W), BF16),
        scratch_shapes=[pltpu.VMEM((T, W), F32)] * 4,
        compiler_params=_cparams(("parallel", "parallel")),
        name="lru",
    )(z, z, cw, cb, gw_bd, gb, lam)


def _attn_kernel(q_ref, k_ref, v_ref, cos_ref, sin_ref, bias_ref, o_ref, qraw_s, qrot_s, krot_s, kc_s, v_s):
    lane = lax.broadcasted_iota(jnp.int32, (1, LANES), 1)
    head0 = lane < HEAD
    second16 = ((lane // ROPE_F) % 2) == 1
    scale = HEAD ** -0.5

    def rope(x, c, s):
        partner = jnp.where(second16, pltpu.roll(x, ROPE_F, 1), pltpu.roll(x, LANES - ROPE_F, 1))
        return x * c + partner * s

    qraw_s[0:CTX, :] = (q_ref[0:CTX, :] * scale).astype(BF16)
    kc_s[...] = k_ref[0:CTX, :].astype(BF16)
    v_s[0:CTX, :] = v_ref[0:CTX, :].astype(BF16)
    for blk in range(SEQ // ATT_RB):
        r0 = blk * ATT_RB
        r1 = r0 + ATT_RB
        c = cos_ref[r0:r1, :]
        s = sin_ref[r0:r1, :]
        q = q_ref[CTX + r0:CTX + r1, :]
        qraw_s[CTX + r0:CTX + r1, :] = (q * scale).astype(BF16)
        qrot_s[r0:r1, :] = (rope(q, c, s) * scale).astype(BF16)
        krot_s[r0:r1, :] = rope(k_ref[CTX + r0:CTX + r1, :], c, s).astype(BF16)
        v_s[CTX + r0:CTX + r1, :] = v_ref[CTX + r0:CTX + r1, :].astype(BF16)

    def pick(hd, x):
        return jnp.where(head0 if hd == 0 else jnp.logical_not(head0), x, jnp.zeros_like(x))

    outs = []
    for hd in range(2):
        s = _dot_nt(pick(hd, qraw_s[0:CTX, :]), kc_s[...])
        e = jnp.exp(s - jnp.max(s, axis=-1, keepdims=True))
        o = _dot(e.astype(BF16), v_s[0:CTX, :]) / jnp.sum(e, axis=-1, keepdims=True)
        outs.append(o)
    o_ref[0:CTX, :] = jnp.where(head0, outs[0], outs[1]).astype(BF16)

    win = NA_ROWS * GRID_W

    def body(r, _):
        rs = jnp.clip(r - NA_ROWS // 2, 0, GRID_H - NA_ROWS)
        cls = r - rs
        q0 = pl.multiple_of(r * GRID_W, GRID_W)
        qx = pl.multiple_of(CTX + r * GRID_W, GRID_W)
        k0 = pl.multiple_of(rs * GRID_W, GRID_W)
        kx = pl.multiple_of(CTX + rs * GRID_W, GRID_W)
        qr = qrot_s[pl.ds(q0, GRID_W), :]
        qw = qraw_s[pl.ds(qx, GRID_W), :]
        kw = krot_s[pl.ds(k0, win), :]
        vw = v_s[pl.ds(kx, win), :]
        res = []
        for hd in range(2):
            sw = _dot_nt(pick(hd, qr), kw) + bias_ref[hd, cls]
            sc = _dot_nt(pick(hd, qw), kc_s[...])
            m = jnp.maximum(jnp.max(sw, axis=-1, keepdims=True), jnp.max(sc, axis=-1, keepdims=True))
            ew = jnp.exp(sw - m)
            ec = jnp.exp(sc - m)
            den = jnp.sum(ew, axis=-1, keepdims=True) + jnp.sum(ec, axis=-1, keepdims=True)
            res.append((_dot(ew.astype(BF16), vw) + _dot(ec.astype(BF16), v_s[0:CTX, :])) / den)
        o_ref[pl.ds(qx, GRID_W), :] = jnp.where(head0, res[0], res[1]).astype(BF16)
        return 0

    lax.fori_loop(0, GRID_H, body, 0)


def _attn(z, cos_t, sin_t, biasmask):
    nhp = NA_W // LANES
    qb = 2 * LRU_W // LANES
    return pl.pallas_call(
        _attn_kernel,
        grid=(nhp, B),
        in_specs=[
            pl.BlockSpec((None, T, LANES), lambda p, b: (b, 0, qb + p)),
            pl.BlockSpec((None, T, LANES), lambda p, b: (b, 0, qb + nhp + p)),
            pl.BlockSpec((None, T, LANES), lambda p, b: (b, 0, qb + 2 * nhp + p)),
            pl.BlockSpec((SEQ, LANES), lambda p, b: (0, 0)),
            pl.BlockSpec((SEQ, LANES), lambda p, b: (0, 0)),
            pl.BlockSpec((None, 2, NA_ROWS, GRID_W, NA_ROWS * GRID_W), lambda p, b: (p, 0, 0, 0, 0)),
        ],
        out_specs=pl.BlockSpec((None, T, LANES), lambda p, b: (b, 0, p)),
        out_shape=jax.ShapeDtypeStruct((B, T, NA_W), BF16),
        scratch_shapes=[
            pltpu.VMEM((T, LANES), BF16),
            pltpu.VMEM((SEQ, LANES), BF16),
            pltpu.VMEM((SEQ, LANES), BF16),
            pltpu.VMEM((CTX, LANES), BF16),
            pltpu.VMEM((T, LANES), BF16),
        ],
        compiler_params=_cparams(("parallel", "parallel")),
        name="attn",
    )(z, z, z, cos_t, sin_t, biasmask)


def _sgu_kernel(u_ref, v_ref, g_ref, b_ref, ws_ref, bs_ref, o_ref):
    lane = lax.broadcasted_iota(jnp.int32, (1, LANES), 1)
    first = lane < HEAD
    for n in range(TM // SGU_CHUNK):
        r0 = n * SGU_CHUNK
        r1 = r0 + SGU_CHUNK
        v = jax.nn.gelu(v_ref[r0:r1, :])
        mu = jnp.mean(v, axis=-1, keepdims=True)
        var = jnp.mean(jnp.square(v - mu), axis=-1, keepdims=True)
        vn = ((v - mu) * lax.rsqrt(var + EPS) * g_ref[...] + b_ref[...]).astype(BF16)
        for j in range(SGU_W // LANES):
            c0 = j * LANES
            c1 = c0 + LANES
            vp = vn[:, c0:c1]
            mixed = jnp.where(first, _dot(ws_ref[2 * j], vp), _dot(ws_ref[2 * j + 1], vp)) + bs_ref[:, c0:c1]
            o_ref[r0:r1, c0:c1] = (jax.nn.gelu(u_ref[r0:r1, c0:c1]) * mixed).astype(BF16)


def _sgu(z, ln_g, ln_b, ws_bf, bs_full):
    ub = (2 * LRU_W + 3 * NA_W) // SGU_W
    return pl.pallas_call(
        _sgu_kernel,
        grid=(B, NT),
        in_specs=[
            pl.BlockSpec((None, TM, SGU_W), lambda b, i: (b, i, ub)),
            pl.BlockSpec((None, TM, SGU_W), lambda b, i: (b, i, ub + 1)),
            pl.BlockSpec((1, SGU_W), lambda b, i: (0, 0)),
            pl.BlockSpec((1, SGU_W), lambda b, i: (0, 0)),
            pl.BlockSpec((SGU_G, SGU_CHUNK, SGU_CHUNK), lambda b, i: (0, 0, 0)),
            pl.BlockSpec((SGU_CHUNK, SGU_W), lambda b, i: (0, 0)),
        ],
        out_specs=pl.BlockSpec((None, TM, SGU_W), lambda b, i: (b, i, 0)),
        out_shape=jax.ShapeDtypeStruct((B, T, SGU_W), BF16),
        compiler_params=_cparams(("parallel", "parallel")),
        name="sgu",
    )(z, z, ln_g.reshape(1, SGU_W), ln_b.reshape(1, SGU_W), ws_bf, bs_full)


def _outproj_kernel(x_ref, a_ref, b_ref, c_ref, gtb, gtc, w_ref, o_ref):
    i = pl.program_id(1)
    acc = _dot(a_ref[...], w_ref[0:LRU_W, :])
    acc = acc + _dot(b_ref[...], w_ref[LRU_W:LRU_W + NA_W, :])
    acc = acc + _dot(c_ref[...], w_ref[LRU_W + NA_W:D, :])
    gate = jnp.where(_rows_are_ctx(TM, i * TM), gtc[...], gtb[...])
    o_ref[...] = x_ref[...] + gate * acc


def _outproj(xall, oa, ob, oc, mod5, layer, w_bf):
    tn = 1024
    return pl.pallas_call(
        _outproj_kernel,
        grid=(B, NT, D // tn),
        in_specs=[
            pl.BlockSpec((None, TM, tn), lambda b, i, j: (b, i, j)),
            pl.BlockSpec((None, TM, LRU_W), lambda b, i, j: (b, i, 0)),
            pl.BlockSpec((None, TM, NA_W), lambda b, i, j: (b, i, 0)),
            pl.BlockSpec((None, TM, SGU_W), lambda b, i, j: (b, i, 0)),
            _mod_spec(layer, 2, False, tn, 2), _mod_spec(layer, 2, True, tn, 2),
            pl.BlockSpec((D, tn), lambda b, i, j: (0, j)),
        ],
        out_specs=pl.BlockSpec((None, TM, tn), lambda b, i, j: (b, i, j)),
        out_shape=jax.ShapeDtypeStruct((B, T, D), F32),
        compiler_params=_cparams(("parallel", "parallel", "parallel")),
        name="outproj",
    )(xall, oa, ob, oc, mod5, mod5, w_bf)


def _ffn_kernel(x_ref, xp_ref, xn_ref, g_ref, shb, scb, gtb, shc, scc, gtc, wup_ref, cw_ref, cb_ref, wd_ref,
                o_ref, hx_s, u_s, act_s):
    i = pl.program_id(1)
    k = pl.program_id(2)

    @pl.when(k == 0)
    def _():
        _norm_rows(xp_ref, hx_s, 0, i * TM - HALO, g_ref, shb, scb, shc, scc)
        _norm_rows(x_ref, hx_s, HALO, i * TM, g_ref, shb, scb, shc, scc)
        _norm_rows(xn_ref, hx_s, HALO + TM, (i + 1) * TM, g_ref, shb, scb, shc, scc)
        o_ref[...] = jnp.zeros_like(o_ref)

    u_s[...] = _dot(hx_s[...], wup_ref[...])
    cw = cw_ref[...]
    cb = cb_ref[...]
    rb = 128
    for blk in range(TM // rb):
        r0 = HALO + blk * rb
        rows = lax.broadcasted_iota(jnp.int32, (rb, 1), 0) + (i * TM + blk * rb)
        left_ok = jnp.logical_and(rows != 0, rows != CTX)
        right_ok = jnp.logical_and(rows != CTX - 1, rows != T - 1)
        y = cb + jnp.where(left_ok, u_s[r0 - 1:r0 - 1 + rb, :], 0.0) * cw[0:1]
        y = y + u_s[r0:r0 + rb, :] * cw[1:2]
        y = y + jnp.where(right_ok, u_s[r0 + 1:r0 + 1 + rb, :], 0.0) * cw[2:3]
        a = y[:, :TH]
        gg = y[:, TH:]
        act_s[blk * rb:(blk + 1) * rb, :] = (gg * jax.nn.sigmoid(gg) * a).astype(BF16)
    o_ref[...] += _dot(act_s[...], wd_ref[...])

    @pl.when(k == NK - 1)
    def _():
        gate = jnp.where(_rows_are_ctx(TM, i * TM), gtc[...], gtb[...])
        o_ref[...] = x_ref[...] + gate * o_ref[...]


def _ffn(xall, mod5, layer, g, wup_r, cw_r, cb_r, wd_p):
    hb = TM // HALO
    last = T // HALO - 1
    return pl.pallas_call(
        _ffn_kernel,
        grid=(B, NT, NK),
        in_specs=[
            pl.BlockSpec((None, TM, D), lambda b, i, k: (b, i, 0)),
            pl.BlockSpec((None, HALO, D), lambda b, i, k: (b, jnp.maximum(i * hb - 1, 0), 0)),
            pl.BlockSpec((None, HALO, D), lambda b, i, k: (b, jnp.minimum((i + 1) * hb, last), 0)),
            pl.BlockSpec((1, D), lambda b, i, k: (0, 0)),
            _mod_spec(layer, 3, False), _mod_spec(layer, 4, False), _mod_spec(layer, 5, False),
            _mod_spec(layer, 3, True), _mod_spec(layer, 4, True), _mod_spec(layer, 5, True),
            pl.BlockSpec((D, 2 * TH), lambda b, i, k: (0, k)),
            pl.BlockSpec((3, 2 * TH), lambda b, i, k: (0, k)),
            pl.BlockSpec((1, 2 * TH), lambda b, i, k: (0, k)),
            pl.BlockSpec((TH, D), lambda b, i, k: (k, 0)),
        ],
        out_specs=pl.BlockSpec((None, TM, D), lambda b, i, k: (b, i, 0)),
        out_shape=jax.ShapeDtypeStruct((B, T, D), F32),
        scratch_shapes=[
            pltpu.VMEM((TM + 2 * HALO, D), BF16),
            pltpu.VMEM((TM + 2 * HALO, 2 * TH), F32),
            pltpu.VMEM((TM, TH), BF16),
        ],
        compiler_params=_cparams(("parallel", "parallel", "arbitrary"), VMEM_BIG),
        name="ffn",
    )(xall, xall, xall, g.reshape(1, D), mod5, mod5, mod5, mod5, mod5, mod5, wup_r, cw_r, cb_r, wd_p)


def _final_kernel(x_ref, g_ref, o_ref):
    x = x_ref[...]
    o_ref[...] = x * lax.rsqrt(jnp.mean(x * x, axis=-1, keepdims=True) + EPS) * g_ref[...]


def _final_norm(xall, g):
    tm = CTX
    return pl.pallas_call(
        _final_kernel,
        grid=(B, SEQ // tm),
        in_specs=[
            pl.BlockSpec((None, tm, D), lambda b, i: (b, i + 1, 0)),
            pl.BlockSpec((1, D), lambda b, i: (0, 0)),
        ],
        out_specs=pl.BlockSpec((None, tm, D), lambda b, i: (b, i, 0)),
        out_shape=jax.ShapeDtypeStruct((B, SEQ, D), F32),
        compiler_params=_cparams(("parallel", "parallel")),
        name="final_norm",
    )(xall, g.reshape(1, D))


def _rope_tables():
    t = jnp.arange(SEQ)
    pos = jnp.stack([t // GRID_W, t % GRID_W], axis=-1).astype(F32)
    inv = ROPE_BASE ** (-jnp.arange(ROPE_F, dtype=F32) / ROPE_F)
    ang = pos[:, :, None] * inv
    cos, sin = jnp.cos(ang), jnp.sin(ang)
    cos_h = jnp.concatenate([cos[:, 0], cos[:, 0], cos[:, 1], cos[:, 1]], axis=-1)
    sin_h = jnp.concatenate([-sin[:, 0], sin[:, 0], -sin[:, 1], sin[:, 1]], axis=-1)
    reps = LANES // HEAD
    return jnp.tile(cos_h, (1, reps)), jnp.tile(sin_h, (1, reps))


def _bias_tables(rpb):
    cls = jnp.arange(NA_ROWS)[:, None, None, None]
    c = jnp.arange(GRID_W)[None, :, None, None]
    i = jnp.arange(NA_ROWS)[None, None, :, None]
    kc = jnp.arange(GRID_W)[None, None, None, :]
    col_start = jnp.clip(c - NA_KC // 2, 0, GRID_W - NA_KC)
    in_win = (kc >= col_start) & (kc < col_start + NA_KC)
    rel_r = jnp.broadcast_to(i - cls + (NA_ROWS - 1), (NA_ROWS, GRID_W, NA_ROWS, GRID_W))
    rel_c = jnp.broadcast_to(jnp.clip(kc - c + (NA_KC - 1), 0, 2 * NA_KC - 2), rel_r.shape)
    bias = rpb[:, rel_r, rel_c]
    bias = jnp.where(in_win[None], bias, NEG)
    return bias.reshape(NA_HEADS // 2, 2, NA_ROWS, GRID_W, NA_ROWS * GRID_W)


def _gate_blockdiag(gate_w):
    nh = LRU_W // LRU_HALF
    gper = LRU_HALF // HEAD
    w = gate_w.reshape(4, nh, gper, HEAD, HEAD)
    eye = jnp.eye(gper, dtype=gate_w.dtype)
    bd = jnp.einsum('khgio,gj->khgijo', w, eye).reshape(4, nh, LRU_HALF, LRU_HALF)
    return jnp.transpose(bd, (1, 0, 2, 3)).astype(BF16)


def _ffn_layout(w_up, conv_w, conv_b, w_down):
    pad = FFN_HP - FFN_H

    def inter(m):
        r = m.shape[0]
        a = jnp.pad(m[:, :FFN_H], ((0, 0), (0, pad))).reshape(r, NK, 1, TH)
        g = jnp.pad(m[:, FFN_H:], ((0, 0), (0, pad))).reshape(r, NK, 1, TH)
        return jnp.concatenate([a, g], axis=2).reshape(r, NK * 2 * TH)

    wd = jnp.pad(w_down, ((0, pad), (0, 0)))
    return inter(w_up).astype(BF16), inter(conv_w), inter(conv_b.reshape(1, -1)), wd.astype(BF16)


def kernel(x, c, ctx, c_ctx, w_ada, b_ada, norm_mix_g, norm_ffn_g, w_in, lru_conv_w, lru_conv_b, lru_gate_w, lru_gate_b, lru_lambda, na_rpb, sgu_ln_g, sgu_ln_b, sgu_w, sgu_b, w_out, ffn_up, ffn_conv_w, ffn_conv_b, ffn_down, final_norm_g):
    xall = jnp.concatenate([ctx, x], axis=1)
    cc = jnp.concatenate([c, c_ctx[None], jnp.zeros((MOD_ROWS - B - 1, D), F32)], axis=0)
    mod5 = _adaln(cc, w_ada, b_ada).reshape(DEPTH, 6, MOD_ROWS, 1, D)
    cos_t, sin_t = _rope_tables()

    for l in range(DEPTH):
        z = _inproj(xall, mod5, l, norm_mix_g[l], w_in[l].astype(BF16))
        oa = _lru(z, lru_conv_w[l], lru_conv_b[l].reshape(1, LRU_W), _gate_blockdiag(lru_gate_w[l]),
                  lru_gate_b[l].reshape(4, LRU_W), lru_lambda[l])
        ob = _attn(z, cos_t, sin_t, _bias_tables(na_rpb[l]))
        bs_full = jnp.repeat(sgu_b[l].T, HEAD, axis=1)
        oc = _sgu(z, sgu_ln_g[l], sgu_ln_b[l], sgu_w[l].astype(BF16), bs_full)
        xall = _outproj(xall, oa, ob, oc, mod5, l, w_out[l].astype(BF16))
        xall = _ffn(xall, mod5, l, norm_ffn_g[l], *_ffn_layout(ffn_up[l], ffn_conv_w[l], ffn_conv_b[l], ffn_down[l]))
    return _final_norm(xall, final_norm_g)
```

```python
import functools

import jax
import jax.numpy as jnp
from jax import lax
from jax.experimental import pallas as pl
from jax.experimental.pallas import tpu as pltpu

F32 = jnp.float32
BF16 = jnp.bfloat16

D = 2048
B = 8
SEQ = 2048
DEPTH = 4
GRID_W = 64
GRID_H = SEQ // GRID_W
CTX = 256
T = CTX + SEQ
HEAD = 64
LRU_W = 512
LRU_TAPS = 4
LRU_C = 8.0
NA_W = 1024
NA_HEADS = NA_W // HEAD
NA_ROWS = 8
NA_KC = 16
SGU_W = 512
SGU_G = SGU_W // HEAD
SGU_CHUNK = 128
PROJ = 2 * LRU_W + 3 * NA_W + 2 * SGU_W
FFN_H = 5504
EPS = 1e-6
NEG = -1e30
ROPE_BASE = 10000.0
ROPE_F = HEAD // 4
MOD_ROWS = 16
CTX_ROW = B

LANES = 128
SUBLANES = 8
VMEM_BIG = 56 * 1024 * 1024
VMEM_MID = 40 * 1024 * 1024

TM = 768
NT = T // TM
HALO = 16
TH = 512
FFN_HP = 5632
NK = FFN_HP // TH
LRU_HALF = 256
LRU_RB = 128
ATT_RB = 256
NORM_RB = 128
FFN_RB = 128
ATT_UNROLL = 4
ATT_GROUP = 4


def _cparams(sem, vmem=VMEM_MID):
    return pltpu.CompilerParams(dimension_semantics=sem, vmem_limit_bytes=vmem)


def _dot(a, b):
    return jnp.dot(a, b, preferred_element_type=F32)


def _dot_nt(a, b):
    return lax.dot_general(a, b, (((1,), (1,)), ((), ())), preferred_element_type=F32)


def _mod_spec(layer, which, ctx, width=D, col=None):
    def imap(*ids):
        b = ids[0]
        row = CTX_ROW if ctx else b
        c = 0 if col is None else ids[col]
        return (layer, which, row, 0, c)
    return pl.BlockSpec((None, None, None, 1, width), imap)


def _rows_are_ctx(n, row0):
    return (lax.broadcasted_iota(jnp.int32, (n, 1), 0) + row0) < CTX


def _rms_mod(x, g, scale, shift):
    y = x * lax.rsqrt(jnp.mean(x * x, axis=-1, keepdims=True) + EPS)
    return (y * g) * (1.0 + scale) + shift


def _norm_rows(x_ref, dst_ref, dst_off, row0, g_ref, shb, scb, shc, scc):
    n = x_ref.shape[0]
    rb = min(n, NORM_RB)
    for blk in range(n // rb):
        r0 = blk * rb
        isc = _rows_are_ctx(rb, row0 + r0)
        scale = jnp.where(isc, scc[...], scb[...])
        shift = jnp.where(isc, shc[...], shb[...])
        hx = _rms_mod(x_ref[r0:r0 + rb, :], g_ref[...], scale, shift)
        dst_ref[dst_off + r0:dst_off + r0 + rb, :] = hx.astype(BF16)


def _adaln_kernel(c_ref, w_ref, b_ref, o_ref):
    c = c_ref[...]
    s = (c * jax.nn.sigmoid(c)).astype(BF16)
    o_ref[...] = _dot(s, w_ref[...].astype(BF16)) + b_ref[...]


def _adaln(cc, w_ada, b_ada):
    tn = 1024
    nj = D // tn
    return pl.pallas_call(
        _adaln_kernel,
        grid=(DEPTH, 6 * nj),
        in_specs=[
            pl.BlockSpec((MOD_ROWS, D), lambda l, j: (0, 0)),
            pl.BlockSpec((None, D, tn), lambda l, j: (l, 0, j)),
            pl.BlockSpec((None, 1, tn), lambda l, j: (l, 0, j)),
        ],
        out_specs=pl.BlockSpec((None, None, MOD_ROWS, tn), lambda l, j: (l, j // nj, 0, j % nj)),
        out_shape=jax.ShapeDtypeStruct((DEPTH, 6, MOD_ROWS, D), F32),
        compiler_params=_cparams(("parallel", "parallel")),
        name="adaln",
    )(cc, w_ada, b_ada.reshape(DEPTH, 1, 6 * D))


def _inproj_kernel(x_ref, g_ref, shb, scb, shc, scc, w_ref, o_ref, hx_s):
    i = pl.program_id(1)

    @pl.when(pl.program_id(2) == 0)
    def _():
        _norm_rows(x_ref, hx_s, 0, i * TM, g_ref, shb, scb, shc, scc)

    o_ref[...] = _dot(hx_s[...], w_ref[...])


def _inproj(xall, mod5, layer, g, w_bf):
    tn = 1024
    return pl.pallas_call(
        _inproj_kernel,
        grid=(B, NT, PROJ // tn),
        in_specs=[
            pl.BlockSpec((None, TM, D), lambda b, i, j: (b, i, 0)),
            pl.BlockSpec((1, D), lambda b, i, j: (0, 0)),
            _mod_spec(layer, 0, False), _mod_spec(layer, 1, False),
            _mod_spec(layer, 0, True), _mod_spec(layer, 1, True),
            pl.BlockSpec((D, tn), lambda b, i, j: (0, j)),
        ],
        out_specs=pl.BlockSpec((None, TM, tn), lambda b, i, j: (b, i, j)),
        out_shape=jax.ShapeDtypeStruct((B, T, PROJ), F32),
        scratch_shapes=[pltpu.VMEM((TM, D), BF16)],
        compiler_params=_cparams(("parallel", "parallel", "arbitrary")),
        name="inproj",
    )(xall, g.reshape(1, D), mod5, mod5, mod5, mod5, w_bf)


def _tile_scan(a, b, carry, rev):
    rid = lax.broadcasted_iota(jnp.int32, a.shape, 0)
    for d in (1, 2, 4):
        sh = SUBLANES - d if rev else d
        keep = (rid < SUBLANES - d) if rev else (rid >= d)
        a_s = jnp.where(keep, pltpu.roll(a, sh, 0), 1.0)
        b_s = jnp.where(keep, pltpu.roll(b, sh, 0), 0.0)
        b = a * b_s + b
        a = a * a_s
    h = a * carry + b
    return h, (h[0:1] if rev else h[SUBLANES - 1:SUBLANES])


def _lru_kernel(ax_ref, ay_ref, cw_ref, cb_ref, gw_ref, gb_ref, lam_ref, o_ref, af, bf, ab, bb):
    W = LRU_HALF
    cw = cw_ref[...]
    cb = cb_ref[...]
    gb = gb_ref[...]
    nl = -lam_ref[...]
    softplus = jnp.maximum(nl, 0.0) + jnp.log(1.0 + jnp.exp(-jnp.abs(nl)))
    coef = -LRU_C * softplus
    zeros8 = jnp.zeros((SUBLANES, W), F32)
    n_ext = LRU_RB + 2 * SUBLANES

    for blk in range(T // LRU_RB):
        r0 = blk * LRU_RB
        r1 = r0 + LRU_RB
        prev = zeros8 if r0 in (0, CTX) else ax_ref[r0 - SUBLANES:r0, :]
        nxt = zeros8 if r1 in (CTX, T) else ax_ref[r1:r1 + SUBLANES, :]
        ext = jnp.concatenate([prev, ax_ref[r0:r1, :], nxt], axis=0)
        lo, hi = SUBLANES, SUBLANES + LRU_RB
        xc = cb + pltpu.roll(ext, 2, 0)[lo:hi] * cw[0:1]
        xc = xc + pltpu.roll(ext, 1, 0)[lo:hi] * cw[1:2]
        xc = xc + ext[lo:hi] * cw[2:3]
        xc = xc + pltpu.roll(ext, n_ext - 1, 0)[lo:hi] * cw[3:4]
        xb = xc.astype(BF16)
        for d, (a_s, b_s) in enumerate(((af, bf), (ab, bb))):
            r = jax.nn.sigmoid(_dot(xb, gw_ref[2 * d]) + gb[2 * d:2 * d + 1])
            gi = jax.nn.sigmoid(_dot(xb, gw_ref[2 * d + 1]) + gb[2 * d + 1:2 * d + 2])
            log_a = coef[d:d + 1] * r
            a_s[r0:r1, :] = jnp.exp(log_a)
            th = jnp.tanh(log_a)
            b_s[r0:r1, :] = jnp.sqrt(-2.0 * th / (1.0 - th)) * (gi * xc)

    def make_body(f_base, b_top):
        def body(k, carry):
            cf, cr = carry
            rf = pl.multiple_of((f_base + k) * SUBLANES, SUBLANES)
            hf, cf = _tile_scan(af[pl.ds(rf, SUBLANES), :], bf[pl.ds(rf, SUBLANES), :], cf, False)
            bf[pl.ds(rf, SUBLANES), :] = hf
            rb = pl.multiple_of((b_top - k) * SUBLANES, SUBLANES)
            hb, cr = _tile_scan(ab[pl.ds(rb, SUBLANES), :], bb[pl.ds(rb, SUBLANES), :], cr, True)
            bb[pl.ds(rb, SUBLANES), :] = hb
            return cf, cr
        return body

    z1 = jnp.zeros((1, W), F32)
    n_ctx = CTX // SUBLANES
    n_all = T // SUBLANES
    carry = lax.fori_loop(0, n_ctx, make_body(0, n_ctx - 1), (z1, z1))
    lax.fori_loop(0, n_all - n_ctx, make_body(n_ctx, n_all - 1), carry)

    for blk in range(T // LRU_RB):
        r0 = blk * LRU_RB
        r1 = r0 + LRU_RB
        h = bf[r0:r1, :] + bb[r0:r1, :]
        o_ref[r0:r1, :] = (h * jax.nn.gelu(ay_ref[r0:r1, :])).astype(BF16)


def _lru(z, cw, cb, gw_bd, gb, lam):
    W = LRU_HALF
    nh = LRU_W // W
    return pl.pallas_call(
        _lru_kernel,
        grid=(B, nh),
        in_specs=[
            pl.BlockSpec((None, T, W), lambda b, c: (b, 0, c)),
            pl.BlockSpec((None, T, W), lambda b, c: (b, 0, nh + c)),
            pl.BlockSpec((LRU_TAPS, W), lambda b, c: (0, c)),
            pl.BlockSpec((1, W), lambda b, c: (0, c)),
            pl.BlockSpec((None, 4, W, W), lambda b, c: (c, 0, 0, 0)),
            pl.BlockSpec((4, W), lambda b, c: (0, c)),
            pl.BlockSpec((2, W), lambda b, c: (0, c)),
        ],
        out_specs=pl.BlockSpec((None, T, W), lambda b, c: (b, 0, c)),
        out_shape=jax.ShapeDtypeStruct((B, T, LRU_W), BF16),
        scratch_shapes=[pltpu.VMEM((T, W), F32)] * 4,
        compiler_params=_cparams(("parallel", "parallel")),
        name="lru",
    )(z, z, cw, cb, gw_bd, gb, lam)


def _attn_kernel(q_ref, k_ref, v_ref, cos_ref, sin_ref, bias_ref, o_ref,
                 qraw_s, qrot_s, qctx_s, krot_s, kc_s, v_s, sc_s, pc_s, ow_s):
    lane = lax.broadcasted_iota(jnp.int32, (1, LANES), 1)
    head0 = lane < HEAD
    second16 = ((lane // ROPE_F) % 2) == 1
    scale = HEAD ** -0.5
    W2 = 2 * GRID_W

    def rope(x, c, s):
        partner = jnp.where(second16, pltpu.roll(x, ROPE_F, 1), pltpu.roll(x, LANES - ROPE_F, 1))
        return x * c + partner * s

    def stack_heads(x):
        return jnp.concatenate([jnp.where(head0, x, 0.0), jnp.where(head0, 0.0, x)], axis=0).astype(BF16)

    def unstack_heads(y):
        n = y.shape[0] // 2
        return jnp.where(head0, y[:n], y[n:])

    qctx_s[...] = stack_heads(q_ref[0:CTX, :] * scale)
    kc_s[...] = k_ref[0:CTX, :].astype(BF16)
    v_s[0:CTX, :] = v_ref[0:CTX, :].astype(BF16)
    rows_per_blk = ATT_RB // GRID_W
    for blk in range(SEQ // ATT_RB):
        r0 = blk * ATT_RB
        r1 = r0 + ATT_RB
        c = cos_ref[r0:r1, :]
        s = sin_ref[r0:r1, :]
        q = q_ref[CTX + r0:CTX + r1, :]
        qs = q * scale
        qr = rope(q, c, s) * scale
        for j in range(rows_per_blk):
            qraw_s[blk * rows_per_blk + j] = stack_heads(qs[j * GRID_W:(j + 1) * GRID_W])
            qrot_s[blk * rows_per_blk + j] = stack_heads(qr[j * GRID_W:(j + 1) * GRID_W])
        krot_s[r0:r1, :] = rope(k_ref[CTX + r0:CTX + r1, :], c, s).astype(BF16)
        v_s[CTX + r0:CTX + r1, :] = v_ref[CTX + r0:CTX + r1, :].astype(BF16)

    s = _dot_nt(qctx_s[...], kc_s[...])
    e = jnp.exp(s - jnp.max(s, axis=-1, keepdims=True))
    p = (e / jnp.sum(e, axis=-1, keepdims=True)).astype(BF16)
    o_ref[0:CTX, :] = unstack_heads(_dot(p, v_s[0:CTX, :])).astype(BF16)

    for g in range(GRID_H // ATT_GROUP):
        g0, g1 = g * ATT_GROUP, (g + 1) * ATT_GROUP
        sc = _dot_nt(qraw_s[g0:g1].reshape(ATT_GROUP * W2, LANES), kc_s[...])
        sc_s[g0:g1] = sc.reshape(ATT_GROUP, W2, CTX)

    win = NA_ROWS * GRID_W

    def lane_fold(op, *xs):
        cols = [x[:, c:c + LANES] for x in xs for c in range(0, x.shape[1], LANES)]
        acc = cols[0]
        for col in cols[1:]:
            acc = op(acc, col)
        return acc

    def body(t, _):
        rows = [t * ATT_UNROLL + j for j in range(ATT_UNROLL)]
        starts = [jnp.clip(r - NA_ROWS // 2, 0, GRID_H - NA_ROWS) for r in rows]
        sw = [_dot_nt(qrot_s[r], krot_s[pl.ds(pl.multiple_of(rs * GRID_W, GRID_W), win), :]) + bias_ref[r - rs]
              for r, rs in zip(rows, starts)]
        sc = [sc_s[r] for r in rows]
        m = [jnp.max(lane_fold(jnp.maximum, a, c), axis=-1, keepdims=True) for a, c in zip(sw, sc)]
        ew = [jnp.exp(a - mm) for a, mm in zip(sw, m)]
        ec = [jnp.exp(c - mm) for c, mm in zip(sc, m)]
        inv = [1.0 / jnp.sum(lane_fold(jnp.add, a, c), axis=-1, keepdims=True) for a, c in zip(ew, ec)]
        for r, rs, a, c, iv in zip(rows, starts, ew, ec, inv):
            kx = pl.multiple_of(CTX + rs * GRID_W, GRID_W)
            ow_s[r] = _dot(a.astype(BF16), v_s[pl.ds(kx, win), :]) * iv
            pc_s[r] = (c * iv).astype(BF16)
        return 0

    lax.fori_loop(0, GRID_H // ATT_UNROLL, body, 0)

    for g in range(GRID_H // ATT_GROUP):
        g0, g1 = g * ATT_GROUP, (g + 1) * ATT_GROUP
        oc = _dot(pc_s[g0:g1].reshape(ATT_GROUP * W2, CTX), v_s[0:CTX, :]).reshape(ATT_GROUP, W2, LANES)
        tot = oc + ow_s[g0:g1]
        for j in range(ATT_GROUP):
            row = CTX + (g0 + j) * GRID_W
            o_ref[row:row + GRID_W, :] = unstack_heads(tot[j]).astype(BF16)


def _attn(z, cos_t, sin_t, biasmask):
    nhp = NA_W // LANES
    qb = 2 * LRU_W // LANES
    return pl.pallas_call(
        _attn_kernel,
        grid=(nhp, B),
        in_specs=[
            pl.BlockSpec((None, T, LANES), lambda p, b: (b, 0, qb + p)),
            pl.BlockSpec((None, T, LANES), lambda p, b: (b, 0, qb + nhp + p)),
            pl.BlockSpec((None, T, LANES), lambda p, b: (b, 0, qb + 2 * nhp + p)),
            pl.BlockSpec((SEQ, LANES), lambda p, b: (0, 0)),
            pl.BlockSpec((SEQ, LANES), lambda p, b: (0, 0)),
            pl.BlockSpec((None, NA_ROWS, 2 * GRID_W, NA_ROWS * GRID_W), lambda p, b: (p, 0, 0, 0)),
        ],
        out_specs=pl.BlockSpec((None, T, LANES), lambda p, b: (b, 0, p)),
        out_shape=jax.ShapeDtypeStruct((B, T, NA_W), BF16),
        scratch_shapes=[
            pltpu.VMEM((GRID_H, 2 * GRID_W, LANES), BF16),
            pltpu.VMEM((GRID_H, 2 * GRID_W, LANES), BF16),
            pltpu.VMEM((2 * CTX, LANES), BF16),
            pltpu.VMEM((SEQ, LANES), BF16),
            pltpu.VMEM((CTX, LANES), BF16),
            pltpu.VMEM((T, LANES), BF16),
            pltpu.VMEM((GRID_H, 2 * GRID_W, CTX), F32),
            pltpu.VMEM((GRID_H, 2 * GRID_W, CTX), BF16),
            pltpu.VMEM((GRID_H, 2 * GRID_W, LANES), F32),
        ],
        compiler_params=_cparams(("parallel", "parallel")),
        name="attn",
    )(z, z, z, cos_t, sin_t, biasmask)


def _sgu_kernel(u_ref, v_ref, g_ref, b_ref, ws_ref, bs_ref, o_ref):
    lane = lax.broadcasted_iota(jnp.int32, (1, LANES), 1)
    first = lane < HEAD
    for n in range(TM // SGU_CHUNK):
        r0 = n * SGU_CHUNK
        r1 = r0 + SGU_CHUNK
        v = jax.nn.gelu(v_ref[r0:r1, :])
        mu = jnp.mean(v, axis=-1, keepdims=True)
        var = jnp.mean(jnp.square(v - mu), axis=-1, keepdims=True)
        vn = ((v - mu) * lax.rsqrt(var + EPS) * g_ref[...] + b_ref[...]).astype(BF16)
        for j in range(SGU_W // LANES):
            c0 = j * LANES
            c1 = c0 + LANES
            vp = vn[:, c0:c1]
            mixed = jnp.where(first, _dot(ws_ref[2 * j], vp), _dot(ws_ref[2 * j + 1], vp)) + bs_ref[:, c0:c1]
            o_ref[r0:r1, c0:c1] = (jax.nn.gelu(u_ref[r0:r1, c0:c1]) * mixed).astype(BF16)


def _sgu(z, ln_g, ln_b, ws_bf, bs_full):
    ub = (2 * LRU_W + 3 * NA_W) // SGU_W
    return pl.pallas_call(
        _sgu_kernel,
        grid=(B, NT),
        in_specs=[
            pl.BlockSpec((None, TM, SGU_W), lambda b, i: (b, i, ub)),
            pl.BlockSpec((None, TM, SGU_W), lambda b, i: (b, i, ub + 1)),
            pl.BlockSpec((1, SGU_W), lambda b, i: (0, 0)),
            pl.BlockSpec((1, SGU_W), lambda b, i: (0, 0)),
            pl.BlockSpec((SGU_G, SGU_CHUNK, SGU_CHUNK), lambda b, i: (0, 0, 0)),
            pl.BlockSpec((SGU_CHUNK, SGU_W), lambda b, i: (0, 0)),
        ],
        out_specs=pl.BlockSpec((None, TM, SGU_W), lambda b, i: (b, i, 0)),
        out_shape=jax.ShapeDtypeStruct((B, T, SGU_W), BF16),
        compiler_params=_cparams(("parallel", "parallel")),
        name="sgu",
    )(z, z, ln_g.reshape(1, SGU_W), ln_b.reshape(1, SGU_W), ws_bf, bs_full)


def _outproj_kernel(x_ref, a_ref, b_ref, c_ref, gtb, gtc, w_ref, o_ref):
    i = pl.program_id(1)
    acc = _dot(a_ref[...], w_ref[0:LRU_W, :])
    acc = acc + _dot(b_ref[...], w_ref[LRU_W:LRU_W + NA_W, :])
    acc = acc + _dot(c_ref[...], w_ref[LRU_W + NA_W:D, :])
    gate = jnp.where(_rows_are_ctx(TM, i * TM), gtc[...], gtb[...])
    o_ref[...] = x_ref[...] + gate * acc


def _outproj(xall, oa, ob, oc, mod5, layer, w_bf):
    tn = 1024
    return pl.pallas_call(
        _outproj_kernel,
        grid=(B, NT, D // tn),
        in_specs=[
            pl.BlockSpec((None, TM, tn), lambda b, i, j: (b, i, j)),
            pl.BlockSpec((None, TM, LRU_W), lambda b, i, j: (b, i, 0)),
            pl.BlockSpec((None, TM, NA_W), lambda b, i, j: (b, i, 0)),
            pl.BlockSpec((None, TM, SGU_W), lambda b, i, j: (b, i, 0)),
            _mod_spec(layer, 2, False, tn, 2), _mod_spec(layer, 2, True, tn, 2),
            pl.BlockSpec((D, tn), lambda b, i, j: (0, j)),
        ],
        out_specs=pl.BlockSpec((None, TM, tn), lambda b, i, j: (b, i, j)),
        out_shape=jax.ShapeDtypeStruct((B, T, D), F32),
        compiler_params=_cparams(("parallel", "parallel", "parallel")),
        name="outproj",
    )(xall, oa, ob, oc, mod5, mod5, w_bf)


def _ffn_kernel(x_ref, xp_ref, xn_ref, g_ref, shb, scb, gtb, shc, scc, gtc, wa_ref, wg_ref, cw_ref, cb_ref, wd_ref,
                o_ref, hx_s, u_s, act_s):
    i = pl.program_id(1)
    k = pl.program_id(2)
    rb = FFN_RB
    left_blocks = (0, CTX // rb)
    right_blocks = (CTX // rb - 1, TM // rb - 1)

    @pl.when(k == 0)
    def _():
        _norm_rows(xp_ref, hx_s, 0, i * TM - HALO, g_ref, shb, scb, shc, scc)
        _norm_rows(x_ref, hx_s, HALO, i * TM, g_ref, shb, scb, shc, scc)
        _norm_rows(xn_ref, hx_s, HALO + TM, (i + 1) * TM, g_ref, shb, scb, shc, scc)
        o_ref[...] = jnp.zeros_like(o_ref)

    u_s[:, :TH] = _dot(hx_s[...], wa_ref[...])
    u_s[:, TH:] = _dot(hx_s[...], wg_ref[...])
    cw = cw_ref[...]
    cb = cb_ref[...]
    for blk in range(TM // rb):
        r0 = HALO + blk * rb
        rows = lax.broadcasted_iota(jnp.int32, (rb, 1), 0) + (i * TM + blk * rb)
        ul = u_s[r0 - 1:r0 - 1 + rb, :]
        ur = u_s[r0 + 1:r0 + 1 + rb, :]
        if blk in left_blocks:
            ul = jnp.where(jnp.logical_and(rows != 0, rows != CTX), ul, 0.0)
        if blk in right_blocks:
            ur = jnp.where(jnp.logical_and(rows != CTX - 1, rows != T - 1), ur, 0.0)
        y = cb + ul * cw[0:1]
        y = y + u_s[r0:r0 + rb, :] * cw[1:2]
        y = y + ur * cw[2:3]
        a = y[:, :TH]
        gg = y[:, TH:]
        act_s[blk * rb:(blk + 1) * rb, :] = (gg * jax.nn.sigmoid(gg) * a).astype(BF16)
    o_ref[...] += _dot(act_s[...], wd_ref[...])

    @pl.when(k == NK - 1)
    def _():
        gate = jnp.where(_rows_are_ctx(TM, i * TM), gtc[...], gtb[...])
        o_ref[...] = x_ref[...] + gate * o_ref[...]


def _ffn(xall, mod5, layer, g, wa, wg, cw_r, cb_r, wd_p):
    hb = TM // HALO
    last = T // HALO - 1
    col_map = lambda b, i, k: (0, k)
    return pl.pallas_call(
        _ffn_kernel,
        grid=(B, NT, NK),
        in_specs=[
            pl.BlockSpec((None, TM, D), lambda b, i, k: (b, i, 0)),
            pl.BlockSpec((None, HALO, D), lambda b, i, k: (b, jnp.maximum(i * hb - 1, 0), 0)),
            pl.BlockSpec((None, HALO, D), lambda b, i, k: (b, jnp.minimum((i + 1) * hb, last), 0)),
            pl.BlockSpec((1, D), lambda b, i, k: (0, 0)),
            _mod_spec(layer, 3, False), _mod_spec(layer, 4, False), _mod_spec(layer, 5, False),
            _mod_spec(layer, 3, True), _mod_spec(layer, 4, True), _mod_spec(layer, 5, True),
            pl.BlockSpec((D, TH), col_map),
            pl.BlockSpec((D, TH), col_map),
            pl.BlockSpec((3, 2 * TH), col_map),
            pl.BlockSpec((1, 2 * TH), col_map),
            pl.BlockSpec((TH, D), lambda b, i, k: (k, 0)),
        ],
        out_specs=pl.BlockSpec((None, TM, D), lambda b, i, k: (b, i, 0)),
        out_shape=jax.ShapeDtypeStruct((B, T, D), F32),
        scratch_shapes=[
            pltpu.VMEM((TM + 2 * HALO, D), BF16),
            pltpu.VMEM((TM + 2 * HALO, 2 * TH), F32),
            pltpu.VMEM((TM, TH), BF16),
        ],
        compiler_params=_cparams(("parallel", "parallel", "arbitrary"), VMEM_BIG),
        name="ffn",
    )(xall, xall, xall, g.reshape(1, D), mod5, mod5, mod5, mod5, mod5, mod5, wa, wg, cw_r, cb_r, wd_p)


def _final_kernel(x_ref, g_ref, o_ref):
    x = x_ref[...]
    o_ref[...] = x * lax.rsqrt(jnp.mean(x * x, axis=-1, keepdims=True) + EPS) * g_ref[...]


def _final_norm(xall, g):
    tm = CTX
    return pl.pallas_call(
        _final_kernel,
        grid=(B, SEQ // tm),
        in_specs=[
            pl.BlockSpec((None, tm, D), lambda b, i: (b, i + 1, 0)),
            pl.BlockSpec((1, D), lambda b, i: (0, 0)),
        ],
        out_specs=pl.BlockSpec((None, tm, D), lambda b, i: (b, i, 0)),
        out_shape=jax.ShapeDtypeStruct((B, SEQ, D), F32),
        compiler_params=_cparams(("parallel", "parallel")),
        name="final_norm",
    )(xall, g.reshape(1, D))


def _rope_tables():
    t = jnp.arange(SEQ)
    pos = jnp.stack([t // GRID_W, t % GRID_W], axis=-1).astype(F32)
    inv = ROPE_BASE ** (-jnp.arange(ROPE_F, dtype=F32) / ROPE_F)
    ang = pos[:, :, None] * inv
    cos, sin = jnp.cos(ang), jnp.sin(ang)
    cos_h = jnp.concatenate([cos[:, 0], cos[:, 0], cos[:, 1], cos[:, 1]], axis=-1)
    sin_h = jnp.concatenate([-sin[:, 0], sin[:, 0], -sin[:, 1], sin[:, 1]], axis=-1)
    reps = LANES // HEAD
    return jnp.tile(cos_h, (1, reps)), jnp.tile(sin_h, (1, reps))


def _bias_tables(rpb):
    cls = jnp.arange(NA_ROWS)[:, None]
    i = jnp.arange(NA_ROWS)[None, :]
    c = jnp.arange(GRID_W)[:, None]
    kc = jnp.arange(GRID_W)[None, :]
    col_start = jnp.clip(c - NA_KC // 2, 0, GRID_W - NA_KC)
    in_win = (kc >= col_start) & (kc < col_start + NA_KC)
    sel_r = ((i - cls + (NA_ROWS - 1))[..., None] == jnp.arange(2 * NA_ROWS - 1)).astype(F32)
    sel_c = (((kc - c + (NA_KC - 1))[..., None] == jnp.arange(2 * NA_KC - 1)) & in_win[..., None]).astype(F32)
    rows = jnp.einsum('hmn,xim->hxin', rpb, sel_r, precision=lax.Precision.HIGHEST)
    bias = jnp.einsum('hxin,ckn->hxcik', rows, sel_c, precision=lax.Precision.HIGHEST)
    bias = jnp.where(in_win[None, None, :, None, :], bias, NEG)
    bias = bias.reshape(NA_HEADS // 2, 2, NA_ROWS, GRID_W, NA_ROWS * GRID_W)
    return jnp.transpose(bias, (0, 2, 1, 3, 4)).reshape(NA_HEADS // 2, NA_ROWS, 2 * GRID_W, NA_ROWS * GRID_W)


def _gate_blockdiag(gate_w):
    nh = LRU_W // LRU_HALF
    gper = LRU_HALF // HEAD
    w = gate_w.reshape(4, nh, gper, HEAD, HEAD)
    eye = jnp.eye(gper, dtype=gate_w.dtype)
    bd = jnp.einsum('khgio,gj->khgijo', w, eye).reshape(4, nh, LRU_HALF, LRU_HALF)
    return jnp.transpose(bd, (1, 0, 2, 3)).astype(BF16)


def _ffn_layout(w_up, conv_w, conv_b, w_down):
    pad = FFN_HP - FFN_H

    def inter(m):
        r = m.shape[0]
        a = jnp.pad(m[:, :FFN_H], ((0, 0), (0, pad))).reshape(r, NK, 1, TH)
        g = jnp.pad(m[:, FFN_H:], ((0, 0), (0, pad))).reshape(r, NK, 1, TH)
        return jnp.concatenate([a, g], axis=2).reshape(r, NK * 2 * TH)

    wa = jnp.pad(w_up[:, :FFN_H].astype(BF16), ((0, 0), (0, pad)))
    wg = jnp.pad(w_up[:, FFN_H:].astype(BF16), ((0, 0), (0, pad)))
    wd = jnp.pad(w_down.astype(BF16), ((0, pad), (0, 0)))
    return wa, wg, inter(conv_w), inter(conv_b.reshape(1, -1)), wd


def kernel(x, c, ctx, c_ctx, w_ada, b_ada, norm_mix_g, norm_ffn_g, w_in, lru_conv_w, lru_conv_b, lru_gate_w, lru_gate_b, lru_lambda, na_rpb, sgu_ln_g, sgu_ln_b, sgu_w, sgu_b, w_out, ffn_up, ffn_conv_w, ffn_conv_b, ffn_down, final_norm_g):
    xall = jnp.concatenate([ctx, x], axis=1)
    cc = jnp.concatenate([c, c_ctx[None], jnp.zeros((MOD_ROWS - B - 1, D), F32)], axis=0)
    mod5 = _adaln(cc, w_ada, b_ada).reshape(DEPTH, 6, MOD_ROWS, 1, D)
    cos_t, sin_t = _rope_tables()

    for l in range(DEPTH):
        z = _inproj(xall, mod5, l, norm_mix_g[l], w_in[l].astype(BF16))
        oa = _lru(z, lru_conv_w[l], lru_conv_b[l].reshape(1, LRU_W), _gate_blockdiag(lru_gate_w[l]),
                  lru_gate_b[l].reshape(4, LRU_W), lru_lambda[l])
        ob = _attn(z, cos_t, sin_t, _bias_tables(na_rpb[l]))
        bs_full = jnp.repeat(sgu_b[l].T, HEAD, axis=1)
        oc = _sgu(z, sgu_ln_g[l], sgu_ln_b[l], sgu_w[l].astype(BF16), bs_full)
        xall = _outproj(xall, oa, ob, oc, mod5, l, w_out[l].astype(BF16))
        xall = _ffn(xall, mod5, l, norm_ffn_g[l], *_ffn_layout(ffn_up[l], ffn_conv_w[l], ffn_conv_b[l], ffn_down[l]))
    return _final_norm(xall, final_norm_g)
```

```python
import functools

import jax
import jax.numpy as jnp
from jax import lax
from jax.experimental import pallas as pl
from jax.experimental.pallas import tpu as pltpu

F32 = jnp.float32
BF16 = jnp.bfloat16

D = 2048
B = 8
SEQ = 2048
DEPTH = 4
GRID_W = 64
GRID_H = SEQ // GRID_W
CTX = 256
T = CTX + SEQ
HEAD = 64
LRU_W = 512
LRU_TAPS = 4
LRU_C = 8.0
NA_W = 1024
NA_HEADS = NA_W // HEAD
NA_ROWS = 8
NA_KC = 16
SGU_W = 512
SGU_G = SGU_W // HEAD
SGU_CHUNK = 128
PROJ = 2 * LRU_W + 3 * NA_W + 2 * SGU_W
FFN_H = 5504
EPS = 1e-6
NEG = -1e30
ROPE_BASE = 10000.0
ROPE_F = HEAD // 4
MOD_ROWS = 16
CTX_ROW = B

LANES = 128
SUBLANES = 8
VMEM_BIG = 56 * 1024 * 1024
VMEM_MID = 40 * 1024 * 1024

TM = 768
NT = T // TM
HALO = 16
TH = 512
FFN_HP = 5632
NK = FFN_HP // TH
LRU_HALF = 256
LRU_RB = 128
ATT_RB = 256
NORM_RB = 16
NORM_UNROLL = 4
FFN_RB = 128
ATT_UNROLL = 4
ATT_GROUP = 4


def _cparams(sem, vmem=VMEM_MID):
    return pltpu.CompilerParams(dimension_semantics=sem, vmem_limit_bytes=vmem)


def _dot(a, b):
    return jnp.dot(a, b, preferred_element_type=F32)


def _dot_nt(a, b):
    return lax.dot_general(a, b, (((1,), (1,)), ((), ())), preferred_element_type=F32)


def _mod_spec(layer, which, ctx, width=D, col=None):
    def imap(*ids):
        b = ids[0]
        row = CTX_ROW if ctx else b
        c = 0 if col is None else ids[col]
        return (layer, which, row, 0, c)
    return pl.BlockSpec((None, None, None, 1, width), imap)


def _per_segment(i, fn, latent_arg, ctx_arg):
    @pl.when(i == 0)
    def _():
        fn(0, CTX, ctx_arg)
        fn(CTX, TM, latent_arg)

    @pl.when(i != 0)
    def _():
        fn(0, TM, latent_arg)


def _norm_rows(x_ref, dst_ref, dst_off, lo, hi, gain, shift):
    def body(j, _):
        r = pl.multiple_of(lo + j * NORM_RB, NORM_RB)
        x = x_ref[pl.ds(r, NORM_RB), :]
        inv = lax.rsqrt(jnp.mean(x * x, axis=-1, keepdims=True) + EPS)
        dst_ref[pl.ds(dst_off + r, NORM_RB), :] = ((x * inv) * gain + shift).astype(BF16)
        return 0
    n = (hi - lo) // NORM_RB
    lax.fori_loop(0, n, body, 0, unroll=min(n, NORM_UNROLL))


def _norm_tile(i, x_ref, dst_ref, dst_off, g_ref, shb, scb, shc, scc):
    g = g_ref[...]

    def run(lo, hi, mod):
        _norm_rows(x_ref, dst_ref, dst_off, lo, hi, g * (1.0 + mod[0]), mod[1])
    _per_segment(i, run, (scb[...], shb[...]), (scc[...], shc[...]))


def _adaln_kernel(c_ref, w_ref, b_ref, o_ref):
    c = c_ref[...]
    s = (c * jax.nn.sigmoid(c)).astype(BF16)
    o_ref[...] = _dot(s, w_ref[...].astype(BF16)) + b_ref[...]


def _adaln(cc, w_ada, b_ada):
    tn = 1024
    nj = D // tn
    return pl.pallas_call(
        _adaln_kernel,
        grid=(DEPTH, 6 * nj),
        in_specs=[
            pl.BlockSpec((MOD_ROWS, D), lambda l, j: (0, 0)),
            pl.BlockSpec((None, D, tn), lambda l, j: (l, 0, j)),
            pl.BlockSpec((None, 1, tn), lambda l, j: (l, 0, j)),
        ],
        out_specs=pl.BlockSpec((None, None, MOD_ROWS, tn), lambda l, j: (l, j // nj, 0, j % nj)),
        out_shape=jax.ShapeDtypeStruct((DEPTH, 6, MOD_ROWS, D), F32),
        compiler_params=_cparams(("parallel", "parallel")),
        name="adaln",
    )(cc, w_ada, b_ada.reshape(DEPTH, 1, 6 * D))


def _inproj_kernel(x_ref, g_ref, shb, scb, shc, scc, w_ref, o_ref, hx_s):
    i = pl.program_id(1)

    @pl.when(pl.program_id(2) == 0)
    def _():
        _norm_tile(i, x_ref, hx_s, 0, g_ref, shb, scb, shc, scc)

    o_ref[...] = _dot(hx_s[...], w_ref[...])


def _inproj(xall, mod5, layer, g, w_bf):
    tn = 1024
    return pl.pallas_call(
        _inproj_kernel,
        grid=(B, NT, PROJ // tn),
        in_specs=[
            pl.BlockSpec((None, TM, D), lambda b, i, j: (b, i, 0)),
            pl.BlockSpec((1, D), lambda b, i, j: (0, 0)),
            _mod_spec(layer, 0, False), _mod_spec(layer, 1, False),
            _mod_spec(layer, 0, True), _mod_spec(layer, 1, True),
            pl.BlockSpec((D, tn), lambda b, i, j: (0, j)),
        ],
        out_specs=pl.BlockSpec((None, TM, tn), lambda b, i, j: (b, i, j)),
        out_shape=jax.ShapeDtypeStruct((B, T, PROJ), F32),
        scratch_shapes=[pltpu.VMEM((TM, D), BF16)],
        compiler_params=_cparams(("parallel", "parallel", "arbitrary")),
        name="inproj",
    )(xall, g.reshape(1, D), mod5, mod5, mod5, mod5, w_bf)


def _tile_scan(a, b, carry, rev):
    rid = lax.broadcasted_iota(jnp.int32, a.shape, 0)
    for d in (1, 2, 4):
        sh = SUBLANES - d if rev else d
        keep = (rid < SUBLANES - d) if rev else (rid >= d)
        a_s = jnp.where(keep, pltpu.roll(a, sh, 0), 1.0)
        b_s = jnp.where(keep, pltpu.roll(b, sh, 0), 0.0)
        b = a * b_s + b
        a = a * a_s
    h = a * carry + b
    return h, (h[0:1] if rev else h[SUBLANES - 1:SUBLANES])


def _lru_kernel(ax_ref, ay_ref, cw_ref, cb_ref, gw_ref, gb_ref, lam_ref, o_ref, af, bf, ab, bb):
    W = LRU_HALF
    cw = cw_ref[...]
    cb = cb_ref[...]
    gb = gb_ref[...]
    nl = -lam_ref[...]
    softplus = jnp.maximum(nl, 0.0) + jnp.log(1.0 + jnp.exp(-jnp.abs(nl)))
    coef = -LRU_C * softplus
    zeros8 = jnp.zeros((SUBLANES, W), F32)
    n_ext = LRU_RB + 2 * SUBLANES

    for blk in range(T // LRU_RB):
        r0 = blk * LRU_RB
        r1 = r0 + LRU_RB
        prev = zeros8 if r0 in (0, CTX) else ax_ref[r0 - SUBLANES:r0, :]
        nxt = zeros8 if r1 in (CTX, T) else ax_ref[r1:r1 + SUBLANES, :]
        ext = jnp.concatenate([prev, ax_ref[r0:r1, :], nxt], axis=0)
        lo, hi = SUBLANES, SUBLANES + LRU_RB
        xc = cb + pltpu.roll(ext, 2, 0)[lo:hi] * cw[0:1]
        xc = xc + pltpu.roll(ext, 1, 0)[lo:hi] * cw[1:2]
        xc = xc + ext[lo:hi] * cw[2:3]
        xc = xc + pltpu.roll(ext, n_ext - 1, 0)[lo:hi] * cw[3:4]
        xb = xc.astype(BF16)
        for d, (a_s, b_s) in enumerate(((af, bf), (ab, bb))):
            r = jax.nn.sigmoid(_dot(xb, gw_ref[2 * d]) + gb[2 * d:2 * d + 1])
            gi = jax.nn.sigmoid(_dot(xb, gw_ref[2 * d + 1]) + gb[2 * d + 1:2 * d + 2])
            log_a = coef[d:d + 1] * r
            a_s[r0:r1, :] = jnp.exp(log_a)
            th = jnp.tanh(log_a)
            b_s[r0:r1, :] = jnp.sqrt(-2.0 * th / (1.0 - th)) * (gi * xc)

    def make_body(f_base, b_top):
        def body(k, carry):
            cf, cr = carry
            rf = pl.multiple_of((f_base + k) * SUBLANES, SUBLANES)
            hf, cf = _tile_scan(af[pl.ds(rf, SUBLANES), :], bf[pl.ds(rf, SUBLANES), :], cf, False)
            bf[pl.ds(rf, SUBLANES), :] = hf
            rb = pl.multiple_of((b_top - k) * SUBLANES, SUBLANES)
            hb, cr = _tile_scan(ab[pl.ds(rb, SUBLANES), :], bb[pl.ds(rb, SUBLANES), :], cr, True)
            bb[pl.ds(rb, SUBLANES), :] = hb
            return cf, cr
        return body

    z1 = jnp.zeros((1, W), F32)
    n_ctx = CTX // SUBLANES
    n_all = T // SUBLANES
    carry = lax.fori_loop(0, n_ctx, make_body(0, n_ctx - 1), (z1, z1))
    lax.fori_loop(0, n_all - n_ctx, make_body(n_ctx, n_all - 1), carry)

    for blk in range(T // LRU_RB):
        r0 = blk * LRU_RB
        r1 = r0 + LRU_RB
        h = bf[r0:r1, :] + bb[r0:r1, :]
        o_ref[r0:r1, :] = (h * jax.nn.gelu(ay_ref[r0:r1, :])).astype(BF16)


def _lru(z, cw, cb, gw_bd, gb, lam):
    W = LRU_HALF
    nh = LRU_W // W
    return pl.pallas_call(
        _lru_kernel,
        grid=(B, nh),
        in_specs=[
            pl.BlockSpec((None, T, W), lambda b, c: (b, 0, c)),
            pl.BlockSpec((None, T, W), lambda b, c: (b, 0, nh + c)),
            pl.BlockSpec((LRU_TAPS, W), lambda b, c: (0, c)),
            pl.BlockSpec((1, W), lambda b, c: (0, c)),
            pl.BlockSpec((None, 4, W, W), lambda b, c: (c, 0, 0, 0)),
            pl.BlockSpec((4, W), lambda b, c: (0, c)),
            pl.BlockSpec((2, W), lambda b, c: (0, c)),
        ],
        out_specs=pl.BlockSpec((None, T, W), lambda b, c: (b, 0, c)),
        out_shape=jax.ShapeDtypeStruct((B, T, LRU_W), BF16),
        scratch_shapes=[pltpu.VMEM((T, W), F32)] * 4,
        compiler_params=_cparams(("parallel", "parallel")),
        name="lru",
    )(z, z, cw, cb, gw_bd, gb, lam)


def _attn_kernel(q_ref, k_ref, v_ref, cos_ref, sin_ref, bias_ref, o_ref,
                 qraw_s, qrot_s, qctx_s, krot_s, kc_s, v_s, sc_s, pc_s, ow_s):
    lane = lax.broadcasted_iota(jnp.int32, (1, LANES), 1)
    head0 = lane < HEAD
    second16 = ((lane // ROPE_F) % 2) == 1
    scale = HEAD ** -0.5
    W2 = 2 * GRID_W

    def rope(x, c, s):
        partner = jnp.where(second16, pltpu.roll(x, ROPE_F, 1), pltpu.roll(x, LANES - ROPE_F, 1))
        return x * c + partner * s

    def stack_heads(x):
        return jnp.concatenate([jnp.where(head0, x, 0.0), jnp.where(head0, 0.0, x)], axis=0).astype(BF16)

    def unstack_heads(y):
        n = y.shape[0] // 2
        return jnp.where(head0, y[:n], y[n:])

    qctx_s[...] = stack_heads(q_ref[0:CTX, :] * scale)
    kc_s[...] = k_ref[0:CTX, :].astype(BF16)
    v_s[0:CTX, :] = v_ref[0:CTX, :].astype(BF16)
    rows_per_blk = ATT_RB // GRID_W
    for blk in range(SEQ // ATT_RB):
        r0 = blk * ATT_RB
        r1 = r0 + ATT_RB
        c = cos_ref[r0:r1, :]
        s = sin_ref[r0:r1, :]
        q = q_ref[CTX + r0:CTX + r1, :]
        qs = q * scale
        qr = rope(q, c, s) * scale
        for j in range(rows_per_blk):
            qraw_s[blk * rows_per_blk + j] = stack_heads(qs[j * GRID_W:(j + 1) * GRID_W])
            qrot_s[blk * rows_per_blk + j] = stack_heads(qr[j * GRID_W:(j + 1) * GRID_W])
        krot_s[r0:r1, :] = rope(k_ref[CTX + r0:CTX + r1, :], c, s).astype(BF16)
        v_s[CTX + r0:CTX + r1, :] = v_ref[CTX + r0:CTX + r1, :].astype(BF16)

    s = _dot_nt(qctx_s[...], kc_s[...])
    e = jnp.exp(s - jnp.max(s, axis=-1, keepdims=True))
    p = (e / jnp.sum(e, axis=-1, keepdims=True)).astype(BF16)
    o_ref[0:CTX, :] = unstack_heads(_dot(p, v_s[0:CTX, :])).astype(BF16)

    for g in range(GRID_H // ATT_GROUP):
        g0, g1 = g * ATT_GROUP, (g + 1) * ATT_GROUP
        sc = _dot_nt(qraw_s[g0:g1].reshape(ATT_GROUP * W2, LANES), kc_s[...])
        sc_s[g0:g1] = sc.reshape(ATT_GROUP, W2, CTX)

    win = NA_ROWS * GRID_W

    def lane_fold(op, *xs):
        cols = [x[:, c:c + LANES] for x in xs for c in range(0, x.shape[1], LANES)]
        acc = cols[0]
        for col in cols[1:]:
            acc = op(acc, col)
        return acc

    def body(t, _):
        rows = [t * ATT_UNROLL + j for j in range(ATT_UNROLL)]
        starts = [jnp.clip(r - NA_ROWS // 2, 0, GRID_H - NA_ROWS) for r in rows]
        sw = [_dot_nt(qrot_s[r], krot_s[pl.ds(pl.multiple_of(rs * GRID_W, GRID_W), win), :]) + bias_ref[r - rs]
              for r, rs in zip(rows, starts)]
        sc = [sc_s[r] for r in rows]
        m = [jnp.max(lane_fold(jnp.maximum, a, c), axis=-1, keepdims=True) for a, c in zip(sw, sc)]
        ew = [jnp.exp(a - mm) for a, mm in zip(sw, m)]
        ec = [jnp.exp(c - mm) for c, mm in zip(sc, m)]
        inv = [1.0 / jnp.sum(lane_fold(jnp.add, a, c), axis=-1, keepdims=True) for a, c in zip(ew, ec)]
        for r, rs, a, c, iv in zip(rows, starts, ew, ec, inv):
            kx = pl.multiple_of(CTX + rs * GRID_W, GRID_W)
            ow_s[r] = _dot(a.astype(BF16), v_s[pl.ds(kx, win), :]) * iv
            pc_s[r] = (c * iv).astype(BF16)
        return 0

    lax.fori_loop(0, GRID_H // ATT_UNROLL, body, 0)

    for g in range(GRID_H // ATT_GROUP):
        g0, g1 = g * ATT_GROUP, (g + 1) * ATT_GROUP
        oc = _dot(pc_s[g0:g1].reshape(ATT_GROUP * W2, CTX), v_s[0:CTX, :]).reshape(ATT_GROUP, W2, LANES)
        tot = oc + ow_s[g0:g1]
        for j in range(ATT_GROUP):
            row = CTX + (g0 + j) * GRID_W
            o_ref[row:row + GRID_W, :] = unstack_heads(tot[j]).astype(BF16)


def _attn(z, cos_t, sin_t, biasmask):
    nhp = NA_W // LANES
    qb = 2 * LRU_W // LANES
    return pl.pallas_call(
        _attn_kernel,
        grid=(nhp, B),
        in_specs=[
            pl.BlockSpec((None, T, LANES), lambda p, b: (b, 0, qb + p)),
            pl.BlockSpec((None, T, LANES), lambda p, b: (b, 0, qb + nhp + p)),
            pl.BlockSpec((None, T, LANES), lambda p, b: (b, 0, qb + 2 * nhp + p)),
            pl.BlockSpec((SEQ, LANES), lambda p, b: (0, 0)),
            pl.BlockSpec((SEQ, LANES), lambda p, b: (0, 0)),
            pl.BlockSpec((None, NA_ROWS, 2 * GRID_W, NA_ROWS * GRID_W), lambda p, b: (p, 0, 0, 0)),
        ],
        out_specs=pl.BlockSpec((None, T, LANES), lambda p, b: (b, 0, p)),
        out_shape=jax.ShapeDtypeStruct((B, T, NA_W), BF16),
        scratch_shapes=[
            pltpu.VMEM((GRID_H, 2 * GRID_W, LANES), BF16),
            pltpu.VMEM((GRID_H, 2 * GRID_W, LANES), BF16),
            pltpu.VMEM((2 * CTX, LANES), BF16),
            pltpu.VMEM((SEQ, LANES), BF16),
            pltpu.VMEM((CTX, LANES), BF16),
            pltpu.VMEM((T, LANES), BF16),
            pltpu.VMEM((GRID_H, 2 * GRID_W, CTX), F32),
            pltpu.VMEM((GRID_H, 2 * GRID_W, CTX), BF16),
            pltpu.VMEM((GRID_H, 2 * GRID_W, LANES), F32),
        ],
        compiler_params=_cparams(("parallel", "parallel")),
        name="attn",
    )(z, z, z, cos_t, sin_t, biasmask)


def _sgu_kernel(u_ref, v_ref, g_ref, b_ref, ws_ref, bs_ref, o_ref):
    lane = lax.broadcasted_iota(jnp.int32, (1, LANES), 1)
    first = lane < HEAD
    for n in range(TM // SGU_CHUNK):
        r0 = n * SGU_CHUNK
        r1 = r0 + SGU_CHUNK
        v = jax.nn.gelu(v_ref[r0:r1, :])
        mu = jnp.mean(v, axis=-1, keepdims=True)
        var = jnp.mean(jnp.square(v - mu), axis=-1, keepdims=True)
        vn = ((v - mu) * lax.rsqrt(var + EPS) * g_ref[...] + b_ref[...]).astype(BF16)
        for j in range(SGU_W // LANES):
            c0 = j * LANES
            c1 = c0 + LANES
            vp = vn[:, c0:c1]
            mixed = jnp.where(first, _dot(ws_ref[2 * j], vp), _dot(ws_ref[2 * j + 1], vp)) + bs_ref[:, c0:c1]
            o_ref[r0:r1, c0:c1] = (jax.nn.gelu(u_ref[r0:r1, c0:c1]) * mixed).astype(BF16)


def _sgu(z, ln_g, ln_b, ws_bf, bs_full):
    ub = (2 * LRU_W + 3 * NA_W) // SGU_W
    return pl.pallas_call(
        _sgu_kernel,
        grid=(B, NT),
        in_specs=[
            pl.BlockSpec((None, TM, SGU_W), lambda b, i: (b, i, ub)),
            pl.BlockSpec((None, TM, SGU_W), lambda b, i: (b, i, ub + 1)),
            pl.BlockSpec((1, SGU_W), lambda b, i: (0, 0)),
            pl.BlockSpec((1, SGU_W), lambda b, i: (0, 0)),
            pl.BlockSpec((SGU_G, SGU_CHUNK, SGU_CHUNK), lambda b, i: (0, 0, 0)),
            pl.BlockSpec((SGU_CHUNK, SGU_W), lambda b, i: (0, 0)),
        ],
        out_specs=pl.BlockSpec((None, TM, SGU_W), lambda b, i: (b, i, 0)),
        out_shape=jax.ShapeDtypeStruct((B, T, SGU_W), BF16),
        compiler_params=_cparams(("parallel", "parallel")),
        name="sgu",
    )(z, z, ln_g.reshape(1, SGU_W), ln_b.reshape(1, SGU_W), ws_bf, bs_full)


def _outproj_kernel(x_ref, a_ref, b_ref, c_ref, gtb, gtc, w_ref, o_ref):
    i = pl.program_id(1)
    acc = _dot(a_ref[...], w_ref[0:LRU_W, :])
    acc = acc + _dot(b_ref[...], w_ref[LRU_W:LRU_W + NA_W, :])
    o_ref[...] = acc + _dot(c_ref[...], w_ref[LRU_W + NA_W:D, :])

    def finish(lo, hi, gate):
        o_ref[lo:hi, :] = x_ref[lo:hi, :] + gate * o_ref[lo:hi, :]
    _per_segment(i, finish, gtb[...], gtc[...])


def _outproj(xall, oa, ob, oc, mod5, layer, w_bf):
    tn = 1024
    return pl.pallas_call(
        _outproj_kernel,
        grid=(B, NT, D // tn),
        in_specs=[
            pl.BlockSpec((None, TM, tn), lambda b, i, j: (b, i, j)),
            pl.BlockSpec((None, TM, LRU_W), lambda b, i, j: (b, i, 0)),
            pl.BlockSpec((None, TM, NA_W), lambda b, i, j: (b, i, 0)),
            pl.BlockSpec((None, TM, SGU_W), lambda b, i, j: (b, i, 0)),
            _mod_spec(layer, 2, False, tn, 2), _mod_spec(layer, 2, True, tn, 2),
            pl.BlockSpec((D, tn), lambda b, i, j: (0, j)),
        ],
        out_specs=pl.BlockSpec((None, TM, tn), lambda b, i, j: (b, i, j)),
        out_shape=jax.ShapeDtypeStruct((B, T, D), F32),
        compiler_params=_cparams(("parallel", "parallel", "parallel")),
        name="outproj",
    )(xall, oa, ob, oc, mod5, mod5, w_bf)


def _ffn_kernel(x_ref, xp_ref, xn_ref, g_ref, shb, scb, gtb, shc, scc, gtc, wa_ref, wg_ref, cw_ref, cb_ref, wd_ref,
                o_ref, hx_s, u_s, act_s):
    i = pl.program_id(1)
    k = pl.program_id(2)
    rb = FFN_RB
    left_blocks = (0, CTX // rb)
    right_blocks = (CTX // rb - 1, TM // rb - 1)

    @pl.when(k == 0)
    def _():
        gain_b = g_ref[...] * (1.0 + scb[...])
        _norm_rows(xp_ref, hx_s, 0, 0, HALO, gain_b, shb[...])
        _norm_rows(xn_ref, hx_s, HALO + TM, 0, HALO, gain_b, shb[...])
        _norm_tile(i, x_ref, hx_s, HALO, g_ref, shb, scb, shc, scc)
        o_ref[...] = jnp.zeros_like(o_ref)

    u_s[:, :TH] = _dot(hx_s[...], wa_ref[...])
    u_s[:, TH:] = _dot(hx_s[...], wg_ref[...])
    cw = cw_ref[...]
    cb = cb_ref[...]
    for blk in range(TM // rb):
        r0 = HALO + blk * rb
        rows = lax.broadcasted_iota(jnp.int32, (rb, 1), 0) + (i * TM + blk * rb)
        ul = u_s[r0 - 1:r0 - 1 + rb, :]
        ur = u_s[r0 + 1:r0 + 1 + rb, :]
        if blk in left_blocks:
            ul = jnp.where(jnp.logical_and(rows != 0, rows != CTX), ul, 0.0)
        if blk in right_blocks:
            ur = jnp.where(jnp.logical_and(rows != CTX - 1, rows != T - 1), ur, 0.0)
        y = cb + ul * cw[0:1]
        y = y + u_s[r0:r0 + rb, :] * cw[1:2]
        y = y + ur * cw[2:3]
        a = y[:, :TH]
        gg = y[:, TH:]
        act_s[blk * rb:(blk + 1) * rb, :] = (gg * jax.nn.sigmoid(gg) * a).astype(BF16)
    o_ref[...] += _dot(act_s[...], wd_ref[...])

    @pl.when(k == NK - 1)
    def _():
        def finish(lo, hi, gate):
            o_ref[lo:hi, :] = x_ref[lo:hi, :] + gate * o_ref[lo:hi, :]
        _per_segment(i, finish, gtb[...], gtc[...])


def _ffn(xall, mod5, layer, g, wa, wg, cw_r, cb_r, wd_p):
    hb = TM // HALO
    last = T // HALO - 1
    col_map = lambda b, i, k: (0, k)
    return pl.pallas_call(
        _ffn_kernel,
        grid=(B, NT, NK),
        in_specs=[
            pl.BlockSpec((None, TM, D), lambda b, i, k: (b, i, 0)),
            pl.BlockSpec((None, HALO, D), lambda b, i, k: (b, jnp.maximum(i * hb - 1, 0), 0)),
            pl.BlockSpec((None, HALO, D), lambda b, i, k: (b, jnp.minimum((i + 1) * hb, last), 0)),
            pl.BlockSpec((1, D), lambda b, i, k: (0, 0)),
            _mod_spec(layer, 3, False), _mod_spec(layer, 4, False), _mod_spec(layer, 5, False),
            _mod_spec(layer, 3, True), _mod_spec(layer, 4, True), _mod_spec(layer, 5, True),
            pl.BlockSpec((D, TH), col_map),
            pl.BlockSpec((D, TH), col_map),
            pl.BlockSpec((3, 2 * TH), col_map),
            pl.BlockSpec((1, 2 * TH), col_map),
            pl.BlockSpec((TH, D), lambda b, i, k: (k, 0)),
        ],
        out_specs=pl.BlockSpec((None, TM, D), lambda b, i, k: (b, i, 0)),
        out_shape=jax.ShapeDtypeStruct((B, T, D), F32),
        scratch_shapes=[
            pltpu.VMEM((TM + 2 * HALO, D), BF16),
            pltpu.VMEM((TM + 2 * HALO, 2 * TH), F32),
            pltpu.VMEM((TM, TH), BF16),
        ],
        compiler_params=_cparams(("parallel", "parallel", "arbitrary"), VMEM_BIG),
        name="ffn",
    )(xall, xall, xall, g.reshape(1, D), mod5, mod5, mod5, mod5, mod5, mod5, wa, wg, cw_r, cb_r, wd_p)


def _final_kernel(x_ref, g_ref, o_ref):
    x = x_ref[...]
    o_ref[...] = x * lax.rsqrt(jnp.mean(x * x, axis=-1, keepdims=True) + EPS) * g_ref[...]


def _final_norm(xall, g):
    tm = CTX
    return pl.pallas_call(
        _final_kernel,
        grid=(B, SEQ // tm),
        in_specs=[
            pl.BlockSpec((None, tm, D), lambda b, i: (b, i + 1, 0)),
            pl.BlockSpec((1, D), lambda b, i: (0, 0)),
        ],
        out_specs=pl.BlockSpec((None, tm, D), lambda b, i: (b, i, 0)),
        out_shape=jax.ShapeDtypeStruct((B, SEQ, D), F32),
        compiler_params=_cparams(("parallel", "parallel")),
        name="final_norm",
    )(xall, g.reshape(1, D))


def _rope_tables():
    t = jnp.arange(SEQ)
    pos = jnp.stack([t // GRID_W, t % GRID_W], axis=-1).astype(F32)
    inv = ROPE_BASE ** (-jnp.arange(ROPE_F, dtype=F32) / ROPE_F)
    ang = pos[:, :, None] * inv
    cos, sin = jnp.cos(ang), jnp.sin(ang)
    cos_h = jnp.concatenate([cos[:, 0], cos[:, 0], cos[:, 1], cos[:, 1]], axis=-1)
    sin_h = jnp.concatenate([-sin[:, 0], sin[:, 0], -sin[:, 1], sin[:, 1]], axis=-1)
    reps = LANES // HEAD
    return jnp.tile(cos_h, (1, reps)), jnp.tile(sin_h, (1, reps))


def _bias_tables(rpb):
    cls = jnp.arange(NA_ROWS)[:, None]
    i = jnp.arange(NA_ROWS)[None, :]
    c = jnp.arange(GRID_W)[:, None]
    kc = jnp.arange(GRID_W)[None, :]
    col_start = jnp.clip(c - NA_KC // 2, 0, GRID_W - NA_KC)
    in_win = (kc >= col_start) & (kc < col_start + NA_KC)
    sel_r = ((i - cls + (NA_ROWS - 1))[..., None] == jnp.arange(2 * NA_ROWS - 1)).astype(F32)
    sel_c = (((kc - c + (NA_KC - 1))[..., None] == jnp.arange(2 * NA_KC - 1)) & in_win[..., None]).astype(F32)
    pairs = rpb.reshape(NA_HEADS // 2, 2, 2 * NA_ROWS - 1, 2 * NA_KC - 1)
    rows = jnp.einsum('pamn,xim->pxain', pairs, sel_r, precision=lax.Precision.HIGHEST)
    bias = jnp.einsum('pxain,ckn->pxacik', rows, sel_c, precision=lax.Precision.HIGHEST)
    bias = jnp.where(in_win[None, None, None, :, None, :], bias, NEG)
    return bias.reshape(NA_HEADS // 2, NA_ROWS, 2 * GRID_W, NA_ROWS * GRID_W)


def _gate_blockdiag(gate_w):
    nh = LRU_W // LRU_HALF
    gper = LRU_HALF // HEAD
    w = gate_w.reshape(4, nh, gper, HEAD, HEAD)
    eye = jnp.eye(gper, dtype=gate_w.dtype)
    bd = jnp.einsum('khgio,gj->khgijo', w, eye).reshape(4, nh, LRU_HALF, LRU_HALF)
    return jnp.transpose(bd, (1, 0, 2, 3)).astype(BF16)


def _ffn_layout(w_up, conv_w, conv_b, w_down):
    pad = FFN_HP - FFN_H

    def inter(m):
        r = m.shape[0]
        a = jnp.pad(m[:, :FFN_H], ((0, 0), (0, pad))).reshape(r, NK, 1, TH)
        g = jnp.pad(m[:, FFN_H:], ((0, 0), (0, pad))).reshape(r, NK, 1, TH)
        return jnp.concatenate([a, g], axis=2).reshape(r, NK * 2 * TH)

    wa = jnp.pad(w_up[:, :FFN_H].astype(BF16), ((0, 0), (0, pad)))
    wg = jnp.pad(w_up[:, FFN_H:].astype(BF16), ((0, 0), (0, pad)))
    wd = jnp.pad(w_down.astype(BF16), ((0, pad), (0, 0)))
    return wa, wg, inter(conv_w), inter(conv_b.reshape(1, -1)), wd


def kernel(x, c, ctx, c_ctx, w_ada, b_ada, norm_mix_g, norm_ffn_g, w_in, lru_conv_w, lru_conv_b, lru_gate_w, lru_gate_b, lru_lambda, na_rpb, sgu_ln_g, sgu_ln_b, sgu_w, sgu_b, w_out, ffn_up, ffn_conv_w, ffn_conv_b, ffn_down, final_norm_g):
    xall = jnp.concatenate([ctx, x], axis=1)
    cc = jnp.concatenate([c, c_ctx[None], jnp.zeros((MOD_ROWS - B - 1, D), F32)], axis=0)
    mod5 = _adaln(cc, w_ada, b_ada).reshape(DEPTH, 6, MOD_ROWS, 1, D)
    cos_t, sin_t = _rope_tables()

    for l in range(DEPTH):
        z = _inproj(xall, mod5, l, norm_mix_g[l], w_in[l].astype(BF16))
        oa = _lru(z, lru_conv_w[l], lru_conv_b[l].reshape(1, LRU_W), _gate_blockdiag(lru_gate_w[l]),
                  lru_gate_b[l].reshape(4, LRU_W), lru_lambda[l])
        ob = _attn(z, cos_t, sin_t, _bias_tables(na_rpb[l]))
        bs_full = jnp.repeat(sgu_b[l].T, HEAD, axis=1)
        oc = _sgu(z, sgu_ln_g[l], sgu_ln_b[l], sgu_w[l].astype(BF16), bs_full)
        xall = _outproj(xall, oa, ob, oc, mod5, l, w_out[l].astype(BF16))
        xall = _ffn(xall, mod5, l, norm_ffn_g[l], *_ffn_layout(ffn_up[l], ffn_conv_w[l], ffn_conv_b[l], ffn_down[l]))
    return _final_norm(xall, final_norm_g)
```

```python
import functools

import jax
import jax.numpy as jnp
from jax import lax
from jax.experimental import pallas as pl
from jax.experimental.pallas import tpu as pltpu

F32 = jnp.float32
BF16 = jnp.bfloat16

D = 2048
B = 8
SEQ = 2048
DEPTH = 4
GRID_W = 64
GRID_H = SEQ // GRID_W
CTX = 256
T = CTX + SEQ
HEAD = 64
LRU_W = 512
LRU_TAPS = 4
LRU_C = 8.0
NA_W = 1024
NA_HEADS = NA_W // HEAD
NA_ROWS = 8
NA_KC = 16
SGU_W = 512
SGU_G = SGU_W // HEAD
SGU_CHUNK = 128
PROJ = 2 * LRU_W + 3 * NA_W + 2 * SGU_W
FFN_H = 5504
EPS = 1e-6
NEG = -1e30
LOG2E = 1.4426950408889634
ROPE_BASE = 10000.0
ROPE_F = HEAD // 4
MOD_ROWS = 16
CTX_ROW = B

LANES = 128
SUBLANES = 8
VMEM_BIG = 56 * 1024 * 1024
VMEM_MID = 40 * 1024 * 1024

TM = 768
NT = T // TM
HALO = 16
TH = 512
FFN_HP = 5632
NK = FFN_HP // TH
LRU_HALF = 256
LRU_RB = 128
ATT_RB = 256
NORM_RB = 16
NORM_UNROLL = 4
FFN_RB = 128
ATT_UNROLL = 4
ATT_GROUP = 4


def _cparams(sem, vmem=VMEM_MID):
    return pltpu.CompilerParams(dimension_semantics=sem, vmem_limit_bytes=vmem)


def _dot(a, b):
    return jnp.dot(a, b, preferred_element_type=F32)


def _dot_nt(a, b):
    return lax.dot_general(a, b, (((1,), (1,)), ((), ())), preferred_element_type=F32)


def _mod_spec(layer, which, ctx, width=D, col=None):
    def imap(*ids):
        b = ids[0]
        row = CTX_ROW if ctx else b
        c = 0 if col is None else ids[col]
        return (layer, which, row, 0, c)
    return pl.BlockSpec((None, None, None, 1, width), imap)


def _per_segment(i, fn, latent_arg, ctx_arg):
    @pl.when(i == 0)
    def _():
        fn(0, CTX, ctx_arg)
        fn(CTX, TM, latent_arg)

    @pl.when(i != 0)
    def _():
        fn(0, TM, latent_arg)


def _norm_rows(x_ref, dst_ref, dst_off, lo, hi, gain, shift):
    def body(j, _):
        r = pl.multiple_of(lo + j * NORM_RB, NORM_RB)
        x = x_ref[pl.ds(r, NORM_RB), :]
        inv = lax.rsqrt(jnp.mean(x * x, axis=-1, keepdims=True) + EPS)
        dst_ref[pl.ds(dst_off + r, NORM_RB), :] = ((x * inv) * gain + shift).astype(BF16)
        return 0
    n = (hi - lo) // NORM_RB
    lax.fori_loop(0, n, body, 0, unroll=min(n, NORM_UNROLL))


def _norm_tile(i, x_ref, dst_ref, dst_off, g_ref, shb, scb, shc, scc):
    g = g_ref[...]

    def run(lo, hi, mod):
        _norm_rows(x_ref, dst_ref, dst_off, lo, hi, g * (1.0 + mod[0]), mod[1])
    _per_segment(i, run, (scb[...], shb[...]), (scc[...], shc[...]))


def _adaln_kernel(c_ref, w_ref, b_ref, o_ref):
    c = c_ref[...]
    s = (c * jax.nn.sigmoid(c)).astype(BF16)
    o_ref[...] = _dot(s, w_ref[...].astype(BF16)) + b_ref[...]


def _adaln(cc, w_ada, b_ada):
    tn = 1024
    nj = D // tn
    return pl.pallas_call(
        _adaln_kernel,
        grid=(DEPTH, 6 * nj),
        in_specs=[
            pl.BlockSpec((MOD_ROWS, D), lambda l, j: (0, 0)),
            pl.BlockSpec((None, D, tn), lambda l, j: (l, 0, j)),
            pl.BlockSpec((None, 1, tn), lambda l, j: (l, 0, j)),
        ],
        out_specs=pl.BlockSpec((None, None, MOD_ROWS, tn), lambda l, j: (l, j // nj, 0, j % nj)),
        out_shape=jax.ShapeDtypeStruct((DEPTH, 6, MOD_ROWS, D), F32),
        compiler_params=_cparams(("parallel", "parallel")),
        name="adaln",
    )(cc, w_ada, b_ada.reshape(DEPTH, 1, 6 * D))


def _inproj_kernel(x_ref, g_ref, shb, scb, shc, scc, w_ref, o_ref, hx_s):
    i = pl.program_id(1)

    @pl.when(pl.program_id(2) == 0)
    def _():
        _norm_tile(i, x_ref, hx_s, 0, g_ref, shb, scb, shc, scc)

    o_ref[...] = _dot(hx_s[...], w_ref[...])


def _inproj(xall, mod5, layer, g_all, w_all):
    tn = 1024
    return pl.pallas_call(
        _inproj_kernel,
        grid=(B, NT, PROJ // tn),
        in_specs=[
            pl.BlockSpec((None, TM, D), lambda b, i, j: (b, i, 0)),
            pl.BlockSpec((None, 1, D), lambda b, i, j: (layer, 0, 0)),
            _mod_spec(layer, 0, False), _mod_spec(layer, 1, False),
            _mod_spec(layer, 0, True), _mod_spec(layer, 1, True),
            pl.BlockSpec((None, D, tn), lambda b, i, j: (layer, 0, j)),
        ],
        out_specs=pl.BlockSpec((None, TM, tn), lambda b, i, j: (b, i, j)),
        out_shape=jax.ShapeDtypeStruct((B, T, PROJ), F32),
        scratch_shapes=[pltpu.VMEM((TM, D), BF16)],
        compiler_params=_cparams(("parallel", "parallel", "arbitrary")),
        name="inproj",
    )(xall, g_all, mod5, mod5, mod5, mod5, w_all)


def _tile_scan(a, b, carry, rev):
    rid = lax.broadcasted_iota(jnp.int32, a.shape, 0)
    for d in (1, 2, 4):
        sh = SUBLANES - d if rev else d
        keep = (rid < SUBLANES - d) if rev else (rid >= d)
        a_s = jnp.where(keep, pltpu.roll(a, sh, 0), 1.0)
        b_s = jnp.where(keep, pltpu.roll(b, sh, 0), 0.0)
        b = a * b_s + b
        a = a * a_s
    h = a * carry + b
    return h, (h[0:1] if rev else h[SUBLANES - 1:SUBLANES])


def _lru_kernel(ax_ref, ay_ref, cw_ref, cb_ref, gw_ref, gb_ref, lam_ref, o_ref, af, bf, ab, bb):
    W = LRU_HALF
    cw = cw_ref[...]
    cb = cb_ref[...]
    gb = gb_ref[...]
    nl = -lam_ref[...]
    softplus = jnp.maximum(nl, 0.0) + jnp.log(1.0 + jnp.exp(-jnp.abs(nl)))
    coef = -LRU_C * softplus
    zeros8 = jnp.zeros((SUBLANES, W), F32)
    n_ext = LRU_RB + 2 * SUBLANES

    for blk in range(T // LRU_RB):
        r0 = blk * LRU_RB
        r1 = r0 + LRU_RB
        prev = zeros8 if r0 in (0, CTX) else ax_ref[r0 - SUBLANES:r0, :]
        nxt = zeros8 if r1 in (CTX, T) else ax_ref[r1:r1 + SUBLANES, :]
        ext = jnp.concatenate([prev, ax_ref[r0:r1, :], nxt], axis=0)
        lo, hi = SUBLANES, SUBLANES + LRU_RB
        xc = cb + pltpu.roll(ext, 2, 0)[lo:hi] * cw[0:1]
        xc = xc + pltpu.roll(ext, 1, 0)[lo:hi] * cw[1:2]
        xc = xc + ext[lo:hi] * cw[2:3]
        xc = xc + pltpu.roll(ext, n_ext - 1, 0)[lo:hi] * cw[3:4]
        xb = xc.astype(BF16)
        for d, (a_s, b_s) in enumerate(((af, bf), (ab, bb))):
            r = jax.nn.sigmoid(_dot(xb, gw_ref[2 * d]) + gb[2 * d:2 * d + 1])
            gi = jax.nn.sigmoid(_dot(xb, gw_ref[2 * d + 1]) + gb[2 * d + 1:2 * d + 2])
            log_a = coef[d:d + 1] * r
            a_s[r0:r1, :] = jnp.exp(log_a)
            th = jnp.tanh(log_a)
            b_s[r0:r1, :] = jnp.sqrt(-2.0 * th / (1.0 - th)) * (gi * xc)

    def make_body(f_base, b_top):
        def body(k, carry):
            cf, cr = carry
            rf = pl.multiple_of((f_base + k) * SUBLANES, SUBLANES)
            hf, cf = _tile_scan(af[pl.ds(rf, SUBLANES), :], bf[pl.ds(rf, SUBLANES), :], cf, False)
            bf[pl.ds(rf, SUBLANES), :] = hf
            rb = pl.multiple_of((b_top - k) * SUBLANES, SUBLANES)
            hb, cr = _tile_scan(ab[pl.ds(rb, SUBLANES), :], bb[pl.ds(rb, SUBLANES), :], cr, True)
            bb[pl.ds(rb, SUBLANES), :] = hb
            return cf, cr
        return body

    z1 = jnp.zeros((1, W), F32)
    n_ctx = CTX // SUBLANES
    n_all = T // SUBLANES
    carry = lax.fori_loop(0, n_ctx, make_body(0, n_ctx - 1), (z1, z1))
    lax.fori_loop(0, n_all - n_ctx, make_body(n_ctx, n_all - 1), carry)

    for blk in range(T // LRU_RB):
        r0 = blk * LRU_RB
        r1 = r0 + LRU_RB
        h = bf[r0:r1, :] + bb[r0:r1, :]
        o_ref[r0:r1, :] = (h * jax.nn.gelu(ay_ref[r0:r1, :])).astype(BF16)


def _lru(z, layer, cw, cb, gw_bd, gb, lam):
    W = LRU_HALF
    nh = LRU_W // W
    return pl.pallas_call(
        _lru_kernel,
        grid=(B, nh),
        in_specs=[
            pl.BlockSpec((None, T, W), lambda b, c: (b, 0, c)),
            pl.BlockSpec((None, T, W), lambda b, c: (b, 0, nh + c)),
            pl.BlockSpec((None, LRU_TAPS, W), lambda b, c: (layer, 0, c)),
            pl.BlockSpec((None, 1, W), lambda b, c: (layer, 0, c)),
            pl.BlockSpec((None, None, 4, W, W), lambda b, c: (layer, c, 0, 0, 0)),
            pl.BlockSpec((None, 4, W), lambda b, c: (layer, 0, c)),
            pl.BlockSpec((None, 2, W), lambda b, c: (layer, 0, c)),
        ],
        out_specs=pl.BlockSpec((None, T, W), lambda b, c: (b, 0, c)),
        out_shape=jax.ShapeDtypeStruct((B, T, LRU_W), BF16),
        scratch_shapes=[pltpu.VMEM((T, W), F32)] * 4,
        compiler_params=_cparams(("parallel", "parallel")),
        name="lru",
    )(z, z, cw, cb, gw_bd, gb, lam)


def _attn_kernel(q_ref, k_ref, v_ref, cos_ref, sin_ref, bias_ref, o_ref,
                 qraw_s, qrot_s, qctx_s, krot_s, kc_s, v_s, sc_s, pc_s, ow_s):
    lane = lax.broadcasted_iota(jnp.int32, (1, LANES), 1)
    head0 = lane < HEAD
    second16 = ((lane // ROPE_F) % 2) == 1
    scale = HEAD ** -0.5 * LOG2E
    W2 = 2 * GRID_W

    def rope(x, c, s):
        partner = jnp.where(second16, pltpu.roll(x, ROPE_F, 1), pltpu.roll(x, LANES - ROPE_F, 1))
        return x * c + partner * s

    def stack_heads(x):
        return jnp.concatenate([jnp.where(head0, x, 0.0), jnp.where(head0, 0.0, x)], axis=0).astype(BF16)

    def unstack_heads(y):
        n = y.shape[0] // 2
        return jnp.where(head0, y[:n], y[n:])

    qctx_s[...] = stack_heads(q_ref[0:CTX, :] * scale)
    kc_s[...] = k_ref[0:CTX, :].astype(BF16)
    v_s[0:CTX, :] = v_ref[0:CTX, :].astype(BF16)
    rows_per_blk = ATT_RB // GRID_W
    for blk in range(SEQ // ATT_RB):
        r0 = blk * ATT_RB
        r1 = r0 + ATT_RB
        c = cos_ref[r0:r1, :]
        s = sin_ref[r0:r1, :]
        q = q_ref[CTX + r0:CTX + r1, :]
        qs = q * scale
        qr = rope(q, c, s) * scale
        for j in range(rows_per_blk):
            qraw_s[blk * rows_per_blk + j] = stack_heads(qs[j * GRID_W:(j + 1) * GRID_W])
            qrot_s[blk * rows_per_blk + j] = stack_heads(qr[j * GRID_W:(j + 1) * GRID_W])
        krot_s[r0:r1, :] = rope(k_ref[CTX + r0:CTX + r1, :], c, s).astype(BF16)
        v_s[CTX + r0:CTX + r1, :] = v_ref[CTX + r0:CTX + r1, :].astype(BF16)

    s = _dot_nt(qctx_s[...], kc_s[...])
    e = jnp.exp2(s - jnp.max(s, axis=-1, keepdims=True))
    p = (e / jnp.sum(e, axis=-1, keepdims=True)).astype(BF16)
    o_ref[0:CTX, :] = unstack_heads(_dot(p, v_s[0:CTX, :])).astype(BF16)

    for g in range(GRID_H // ATT_GROUP):
        g0, g1 = g * ATT_GROUP, (g + 1) * ATT_GROUP
        sc = _dot_nt(qraw_s[g0:g1].reshape(ATT_GROUP * W2, LANES), kc_s[...])
        sc_s[g0:g1] = sc.reshape(ATT_GROUP, W2, CTX)

    win = NA_ROWS * GRID_W

    def lane_fold(op, *xs):
        cols = [x[:, c:c + LANES] for x in xs for c in range(0, x.shape[1], LANES)]
        acc = cols[0]
        for col in cols[1:]:
            acc = op(acc, col)
        return acc

    def body(t, _):
        rows = [t * ATT_UNROLL + j for j in range(ATT_UNROLL)]
        starts = [jnp.clip(r - NA_ROWS // 2, 0, GRID_H - NA_ROWS) for r in rows]
        sw = [_dot_nt(qrot_s[r], krot_s[pl.ds(pl.multiple_of(rs * GRID_W, GRID_W), win), :]) + bias_ref[r - rs]
              for r, rs in zip(rows, starts)]
        sc = [sc_s[r] for r in rows]
        m = [jnp.max(lane_fold(jnp.maximum, a, c), axis=-1, keepdims=True) for a, c in zip(sw, sc)]
        ew = [jnp.exp2(a - mm) for a, mm in zip(sw, m)]
        ec = [jnp.exp2(c - mm) for c, mm in zip(sc, m)]
        inv = [1.0 / jnp.sum(lane_fold(jnp.add, a, c), axis=-1, keepdims=True) for a, c in zip(ew, ec)]
        for r, rs, a, c, iv in zip(rows, starts, ew, ec, inv):
            kx = pl.multiple_of(CTX + rs * GRID_W, GRID_W)
            ow_s[r] = _dot(a.astype(BF16), v_s[pl.ds(kx, win), :]) * iv
            pc_s[r] = (c * iv).astype(BF16)
        return 0

    lax.fori_loop(0, GRID_H // ATT_UNROLL, body, 0)

    for g in range(GRID_H // ATT_GROUP):
        g0, g1 = g * ATT_GROUP, (g + 1) * ATT_GROUP
        oc = _dot(pc_s[g0:g1].reshape(ATT_GROUP * W2, CTX), v_s[0:CTX, :]).reshape(ATT_GROUP, W2, LANES)
        tot = oc + ow_s[g0:g1]
        for j in range(ATT_GROUP):
            row = CTX + (g0 + j) * GRID_W
            o_ref[row:row + GRID_W, :] = unstack_heads(tot[j]).astype(BF16)


def _attn(z, layer, cos_t, sin_t, biasmask):
    nhp = NA_W // LANES
    qb = 2 * LRU_W // LANES
    return pl.pallas_call(
        _attn_kernel,
        grid=(nhp, B),
        in_specs=[
            pl.BlockSpec((None, T, LANES), lambda p, b: (b, 0, qb + p)),
            pl.BlockSpec((None, T, LANES), lambda p, b: (b, 0, qb + nhp + p)),
            pl.BlockSpec((None, T, LANES), lambda p, b: (b, 0, qb + 2 * nhp + p)),
            pl.BlockSpec((SEQ, LANES), lambda p, b: (0, 0)),
            pl.BlockSpec((SEQ, LANES), lambda p, b: (0, 0)),
            pl.BlockSpec((None, None, NA_ROWS, 2 * GRID_W, NA_ROWS * GRID_W), lambda p, b: (layer, p, 0, 0, 0)),
        ],
        out_specs=pl.BlockSpec((None, T, LANES), lambda p, b: (b, 0, p)),
        out_shape=jax.ShapeDtypeStruct((B, T, NA_W), BF16),
        scratch_shapes=[
            pltpu.VMEM((GRID_H, 2 * GRID_W, LANES), BF16),
            pltpu.VMEM((GRID_H, 2 * GRID_W, LANES), BF16),
            pltpu.VMEM((2 * CTX, LANES), BF16),
            pltpu.VMEM((SEQ, LANES), BF16),
            pltpu.VMEM((CTX, LANES), BF16),
            pltpu.VMEM((T, LANES), BF16),
            pltpu.VMEM((GRID_H, 2 * GRID_W, CTX), F32),
            pltpu.VMEM((GRID_H, 2 * GRID_W, CTX), BF16),
            pltpu.VMEM((GRID_H, 2 * GRID_W, LANES), F32),
        ],
        compiler_params=_cparams(("parallel", "parallel")),
        name="attn",
    )(z, z, z, cos_t, sin_t, biasmask)


def _sgu_kernel(u_ref, v_ref, g_ref, b_ref, ws_ref, bs_ref, o_ref):
    lane = lax.broadcasted_iota(jnp.int32, (1, LANES), 1)
    first = lane < HEAD
    for n in range(TM // SGU_CHUNK):
        r0 = n * SGU_CHUNK
        r1 = r0 + SGU_CHUNK
        v = jax.nn.gelu(v_ref[r0:r1, :])
        mu = jnp.mean(v, axis=-1, keepdims=True)
        var = jnp.mean(jnp.square(v - mu), axis=-1, keepdims=True)
        vn = ((v - mu) * lax.rsqrt(var + EPS) * g_ref[...] + b_ref[...]).astype(BF16)
        for j in range(SGU_W // LANES):
            c0 = j * LANES
            c1 = c0 + LANES
            vp = vn[:, c0:c1]
            mixed = jnp.where(first, _dot(ws_ref[2 * j], vp), _dot(ws_ref[2 * j + 1], vp)) + bs_ref[:, c0:c1]
            o_ref[r0:r1, c0:c1] = (jax.nn.gelu(u_ref[r0:r1, c0:c1]) * mixed).astype(BF16)


def _sgu(z, layer, ln_g, ln_b, ws_bf, bs_full):
    ub = (2 * LRU_W + 3 * NA_W) // SGU_W
    return pl.pallas_call(
        _sgu_kernel,
        grid=(B, NT),
        in_specs=[
            pl.BlockSpec((None, TM, SGU_W), lambda b, i: (b, i, ub)),
            pl.BlockSpec((None, TM, SGU_W), lambda b, i: (b, i, ub + 1)),
            pl.BlockSpec((None, 1, SGU_W), lambda b, i: (layer, 0, 0)),
            pl.BlockSpec((None, 1, SGU_W), lambda b, i: (layer, 0, 0)),
            pl.BlockSpec((None, SGU_G, SGU_CHUNK, SGU_CHUNK), lambda b, i: (layer, 0, 0, 0)),
            pl.BlockSpec((None, SGU_CHUNK, SGU_W), lambda b, i: (layer, 0, 0)),
        ],
        out_specs=pl.BlockSpec((None, TM, SGU_W), lambda b, i: (b, i, 0)),
        out_shape=jax.ShapeDtypeStruct((B, T, SGU_W), BF16),
        compiler_params=_cparams(("parallel", "parallel")),
        name="sgu",
    )(z, z, ln_g, ln_b, ws_bf, bs_full)


def _outproj_kernel(x_ref, a_ref, b_ref, c_ref, gtb, gtc, w_ref, o_ref):
    i = pl.program_id(1)
    acc = _dot(a_ref[...], w_ref[0:LRU_W, :])
    acc = acc + _dot(b_ref[...], w_ref[LRU_W:LRU_W + NA_W, :])
    acc = acc + _dot(c_ref[...], w_ref[LRU_W + NA_W:D, :])
    is_ctx = (lax.broadcasted_iota(jnp.int32, (TM, 1), 0) + i * TM) < CTX
    o_ref[...] = x_ref[...] + jnp.where(is_ctx, gtc[...], gtb[...]) * acc


def _outproj(xall, oa, ob, oc, mod5, layer, w_all):
    tn = 1024
    return pl.pallas_call(
        _outproj_kernel,
        grid=(B, NT, D // tn),
        in_specs=[
            pl.BlockSpec((None, TM, tn), lambda b, i, j: (b, i, j)),
            pl.BlockSpec((None, TM, LRU_W), lambda b, i, j: (b, i, 0)),
            pl.BlockSpec((None, TM, NA_W), lambda b, i, j: (b, i, 0)),
            pl.BlockSpec((None, TM, SGU_W), lambda b, i, j: (b, i, 0)),
            _mod_spec(layer, 2, False, tn, 2), _mod_spec(layer, 2, True, tn, 2),
            pl.BlockSpec((None, D, tn), lambda b, i, j: (layer, 0, j)),
        ],
        out_specs=pl.BlockSpec((None, TM, tn), lambda b, i, j: (b, i, j)),
        out_shape=jax.ShapeDtypeStruct((B, T, D), F32),
        compiler_params=_cparams(("parallel", "parallel", "parallel")),
        name="outproj",
    )(xall, oa, ob, oc, mod5, mod5, w_all)


def _ffn_kernel(x_ref, xp_ref, xn_ref, g_ref, shb, scb, gtb, shc, scc, gtc, wa_ref, wg_ref, cw_ref, cb_ref, wd_ref,
                o_ref, hx_s, u_s, act_s):
    i = pl.program_id(1)
    k = pl.program_id(2)
    rb = FFN_RB
    left_blocks = (0, CTX // rb)
    right_blocks = (CTX // rb - 1, TM // rb - 1)

    @pl.when(k == 0)
    def _():
        gain_b = g_ref[...] * (1.0 + scb[...])
        _norm_rows(xp_ref, hx_s, 0, 0, HALO, gain_b, shb[...])
        _norm_rows(xn_ref, hx_s, HALO + TM, 0, HALO, gain_b, shb[...])
        _norm_tile(i, x_ref, hx_s, HALO, g_ref, shb, scb, shc, scc)
        o_ref[...] = jnp.zeros_like(o_ref)

    u_s[:, :TH] = _dot(hx_s[...], wa_ref[...])
    u_s[:, TH:] = _dot(hx_s[...], wg_ref[...])
    cw = cw_ref[...]
    cb = cb_ref[...]
    for blk in range(TM // rb):
        r0 = HALO + blk * rb
        rows = lax.broadcasted_iota(jnp.int32, (rb, 1), 0) + (i * TM + blk * rb)
        ul = u_s[r0 - 1:r0 - 1 + rb, :]
        ur = u_s[r0 + 1:r0 + 1 + rb, :]
        if blk in left_blocks:
            ul = jnp.where(jnp.logical_and(rows != 0, rows != CTX), ul, 0.0)
        if blk in right_blocks:
            ur = jnp.where(jnp.logical_and(rows != CTX - 1, rows != T - 1), ur, 0.0)
        y = cb + ul * cw[0:1]
        y = y + u_s[r0:r0 + rb, :] * cw[1:2]
        y = y + ur * cw[2:3]
        a = y[:, :TH]
        gg = y[:, TH:]
        act_s[blk * rb:(blk + 1) * rb, :] = (gg * jax.nn.sigmoid(gg) * a).astype(BF16)
    o_ref[...] += _dot(act_s[...], wd_ref[...])

    @pl.when(k == NK - 1)
    def _():
        def finish(lo, hi, gate):
            o_ref[lo:hi, :] = x_ref[lo:hi, :] + gate * o_ref[lo:hi, :]
        _per_segment(i, finish, gtb[...], gtc[...])


def _ffn(xall, mod5, layer, g, wa, wg, cw_r, cb_r, wd_p):
    hb = TM // HALO
    last = T // HALO - 1
    col_map = lambda b, i, k: (layer, 0, k)
    return pl.pallas_call(
        _ffn_kernel,
        grid=(B, NT, NK),
        in_specs=[
            pl.BlockSpec((None, TM, D), lambda b, i, k: (b, i, 0)),
            pl.BlockSpec((None, HALO, D), lambda b, i, k: (b, jnp.maximum(i * hb - 1, 0), 0)),
            pl.BlockSpec((None, HALO, D), lambda b, i, k: (b, jnp.minimum((i + 1) * hb, last), 0)),
            pl.BlockSpec((None, 1, D), lambda b, i, k: (layer, 0, 0)),
            _mod_spec(layer, 3, False), _mod_spec(layer, 4, False), _mod_spec(layer, 5, False),
            _mod_spec(layer, 3, True), _mod_spec(layer, 4, True), _mod_spec(layer, 5, True),
            pl.BlockSpec((None, D, TH), col_map),
            pl.BlockSpec((None, D, TH), col_map),
            pl.BlockSpec((None, 3, 2 * TH), col_map),
            pl.BlockSpec((None, 1, 2 * TH), col_map),
            pl.BlockSpec((None, TH, D), lambda b, i, k: (layer, k, 0)),
        ],
        out_specs=pl.BlockSpec((None, TM, D), lambda b, i, k: (b, i, 0)),
        out_shape=jax.ShapeDtypeStruct((B, T, D), F32),
        scratch_shapes=[
            pltpu.VMEM((TM + 2 * HALO, D), BF16),
            pltpu.VMEM((TM + 2 * HALO, 2 * TH), F32),
            pltpu.VMEM((TM, TH), BF16),
        ],
        compiler_params=_cparams(("parallel", "parallel", "arbitrary"), VMEM_BIG),
        name="ffn",
    )(xall, xall, xall, g, mod5, mod5, mod5, mod5, mod5, mod5, wa, wg, cw_r, cb_r, wd_p)


def _final_kernel(x_ref, g_ref, o_ref):
    x = x_ref[...]
    o_ref[...] = x * lax.rsqrt(jnp.mean(x * x, axis=-1, keepdims=True) + EPS) * g_ref[...]


def _final_norm(xall, g):
    tm = CTX
    return pl.pallas_call(
        _final_kernel,
        grid=(B, SEQ // tm),
        in_specs=[
            pl.BlockSpec((None, tm, D), lambda b, i: (b, i + 1, 0)),
            pl.BlockSpec((1, D), lambda b, i: (0, 0)),
        ],
        out_specs=pl.BlockSpec((None, tm, D), lambda b, i: (b, i, 0)),
        out_shape=jax.ShapeDtypeStruct((B, SEQ, D), F32),
        compiler_params=_cparams(("parallel", "parallel")),
        name="final_norm",
    )(xall, g.reshape(1, D))


def _rope_tables():
    t = jnp.arange(SEQ)
    pos = jnp.stack([t // GRID_W, t % GRID_W], axis=-1).astype(F32)
    inv = ROPE_BASE ** (-jnp.arange(ROPE_F, dtype=F32) / ROPE_F)
    ang = pos[:, :, None] * inv
    cos, sin = jnp.cos(ang), jnp.sin(ang)
    cos_h = jnp.concatenate([cos[:, 0], cos[:, 0], cos[:, 1], cos[:, 1]], axis=-1)
    sin_h = jnp.concatenate([-sin[:, 0], sin[:, 0], -sin[:, 1], sin[:, 1]], axis=-1)
    reps = LANES // HEAD
    return jnp.tile(cos_h, (1, reps)), jnp.tile(sin_h, (1, reps))


def _bias_tables(rpb):
    cls = jnp.arange(NA_ROWS)[:, None]
    i = jnp.arange(NA_ROWS)[None, :]
    c = jnp.arange(GRID_W)[:, None]
    kc = jnp.arange(GRID_W)[None, :]
    col_start = jnp.clip(c - NA_KC // 2, 0, GRID_W - NA_KC)
    in_win = (kc >= col_start) & (kc < col_start + NA_KC)
    sel_r = ((i - cls + (NA_ROWS - 1))[..., None] == jnp.arange(2 * NA_ROWS - 1)).astype(F32)
    sel_c = (((kc - c + (NA_KC - 1))[..., None] == jnp.arange(2 * NA_KC - 1)) & in_win[..., None]).astype(F32)
    nl = rpb.shape[0]
    pairs = rpb.reshape(nl, NA_HEADS // 2, 2, 2 * NA_ROWS - 1, 2 * NA_KC - 1) * LOG2E
    rows = jnp.einsum('lpamn,xim->lpxain', pairs, sel_r, precision=lax.Precision.HIGHEST)
    bias = jnp.einsum('lpxain,ckn->lpxacik', rows, sel_c, precision=lax.Precision.HIGHEST)
    bias = jnp.where(in_win[None, None, None, None, :, None, :], bias, NEG)
    return bias.reshape(nl, NA_HEADS // 2, NA_ROWS, 2 * GRID_W, NA_ROWS * GRID_W)


def _gate_blockdiag(gate_w):
    nl = gate_w.shape[0]
    nh = LRU_W // LRU_HALF
    gper = LRU_HALF // HEAD
    w = gate_w.reshape(nl, 4, nh, gper, HEAD, HEAD)
    eye = jnp.eye(gper, dtype=gate_w.dtype)
    bd = jnp.einsum('lkhgio,gj->lkhgijo', w, eye).reshape(nl, 4, nh, LRU_HALF, LRU_HALF)
    return jnp.transpose(bd, (0, 2, 1, 3, 4)).astype(BF16)


def _ffn_layout(w_up, conv_w, conv_b, w_down):
    pad = FFN_HP - FFN_H
    nl = w_up.shape[0]

    def inter(m):
        r = m.shape[1]
        a = jnp.pad(m[..., :FFN_H], ((0, 0), (0, 0), (0, pad))).reshape(nl, r, NK, 1, TH)
        g = jnp.pad(m[..., FFN_H:], ((0, 0), (0, 0), (0, pad))).reshape(nl, r, NK, 1, TH)
        return jnp.concatenate([a, g], axis=3).reshape(nl, r, NK * 2 * TH)

    wa = jnp.pad(w_up[..., :FFN_H], ((0, 0), (0, 0), (0, pad))).astype(BF16)
    wg = jnp.pad(w_up[..., FFN_H:], ((0, 0), (0, 0), (0, pad))).astype(BF16)
    wd = jnp.pad(w_down, ((0, 0), (0, pad), (0, 0))).astype(BF16)
    return wa, wg, inter(conv_w), inter(conv_b[:, None, :]), wd


def kernel(x, c, ctx, c_ctx, w_ada, b_ada, norm_mix_g, norm_ffn_g, w_in, lru_conv_w, lru_conv_b, lru_gate_w, lru_gate_b, lru_lambda, na_rpb, sgu_ln_g, sgu_ln_b, sgu_w, sgu_b, w_out, ffn_up, ffn_conv_w, ffn_conv_b, ffn_down, final_norm_g):
    xall = jnp.concatenate([ctx, x], axis=1)
    cc = jnp.concatenate([c, c_ctx[None], jnp.zeros((MOD_ROWS - B - 1, D), F32)], axis=0)
    mod5 = _adaln(cc, w_ada, b_ada).reshape(DEPTH, 6, MOD_ROWS, 1, D)
    cos_t, sin_t = _rope_tables()

    g_mix = norm_mix_g.reshape(DEPTH, 1, D)
    g_ffn = norm_ffn_g.reshape(DEPTH, 1, D)
    w_in_bf = w_in.astype(BF16)
    w_out_bf = w_out.astype(BF16)
    lru_cb = lru_conv_b.reshape(DEPTH, 1, LRU_W)
    lru_gw = _gate_blockdiag(lru_gate_w)
    lru_gb = lru_gate_b.reshape(DEPTH, 4, LRU_W)
    bias = _bias_tables(na_rpb)
    sgu_g = sgu_ln_g.reshape(DEPTH, 1, SGU_W)
    sgu_bb = sgu_ln_b.reshape(DEPTH, 1, SGU_W)
    sgu_w_bf = sgu_w.astype(BF16)
    sgu_bias = jnp.repeat(jnp.swapaxes(sgu_b, 1, 2), HEAD, axis=2)
    ffn_params = _ffn_layout(ffn_up, ffn_conv_w, ffn_conv_b, ffn_down)

    for l in range(DEPTH):
        z = _inproj(xall, mod5, l, g_mix, w_in_bf)
        oa = _lru(z, l, lru_conv_w, lru_cb, lru_gw, lru_gb, lru_lambda)
        ob = _attn(z, l, cos_t, sin_t, bias)
        oc = _sgu(z, l, sgu_g, sgu_bb, sgu_w_bf, sgu_bias)
        xall = _outproj(xall, oa, ob, oc, mod5, l, w_out_bf)
        xall = _ffn(xall, mod5, l, g_ffn, *ffn_params)
    return _final_norm(xall, final_norm_g)
```

```python
import functools

import jax
import jax.numpy as jnp
from jax import lax
from jax.experimental import pallas as pl
from jax.experimental.pallas import tpu as pltpu

F32 = jnp.float32
BF16 = jnp.bfloat16

D = 2048
B = 8
SEQ = 2048
DEPTH = 4
GRID_W = 64
GRID_H = SEQ // GRID_W
CTX = 256
T = CTX + SEQ
HEAD = 64
LRU_W = 512
LRU_TAPS = 4
LRU_C = 8.0
NA_W = 1024
NA_HEADS = NA_W // HEAD
NA_ROWS = 8
NA_KC = 16
SGU_W = 512
SGU_G = SGU_W // HEAD
SGU_CHUNK = 128
PROJ = 2 * LRU_W + 3 * NA_W + 2 * SGU_W
FFN_H = 5504
EPS = 1e-6
NEG = -1e30
LOG2E = 1.4426950408889634
ROPE_BASE = 10000.0
ROPE_F = HEAD // 4
MOD_ROWS = 16
CTX_ROW = B

LANES = 128
SUBLANES = 8
VMEM_BIG = 56 * 1024 * 1024
VMEM_MID = 40 * 1024 * 1024

TM = 768
NT = T // TM
HALO = 16
TH = 512
FFN_HP = 5632
NK = FFN_HP // TH
LRU_HALF = 256
LRU_RB = 128
ATT_RB = 256
NORM_RB = 16
NORM_UNROLL = 4
FFN_RB = 128
ATT_UNROLL = 4
ATT_GROUP = 4


def _cparams(sem, vmem=VMEM_MID):
    return pltpu.CompilerParams(dimension_semantics=sem, vmem_limit_bytes=vmem)


def _dot(a, b):
    return jnp.dot(a, b, preferred_element_type=F32)


def _dot_nt(a, b):
    return lax.dot_general(a, b, (((1,), (1,)), ((), ())), preferred_element_type=F32)


def _mod_spec(layer, which, ctx, width=D, col=None):
    def imap(*ids):
        b = ids[0]
        row = CTX_ROW if ctx else b
        c = 0 if col is None else ids[col]
        return (layer, which, row, 0, c)
    return pl.BlockSpec((None, None, None, 1, width), imap)


def _per_segment(i, fn, latent_arg, ctx_arg):
    @pl.when(i == 0)
    def _():
        fn(0, CTX, ctx_arg)
        fn(CTX, TM, latent_arg)

    @pl.when(i != 0)
    def _():
        fn(0, TM, latent_arg)


def _norm_rows(x_ref, dst_ref, dst_off, lo, hi, gain, shift):
    def body(j, _):
        r = pl.multiple_of(lo + j * NORM_RB, NORM_RB)
        x = x_ref[pl.ds(r, NORM_RB), :]
        inv = lax.rsqrt(jnp.mean(x * x, axis=-1, keepdims=True) + EPS)
        dst_ref[pl.ds(dst_off + r, NORM_RB), :] = ((x * inv) * gain + shift).astype(BF16)
        return 0
    n = (hi - lo) // NORM_RB
    lax.fori_loop(0, n, body, 0, unroll=min(n, NORM_UNROLL))


def _norm_tile(i, x_ref, dst_ref, dst_off, g_ref, shb, scb, shc, scc):
    g = g_ref[...]

    def run(lo, hi, mod):
        _norm_rows(x_ref, dst_ref, dst_off, lo, hi, g * (1.0 + mod[0]), mod[1])
    _per_segment(i, run, (scb[...], shb[...]), (scc[...], shc[...]))


def _adaln_kernel(c_ref, w_ref, b_ref, o_ref):
    c = c_ref[...]
    s = (c * jax.nn.sigmoid(c)).astype(BF16)
    o_ref[...] = _dot(s, w_ref[...].astype(BF16)) + b_ref[...]


def _adaln(cc, w_ada, b_ada):
    tn = 1024
    nj = D // tn
    return pl.pallas_call(
        _adaln_kernel,
        grid=(DEPTH, 6 * nj),
        in_specs=[
            pl.BlockSpec((MOD_ROWS, D), lambda l, j: (0, 0)),
            pl.BlockSpec((None, D, tn), lambda l, j: (l, 0, j)),
            pl.BlockSpec((None, 1, tn), lambda l, j: (l, 0, j)),
        ],
        out_specs=pl.BlockSpec((None, None, MOD_ROWS, tn), lambda l, j: (l, j // nj, 0, j % nj)),
        out_shape=jax.ShapeDtypeStruct((DEPTH, 6, MOD_ROWS, D), F32),
        compiler_params=_cparams(("parallel", "parallel")),
        name="adaln",
    )(cc, w_ada, b_ada.reshape(DEPTH, 1, 6 * D))


def _inproj_kernel(x_ref, g_ref, shb, scb, shc, scc, w_ref, o_ref, hx_s):
    i = pl.program_id(1)

    @pl.when(pl.program_id(2) == 0)
    def _():
        _norm_tile(i, x_ref, hx_s, 0, g_ref, shb, scb, shc, scc)

    o_ref[...] = _dot(hx_s[...], w_ref[...])


def _inproj(xall, mod5, layer, g_all, w_all):
    tn = 1024
    return pl.pallas_call(
        _inproj_kernel,
        grid=(B, NT, PROJ // tn),
        in_specs=[
            pl.BlockSpec((None, TM, D), lambda b, i, j: (b, i, 0)),
            pl.BlockSpec((None, 1, D), lambda b, i, j: (layer, 0, 0)),
            _mod_spec(layer, 0, False), _mod_spec(layer, 1, False),
            _mod_spec(layer, 0, True), _mod_spec(layer, 1, True),
            pl.BlockSpec((None, D, tn), lambda b, i, j: (layer, 0, j)),
        ],
        out_specs=pl.BlockSpec((None, TM, tn), lambda b, i, j: (b, i, j)),
        out_shape=jax.ShapeDtypeStruct((B, T, PROJ), F32),
        scratch_shapes=[pltpu.VMEM((TM, D), BF16)],
        compiler_params=_cparams(("parallel", "parallel", "arbitrary")),
        name="inproj",
    )(xall, g_all, mod5, mod5, mod5, mod5, w_all)


def _tile_scan(a, b, carry, rev):
    rid = lax.broadcasted_iota(jnp.int32, a.shape, 0)
    for d in (1, 2, 4):
        sh = SUBLANES - d if rev else d
        keep = (rid < SUBLANES - d) if rev else (rid >= d)
        a_s = jnp.where(keep, pltpu.roll(a, sh, 0), 1.0)
        b_s = jnp.where(keep, pltpu.roll(b, sh, 0), 0.0)
        b = a * b_s + b
        a = a * a_s
    h = a * carry + b
    return h, (h[0:1] if rev else h[SUBLANES - 1:SUBLANES])


def _lru_kernel(ax_ref, ay_ref, cw_ref, cb_ref, gw_ref, gb_ref, lam_ref, o_ref, af, bf, ab, bb):
    W = LRU_HALF
    cw = cw_ref[...]
    cb = cb_ref[...]
    gb = gb_ref[...]
    nl = -lam_ref[...]
    softplus = jnp.maximum(nl, 0.0) + jnp.log(1.0 + jnp.exp(-jnp.abs(nl)))
    coef = -LRU_C * softplus
    zeros8 = jnp.zeros((SUBLANES, W), F32)
    n_ext = LRU_RB + 2 * SUBLANES

    for blk in range(T // LRU_RB):
        r0 = blk * LRU_RB
        r1 = r0 + LRU_RB
        prev = zeros8 if r0 in (0, CTX) else ax_ref[r0 - SUBLANES:r0, :]
        nxt = zeros8 if r1 in (CTX, T) else ax_ref[r1:r1 + SUBLANES, :]
        ext = jnp.concatenate([prev, ax_ref[r0:r1, :], nxt], axis=0)
        lo, hi = SUBLANES, SUBLANES + LRU_RB
        xc = cb + pltpu.roll(ext, 2, 0)[lo:hi] * cw[0:1]
        xc = xc + pltpu.roll(ext, 1, 0)[lo:hi] * cw[1:2]
        xc = xc + ext[lo:hi] * cw[2:3]
        xc = xc + pltpu.roll(ext, n_ext - 1, 0)[lo:hi] * cw[3:4]
        xb = xc.astype(BF16)
        for d, (a_s, b_s) in enumerate(((af, bf), (ab, bb))):
            r = jax.nn.sigmoid(_dot(xb, gw_ref[2 * d]) + gb[2 * d:2 * d + 1])
            gi = jax.nn.sigmoid(_dot(xb, gw_ref[2 * d + 1]) + gb[2 * d + 1:2 * d + 2])
            log_a = coef[d:d + 1] * r
            a_s[r0:r1, :] = jnp.exp(log_a)
            th = jnp.tanh(log_a)
            b_s[r0:r1, :] = jnp.sqrt(-2.0 * th / (1.0 - th)) * (gi * xc)

    def make_body(f_base, b_top):
        def body(k, carry):
            cf, cr = carry
            rf = pl.multiple_of((f_base + k) * SUBLANES, SUBLANES)
            hf, cf = _tile_scan(af[pl.ds(rf, SUBLANES), :], bf[pl.ds(rf, SUBLANES), :], cf, False)
            bf[pl.ds(rf, SUBLANES), :] = hf
            rb = pl.multiple_of((b_top - k) * SUBLANES, SUBLANES)
            hb, cr = _tile_scan(ab[pl.ds(rb, SUBLANES), :], bb[pl.ds(rb, SUBLANES), :], cr, True)
            bb[pl.ds(rb, SUBLANES), :] = hb
            return cf, cr
        return body

    z1 = jnp.zeros((1, W), F32)
    n_ctx = CTX // SUBLANES
    n_all = T // SUBLANES
    carry = lax.fori_loop(0, n_ctx, make_body(0, n_ctx - 1), (z1, z1))
    lax.fori_loop(0, n_all - n_ctx, make_body(n_ctx, n_all - 1), carry)

    for blk in range(T // LRU_RB):
        r0 = blk * LRU_RB
        r1 = r0 + LRU_RB
        h = bf[r0:r1, :] + bb[r0:r1, :]
        o_ref[r0:r1, :] = (h * jax.nn.gelu(ay_ref[r0:r1, :])).astype(BF16)


def _lru(z, layer, cw, cb, gw_bd, gb, lam):
    W = LRU_HALF
    nh = LRU_W // W
    return pl.pallas_call(
        _lru_kernel,
        grid=(B, nh),
        in_specs=[
            pl.BlockSpec((None, T, W), lambda b, c: (b, 0, c)),
            pl.BlockSpec((None, T, W), lambda b, c: (b, 0, nh + c)),
            pl.BlockSpec((None, LRU_TAPS, W), lambda b, c: (layer, 0, c)),
            pl.BlockSpec((None, 1, W), lambda b, c: (layer, 0, c)),
            pl.BlockSpec((None, None, 4, W, W), lambda b, c: (layer, c, 0, 0, 0)),
            pl.BlockSpec((None, 4, W), lambda b, c: (layer, 0, c)),
            pl.BlockSpec((None, 2, W), lambda b, c: (layer, 0, c)),
        ],
        out_specs=pl.BlockSpec((None, T, W), lambda b, c: (b, 0, c)),
        out_shape=jax.ShapeDtypeStruct((B, T, LRU_W), BF16),
        scratch_shapes=[pltpu.VMEM((T, W), F32)] * 4,
        compiler_params=_cparams(("parallel", "parallel")),
        name="lru",
    )(z, z, cw, cb, gw_bd, gb, lam)


def _attn_kernel(q_ref, k_ref, v_ref, cos_ref, sin_ref, bias_ref, o_ref,
                 qraw_s, qrot_s, qctx_s, krot_s, kc_s, v_s, sc_s, pc_s, ow_s):
    lane = lax.broadcasted_iota(jnp.int32, (1, LANES), 1)
    head0 = lane < HEAD
    second16 = ((lane // ROPE_F) % 2) == 1
    scale = HEAD ** -0.5 * LOG2E
    W2 = 2 * GRID_W

    def rope(x, c, s):
        partner = jnp.where(second16, pltpu.roll(x, ROPE_F, 1), pltpu.roll(x, LANES - ROPE_F, 1))
        return x * c + partner * s

    def stack_heads(x):
        return jnp.concatenate([jnp.where(head0, x, 0.0), jnp.where(head0, 0.0, x)], axis=0).astype(BF16)

    def unstack_heads(y):
        n = y.shape[0] // 2
        return jnp.where(head0, y[:n], y[n:])

    qctx_s[...] = stack_heads(q_ref[0:CTX, :] * scale)
    kc_s[...] = k_ref[0:CTX, :].astype(BF16)
    v_s[0:CTX, :] = v_ref[0:CTX, :].astype(BF16)
    rows_per_blk = ATT_RB // GRID_W
    for blk in range(SEQ // ATT_RB):
        r0 = blk * ATT_RB
        r1 = r0 + ATT_RB
        c = cos_ref[r0:r1, :]
        s = sin_ref[r0:r1, :]
        q = q_ref[CTX + r0:CTX + r1, :]
        qs = q * scale
        qr = rope(q, c, s) * scale
        for j in range(rows_per_blk):
            qraw_s[blk * rows_per_blk + j] = stack_heads(qs[j * GRID_W:(j + 1) * GRID_W])
            qrot_s[blk * rows_per_blk + j] = stack_heads(qr[j * GRID_W:(j + 1) * GRID_W])
        krot_s[r0:r1, :] = rope(k_ref[CTX + r0:CTX + r1, :], c, s).astype(BF16)
        v_s[CTX + r0:CTX + r1, :] = v_ref[CTX + r0:CTX + r1, :].astype(BF16)

    s = _dot_nt(qctx_s[...], kc_s[...])
    e = jnp.exp2(s - jnp.max(s, axis=-1, keepdims=True))
    p = (e / jnp.sum(e, axis=-1, keepdims=True)).astype(BF16)
    o_ref[0:CTX, :] = unstack_heads(_dot(p, v_s[0:CTX, :])).astype(BF16)

    for g in range(GRID_H // ATT_GROUP):
        g0, g1 = g * ATT_GROUP, (g + 1) * ATT_GROUP
        sc = _dot_nt(qraw_s[g0:g1].reshape(ATT_GROUP * W2, LANES), kc_s[...])
        sc_s[g0:g1] = sc.reshape(ATT_GROUP, W2, CTX)

    win = NA_ROWS * GRID_W

    def lane_fold(op, *xs):
        cols = [x[:, c:c + LANES] for x in xs for c in range(0, x.shape[1], LANES)]
        acc = cols[0]
        for col in cols[1:]:
            acc = op(acc, col)
        return acc

    def body(t, _):
        rows = [t * ATT_UNROLL + j for j in range(ATT_UNROLL)]
        starts = [jnp.clip(r - NA_ROWS // 2, 0, GRID_H - NA_ROWS) for r in rows]
        sw = [_dot_nt(qrot_s[r], krot_s[pl.ds(pl.multiple_of(rs * GRID_W, GRID_W), win), :]) + bias_ref[r - rs]
              for r, rs in zip(rows, starts)]
        sc = [sc_s[r] for r in rows]
        m = [jnp.max(lane_fold(jnp.maximum, a, c), axis=-1, keepdims=True) for a, c in zip(sw, sc)]
        ew = [jnp.exp2(a - mm) for a, mm in zip(sw, m)]
        ec = [jnp.exp2(c - mm) for c, mm in zip(sc, m)]
        inv = [1.0 / jnp.sum(lane_fold(jnp.add, a, c), axis=-1, keepdims=True) for a, c in zip(ew, ec)]
        for r, rs, a, c, iv in zip(rows, starts, ew, ec, inv):
            kx = pl.multiple_of(CTX + rs * GRID_W, GRID_W)
            ow_s[r] = _dot(a.astype(BF16), v_s[pl.ds(kx, win), :]) * iv
            pc_s[r] = (c * iv).astype(BF16)
        return 0

    lax.fori_loop(0, GRID_H // ATT_UNROLL, body, 0)

    for g in range(GRID_H // ATT_GROUP):
        g0, g1 = g * ATT_GROUP, (g + 1) * ATT_GROUP
        oc = _dot(pc_s[g0:g1].reshape(ATT_GROUP * W2, CTX), v_s[0:CTX, :]).reshape(ATT_GROUP, W2, LANES)
        tot = oc + ow_s[g0:g1]
        for j in range(ATT_GROUP):
            row = CTX + (g0 + j) * GRID_W
            o_ref[row:row + GRID_W, :] = unstack_heads(tot[j]).astype(BF16)


def _attn(z, layer, cos_t, sin_t, biasmask):
    nhp = NA_W // LANES
    qb = 2 * LRU_W // LANES
    return pl.pallas_call(
        _attn_kernel,
        grid=(nhp, B),
        in_specs=[
            pl.BlockSpec((None, T, LANES), lambda p, b: (b, 0, qb + p)),
            pl.BlockSpec((None, T, LANES), lambda p, b: (b, 0, qb + nhp + p)),
            pl.BlockSpec((None, T, LANES), lambda p, b: (b, 0, qb + 2 * nhp + p)),
            pl.BlockSpec((SEQ, LANES), lambda p, b: (0, 0)),
            pl.BlockSpec((SEQ, LANES), lambda p, b: (0, 0)),
            pl.BlockSpec((None, None, NA_ROWS, 2 * GRID_W, NA_ROWS * GRID_W), lambda p, b: (layer, p, 0, 0, 0)),
        ],
        out_specs=pl.BlockSpec((None, T, LANES), lambda p, b: (b, 0, p)),
        out_shape=jax.ShapeDtypeStruct((B, T, NA_W), BF16),
        scratch_shapes=[
            pltpu.VMEM((GRID_H, 2 * GRID_W, LANES), BF16),
            pltpu.VMEM((GRID_H, 2 * GRID_W, LANES), BF16),
            pltpu.VMEM((2 * CTX, LANES), BF16),
            pltpu.VMEM((SEQ, LANES), BF16),
            pltpu.VMEM((CTX, LANES), BF16),
            pltpu.VMEM((T, LANES), BF16),
            pltpu.VMEM((GRID_H, 2 * GRID_W, CTX), F32),
            pltpu.VMEM((GRID_H, 2 * GRID_W, CTX), BF16),
            pltpu.VMEM((GRID_H, 2 * GRID_W, LANES), F32),
        ],
        compiler_params=_cparams(("parallel", "parallel")),
        name="attn",
    )(z, z, z, cos_t, sin_t, biasmask)


def _sgu_kernel(u_ref, v_ref, g_ref, b_ref, ws_ref, bs_ref, o_ref):
    lane = lax.broadcasted_iota(jnp.int32, (1, LANES), 1)
    first = lane < HEAD
    for n in range(TM // SGU_CHUNK):
        r0 = n * SGU_CHUNK
        r1 = r0 + SGU_CHUNK
        v = jax.nn.gelu(v_ref[r0:r1, :])
        mu = jnp.mean(v, axis=-1, keepdims=True)
        var = jnp.mean(jnp.square(v - mu), axis=-1, keepdims=True)
        vn = ((v - mu) * lax.rsqrt(var + EPS) * g_ref[...] + b_ref[...]).astype(BF16)
        for j in range(SGU_W // LANES):
            c0 = j * LANES
            c1 = c0 + LANES
            vp = vn[:, c0:c1]
            mixed = jnp.where(first, _dot(ws_ref[2 * j], vp), _dot(ws_ref[2 * j + 1], vp)) + bs_ref[:, c0:c1]
            o_ref[r0:r1, c0:c1] = (jax.nn.gelu(u_ref[r0:r1, c0:c1]) * mixed).astype(BF16)


def _sgu(z, layer, ln_g, ln_b, ws_bf, bs_full):
    ub = (2 * LRU_W + 3 * NA_W) // SGU_W
    return pl.pallas_call(
        _sgu_kernel,
        grid=(B, NT),
        in_specs=[
            pl.BlockSpec((None, TM, SGU_W), lambda b, i: (b, i, ub)),
            pl.BlockSpec((None, TM, SGU_W), lambda b, i: (b, i, ub + 1)),
            pl.BlockSpec((None, 1, SGU_W), lambda b, i: (layer, 0, 0)),
            pl.BlockSpec((None, 1, SGU_W), lambda b, i: (layer, 0, 0)),
            pl.BlockSpec((None, SGU_G, SGU_CHUNK, SGU_CHUNK), lambda b, i: (layer, 0, 0, 0)),
            pl.BlockSpec((None, SGU_CHUNK, SGU_W), lambda b, i: (layer, 0, 0)),
        ],
        out_specs=pl.BlockSpec((None, TM, SGU_W), lambda b, i: (b, i, 0)),
        out_shape=jax.ShapeDtypeStruct((B, T, SGU_W), BF16),
        compiler_params=_cparams(("parallel", "parallel")),
        name="sgu",
    )(z, z, ln_g, ln_b, ws_bf, bs_full)


def _outproj_kernel(x_ref, a_ref, b_ref, c_ref, gtb, gtc, w_ref, o_ref):
    i = pl.program_id(1)
    acc = _dot(a_ref[...], w_ref[0:LRU_W, :])
    acc = acc + _dot(b_ref[...], w_ref[LRU_W:LRU_W + NA_W, :])
    acc = acc + _dot(c_ref[...], w_ref[LRU_W + NA_W:D, :])
    is_ctx = (lax.broadcasted_iota(jnp.int32, (TM, 1), 0) + i * TM) < CTX
    o_ref[...] = x_ref[...] + jnp.where(is_ctx, gtc[...], gtb[...]) * acc


def _outproj(xall, oa, ob, oc, mod5, layer, w_all):
    tn = D
    return pl.pallas_call(
        _outproj_kernel,
        grid=(B, NT, D // tn),
        in_specs=[
            pl.BlockSpec((None, TM, tn), lambda b, i, j: (b, i, j)),
            pl.BlockSpec((None, TM, LRU_W), lambda b, i, j: (b, i, 0)),
            pl.BlockSpec((None, TM, NA_W), lambda b, i, j: (b, i, 0)),
            pl.BlockSpec((None, TM, SGU_W), lambda b, i, j: (b, i, 0)),
            _mod_spec(layer, 2, False, tn, 2), _mod_spec(layer, 2, True, tn, 2),
            pl.BlockSpec((None, D, tn), lambda b, i, j: (layer, 0, j), pipeline_mode=pl.Buffered(1)),
        ],
        out_specs=pl.BlockSpec((None, TM, tn), lambda b, i, j: (b, i, j)),
        out_shape=jax.ShapeDtypeStruct((B, T, D), F32),
        compiler_params=_cparams(("parallel", "parallel", "parallel"), VMEM_BIG),
        name="outproj",
    )(xall, oa, ob, oc, mod5, mod5, w_all)


def _ffn_kernel(x_ref, xp_ref, xn_ref, g_ref, shb, scb, gtb, shc, scc, gtc, wa_ref, wg_ref, cw_ref, cb_ref, wd_ref,
                o_ref, hx_s, u_s, act_s):
    i = pl.program_id(1)
    k = pl.program_id(2)
    rb = FFN_RB
    left_blocks = (0, CTX // rb)
    right_blocks = (CTX // rb - 1, TM // rb - 1)

    @pl.when(k == 0)
    def _():
        gain_b = g_ref[...] * (1.0 + scb[...])
        _norm_rows(xp_ref, hx_s, 0, 0, HALO, gain_b, shb[...])
        _norm_rows(xn_ref, hx_s, HALO + TM, 0, HALO, gain_b, shb[...])
        _norm_tile(i, x_ref, hx_s, HALO, g_ref, shb, scb, shc, scc)
        o_ref[...] = jnp.zeros_like(o_ref)

    u_s[:, :TH] = _dot(hx_s[...], wa_ref[...])
    u_s[:, TH:] = _dot(hx_s[...], wg_ref[...])
    cw = cw_ref[...]
    cb = cb_ref[...]
    for blk in range(TM // rb):
        r0 = HALO + blk * rb
        rows = lax.broadcasted_iota(jnp.int32, (rb, 1), 0) + (i * TM + blk * rb)
        ul = u_s[r0 - 1:r0 - 1 + rb, :]
        ur = u_s[r0 + 1:r0 + 1 + rb, :]
        if blk in left_blocks:
            ul = jnp.where(jnp.logical_and(rows != 0, rows != CTX), ul, 0.0)
        if blk in right_blocks:
            ur = jnp.where(jnp.logical_and(rows != CTX - 1, rows != T - 1), ur, 0.0)
        y = cb + ul * cw[0:1]
        y = y + u_s[r0:r0 + rb, :] * cw[1:2]
        y = y + ur * cw[2:3]
        a = y[:, :TH]
        gg = y[:, TH:]
        act_s[blk * rb:(blk + 1) * rb, :] = (gg * jax.nn.sigmoid(gg) * a).astype(BF16)
    o_ref[...] += _dot(act_s[...], wd_ref[...])

    @pl.when(k == NK - 1)
    def _():
        def finish(lo, hi, gate):
            o_ref[lo:hi, :] = x_ref[lo:hi, :] + gate * o_ref[lo:hi, :]
        _per_segment(i, finish, gtb[...], gtc[...])


def _ffn(xall, mod5, layer, g, wa, wg, cw_r, cb_r, wd_p):
    hb = TM // HALO
    last = T // HALO - 1
    col_map = lambda b, i, k: (layer, 0, k)
    return pl.pallas_call(
        _ffn_kernel,
        grid=(B, NT, NK),
        in_specs=[
            pl.BlockSpec((None, TM, D), lambda b, i, k: (b, i, 0)),
            pl.BlockSpec((None, HALO, D), lambda b, i, k: (b, jnp.maximum(i * hb - 1, 0), 0)),
            pl.BlockSpec((None, HALO, D), lambda b, i, k: (b, jnp.minimum((i + 1) * hb, last), 0)),
            pl.BlockSpec((None, 1, D), lambda b, i, k: (layer, 0, 0)),
            _mod_spec(layer, 3, False), _mod_spec(layer, 4, False), _mod_spec(layer, 5, False),
            _mod_spec(layer, 3, True), _mod_spec(layer, 4, True), _mod_spec(layer, 5, True),
            pl.BlockSpec((None, D, TH), col_map),
            pl.BlockSpec((None, D, TH), col_map),
            pl.BlockSpec((None, 3, 2 * TH), col_map),
            pl.BlockSpec((None, 1, 2 * TH), col_map),
            pl.BlockSpec((None, TH, D), lambda b, i, k: (layer, k, 0)),
        ],
        out_specs=pl.BlockSpec((None, TM, D), lambda b, i, k: (b, i, 0)),
        out_shape=jax.ShapeDtypeStruct((B, T, D), F32),
        scratch_shapes=[
            pltpu.VMEM((TM + 2 * HALO, D), BF16),
            pltpu.VMEM((TM + 2 * HALO, 2 * TH), F32),
            pltpu.VMEM((TM, TH), BF16),
        ],
        compiler_params=_cparams(("parallel", "parallel", "arbitrary"), VMEM_BIG),
        name="ffn",
    )(xall, xall, xall, g, mod5, mod5, mod5, mod5, mod5, mod5, wa, wg, cw_r, cb_r, wd_p)


def _final_kernel(x_ref, g_ref, o_ref):
    x = x_ref[...]
    o_ref[...] = x * lax.rsqrt(jnp.mean(x * x, axis=-1, keepdims=True) + EPS) * g_ref[...]


def _final_norm(xall, g):
    tm = CTX
    return pl.pallas_call(
        _final_kernel,
        grid=(B, SEQ // tm),
        in_specs=[
            pl.BlockSpec((None, tm, D), lambda b, i: (b, i + 1, 0)),
            pl.BlockSpec((1, D), lambda b, i: (0, 0)),
        ],
        out_specs=pl.BlockSpec((None, tm, D), lambda b, i: (b, i, 0)),
        out_shape=jax.ShapeDtypeStruct((B, SEQ, D), F32),
        compiler_params=_cparams(("parallel", "parallel")),
        name="final_norm",
    )(xall, g.reshape(1, D))


def _rope_tables():
    t = jnp.arange(SEQ)
    pos = jnp.stack([t // GRID_W, t % GRID_W], axis=-1).astype(F32)
    inv = ROPE_BASE ** (-jnp.arange(ROPE_F, dtype=F32) / ROPE_F)
    ang = pos[:, :, None] * inv
    cos, sin = jnp.cos(ang), jnp.sin(ang)
    cos_h = jnp.concatenate([cos[:, 0], cos[:, 0], cos[:, 1], cos[:, 1]], axis=-1)
    sin_h = jnp.concatenate([-sin[:, 0], sin[:, 0], -sin[:, 1], sin[:, 1]], axis=-1)
    reps = LANES // HEAD
    return jnp.tile(cos_h, (1, reps)), jnp.tile(sin_h, (1, reps))


def _bias_tables(rpb):
    nl = rpb.shape[0]
    nr, nc = 2 * NA_ROWS - 1, 2 * NA_KC - 1
    pairs = rpb.reshape(nl, NA_HEADS // 2, 2, nr, nc) * LOG2E
    pairs = jnp.pad(pairs, ((0, 0), (0, 0), (0, 0), (0, 2 * NA_ROWS - nr), (0, LANES - nc)))
    return pl.pallas_call(
        _bias_kernel,
        grid=(nl, NA_HEADS // 2),
        in_specs=[pl.BlockSpec((None, None, 2, 2 * NA_ROWS, LANES), lambda l, p: (l, p, 0, 0, 0))],
        out_specs=pl.BlockSpec((None, None, NA_ROWS, 2 * GRID_W, NA_ROWS * GRID_W), lambda l, p: (l, p, 0, 0, 0)),
        out_shape=jax.ShapeDtypeStruct((nl, NA_HEADS // 2, NA_ROWS, 2 * GRID_W, NA_ROWS * GRID_W), F32),
        compiler_params=_cparams(("parallel", "parallel")),
        name="bias_table",
    )(pairs)


def _bias_kernel(rp_ref, o_ref):
    c = lax.broadcasted_iota(jnp.int32, (GRID_W, LANES), 0)
    lane = lax.broadcasted_iota(jnp.int32, (GRID_W, LANES), 1)
    kc = lane % GRID_W
    col_start = jnp.clip(c - NA_KC // 2, 0, GRID_W - NA_KC)
    in_win = jnp.logical_and(kc >= col_start, kc < col_start + NA_KC)
    first = lane < GRID_W
    per_blk = LANES // GRID_W
    for cls in range(NA_ROWS):
        for hd in range(2):
            for blk in range(NA_ROWS // per_blk):
                parts = []
                for h in range(per_blk):
                    m = blk * per_blk + h - cls + NA_ROWS - 1
                    row = jnp.broadcast_to(rp_ref[hd, m:m + 1, :], (GRID_W, LANES))
                    shift = (LANES - (NA_KC - 1) + h * GRID_W) % LANES
                    parts.append(pltpu.roll(row, shift, 1, stride=1, stride_axis=0))
                tile = jnp.where(in_win, jnp.where(first, parts[0], parts[1]), NEG)
                o_ref[cls, hd * GRID_W:(hd + 1) * GRID_W, blk * LANES:(blk + 1) * LANES] = tile


def _gate_blockdiag(gate_w):
    nl = gate_w.shape[0]
    nh = LRU_W // LRU_HALF
    gper = LRU_HALF // HEAD
    w = gate_w.reshape(nl, 4, nh, gper, HEAD, HEAD)
    eye = jnp.eye(gper, dtype=gate_w.dtype)
    bd = jnp.einsum('lkhgio,gj->lkhgijo', w, eye).reshape(nl, 4, nh, LRU_HALF, LRU_HALF)
    return jnp.transpose(bd, (0, 2, 1, 3, 4)).astype(BF16)


def _ffn_layout(w_up, conv_w, conv_b, w_down):
    pad = FFN_HP - FFN_H
    nl = w_up.shape[0]

    def inter(m):
        r = m.shape[1]
        a = jnp.pad(m[..., :FFN_H], ((0, 0), (0, 0), (0, pad))).reshape(nl, r, NK, 1, TH)
        g = jnp.pad(m[..., FFN_H:], ((0, 0), (0, 0), (0, pad))).reshape(nl, r, NK, 1, TH)
        return jnp.concatenate([a, g], axis=3).reshape(nl, r, NK * 2 * TH)

    zc = jnp.zeros((nl, D, pad), BF16)
    wa = jnp.concatenate([w_up[..., :FFN_H].astype(BF16), zc], axis=2)
    wg = jnp.concatenate([w_up[..., FFN_H:].astype(BF16), zc], axis=2)
    wd = jnp.concatenate([w_down.astype(BF16), jnp.zeros((nl, pad, D), BF16)], axis=1)
    return wa, wg, inter(conv_w), inter(conv_b[:, None, :]), wd


def kernel(x, c, ctx, c_ctx, w_ada, b_ada, norm_mix_g, norm_ffn_g, w_in, lru_conv_w, lru_conv_b, lru_gate_w, lru_gate_b, lru_lambda, na_rpb, sgu_ln_g, sgu_ln_b, sgu_w, sgu_b, w_out, ffn_up, ffn_conv_w, ffn_conv_b, ffn_down, final_norm_g):
    xall = jnp.concatenate([ctx, x], axis=1)
    cc = jnp.concatenate([c, c_ctx[None], jnp.zeros((MOD_ROWS - B - 1, D), F32)], axis=0)
    mod5 = _adaln(cc, w_ada, b_ada).reshape(DEPTH, 6, MOD_ROWS, 1, D)
    cos_t, sin_t = _rope_tables()

    g_mix = norm_mix_g.reshape(DEPTH, 1, D)
    g_ffn = norm_ffn_g.reshape(DEPTH, 1, D)
    w_in_bf = w_in.astype(BF16)
    w_out_bf = w_out.astype(BF16)
    lru_cb = lru_conv_b.reshape(DEPTH, 1, LRU_W)
    lru_gw = _gate_blockdiag(lru_gate_w)
    lru_gb = lru_gate_b.reshape(DEPTH, 4, LRU_W)
    bias = _bias_tables(na_rpb)
    sgu_g = sgu_ln_g.reshape(DEPTH, 1, SGU_W)
    sgu_bb = sgu_ln_b.reshape(DEPTH, 1, SGU_W)
    sgu_w_bf = sgu_w.astype(BF16)
    sgu_bias = jnp.repeat(jnp.swapaxes(sgu_b, 1, 2), HEAD, axis=2)
    ffn_params = _ffn_layout(ffn_up, ffn_conv_w, ffn_conv_b, ffn_down)

    for l in range(DEPTH):
        z = _inproj(xall, mod5, l, g_mix, w_in_bf)
        oa = _lru(z, l, lru_conv_w, lru_cb, lru_gw, lru_gb, lru_lambda)
        ob = _attn(z, l, cos_t, sin_t, bias)
        oc = _sgu(z, l, sgu_g, sgu_bb, sgu_w_bf, sgu_bias)
        xall = _outproj(xall, oa, ob, oc, mod5, l, w_out_bf)
        xall = _ffn(xall, mod5, l, g_ffn, *ffn_params)
    return _final_norm(xall, final_norm_g)
```

```python
import functools

import jax
import jax.numpy as jnp
from jax import lax
from jax.experimental import pallas as pl
from jax.experimental.pallas import tpu as pltpu

F32 = jnp.float32
BF16 = jnp.bfloat16

D = 2048
B = 8
SEQ = 2048
DEPTH = 4
GRID_W = 64
GRID_H = SEQ // GRID_W
CTX = 256
T = CTX + SEQ
HEAD = 64
LRU_W = 512
LRU_TAPS = 4
LRU_C = 8.0
NA_W = 1024
NA_HEADS = NA_W // HEAD
NA_ROWS = 8
NA_KC = 16
SGU_W = 512
SGU_G = SGU_W // HEAD
SGU_CHUNK = 128
PROJ = 2 * LRU_W + 3 * NA_W + 2 * SGU_W
FFN_H = 5504
EPS = 1e-6
NEG = -1e30
LOG2E = 1.4426950408889634
ROPE_BASE = 10000.0
ROPE_F = HEAD // 4
MOD_ROWS = 16
CTX_ROW = B

LANES = 128
SUBLANES = 8
VMEM_BIG = 56 * 1024 * 1024
VMEM_MID = 40 * 1024 * 1024

TM = 768
NT = T // TM
HALO = 16
TH = 512
FFN_HP = 5632
NK = FFN_HP // TH
LRU_HALF = 256
LRU_RB = 128
ATT_RB = 256
NORM_RB = 16
NORM_UNROLL = 4
FFN_RB = 128
ATT_UNROLL = 4
ATT_GROUP = 4


def _cparams(sem, vmem=VMEM_MID):
    return pltpu.CompilerParams(dimension_semantics=sem, vmem_limit_bytes=vmem)


def _dot(a, b):
    return jnp.dot(a, b, preferred_element_type=F32)


def _dot_nt(a, b):
    return lax.dot_general(a, b, (((1,), (1,)), ((), ())), preferred_element_type=F32)


def _mod_spec(layer, which, ctx, width=D, col=None):
    def imap(*ids):
        b = ids[0]
        row = CTX_ROW if ctx else b
        c = 0 if col is None else ids[col]
        return (layer, which, row, 0, c)
    return pl.BlockSpec((None, None, None, 1, width), imap)


def _per_segment(i, fn, latent_arg, ctx_arg):
    @pl.when(i == 0)
    def _():
        fn(0, CTX, ctx_arg)
        fn(CTX, TM, latent_arg)

    @pl.when(i != 0)
    def _():
        fn(0, TM, latent_arg)


def _norm_rows(x_ref, dst_ref, dst_off, lo, hi, gain, shift):
    def body(j, _):
        r = pl.multiple_of(lo + j * NORM_RB, NORM_RB)
        x = x_ref[pl.ds(r, NORM_RB), :]
        inv = lax.rsqrt(jnp.mean(x * x, axis=-1, keepdims=True) + EPS)
        dst_ref[pl.ds(dst_off + r, NORM_RB), :] = ((x * inv) * gain + shift).astype(BF16)
        return 0
    n = (hi - lo) // NORM_RB
    lax.fori_loop(0, n, body, 0, unroll=min(n, NORM_UNROLL))


def _norm_tile(i, x_ref, dst_ref, dst_off, g_ref, shb, scb, shc, scc):
    g = g_ref[...]

    def run(lo, hi, mod):
        _norm_rows(x_ref, dst_ref, dst_off, lo, hi, g * (1.0 + mod[0]), mod[1])
    _per_segment(i, run, (scb[...], shb[...]), (scc[...], shc[...]))


def _adaln_kernel(c_ref, w_ref, b_ref, o_ref):
    c = c_ref[...]
    s = (c * jax.nn.sigmoid(c)).astype(BF16)
    o_ref[...] = _dot(s, w_ref[...].astype(BF16)) + b_ref[...]


def _adaln(cc, w_ada, b_ada):
    tn = 1024
    nj = D // tn
    return pl.pallas_call(
        _adaln_kernel,
        grid=(DEPTH, 6 * nj),
        in_specs=[
            pl.BlockSpec((MOD_ROWS, D), lambda l, j: (0, 0)),
            pl.BlockSpec((None, D, tn), lambda l, j: (l, 0, j)),
            pl.BlockSpec((None, 1, tn), lambda l, j: (l, 0, j)),
        ],
        out_specs=pl.BlockSpec((None, None, MOD_ROWS, tn), lambda l, j: (l, j // nj, 0, j % nj)),
        out_shape=jax.ShapeDtypeStruct((DEPTH, 6, MOD_ROWS, D), F32),
        compiler_params=_cparams(("parallel", "parallel")),
        name="adaln",
    )(cc, w_ada, b_ada.reshape(DEPTH, 1, 6 * D))


def _inproj_kernel(x_ref, g_ref, shb, scb, shc, scc, w_ref, o_ref, hx_s):
    i = pl.program_id(1)

    @pl.when(pl.program_id(2) == 0)
    def _():
        _norm_tile(i, x_ref, hx_s, 0, g_ref, shb, scb, shc, scc)

    o_ref[...] = _dot(hx_s[...], w_ref[...])


def _inproj(xall, mod5, layer, g_all, w_all):
    tn = 1024
    return pl.pallas_call(
        _inproj_kernel,
        grid=(B, NT, PROJ // tn),
        in_specs=[
            pl.BlockSpec((None, TM, D), lambda b, i, j: (b, i, 0)),
            pl.BlockSpec((None, 1, D), lambda b, i, j: (layer, 0, 0)),
            _mod_spec(layer, 0, False), _mod_spec(layer, 1, False),
            _mod_spec(layer, 0, True), _mod_spec(layer, 1, True),
            pl.BlockSpec((None, D, tn), lambda b, i, j: (layer, 0, j)),
        ],
        out_specs=pl.BlockSpec((None, TM, tn), lambda b, i, j: (b, i, j)),
        out_shape=jax.ShapeDtypeStruct((B, T, PROJ), F32),
        scratch_shapes=[pltpu.VMEM((TM, D), BF16)],
        compiler_params=_cparams(("parallel", "parallel", "arbitrary")),
        name="inproj",
    )(xall, g_all, mod5, mod5, mod5, mod5, w_all)


def _tile_scan(a, b, carry, rev):
    rid = lax.broadcasted_iota(jnp.int32, a.shape, 0)
    for d in (1, 2, 4):
        sh = SUBLANES - d if rev else d
        keep = (rid < SUBLANES - d) if rev else (rid >= d)
        a_s = jnp.where(keep, pltpu.roll(a, sh, 0), 1.0)
        b_s = jnp.where(keep, pltpu.roll(b, sh, 0), 0.0)
        b = a * b_s + b
        a = a * a_s
    h = a * carry + b
    return h, (h[0:1] if rev else h[SUBLANES - 1:SUBLANES])


def _lru_kernel(ax_ref, ay_ref, cw_ref, cb_ref, gw_ref, gb_ref, lam_ref, o_ref, af, bf, ab, bb):
    W = LRU_HALF
    cw = cw_ref[...]
    cb = cb_ref[...]
    gb = gb_ref[...]
    nl = -lam_ref[...]
    softplus = jnp.maximum(nl, 0.0) + jnp.log(1.0 + jnp.exp(-jnp.abs(nl)))
    coef = -LRU_C * softplus
    zeros8 = jnp.zeros((SUBLANES, W), F32)
    n_ext = LRU_RB + 2 * SUBLANES

    for blk in range(T // LRU_RB):
        r0 = blk * LRU_RB
        r1 = r0 + LRU_RB
        prev = zeros8 if r0 in (0, CTX) else ax_ref[r0 - SUBLANES:r0, :]
        nxt = zeros8 if r1 in (CTX, T) else ax_ref[r1:r1 + SUBLANES, :]
        ext = jnp.concatenate([prev, ax_ref[r0:r1, :], nxt], axis=0)
        lo, hi = SUBLANES, SUBLANES + LRU_RB
        xc = cb + pltpu.roll(ext, 2, 0)[lo:hi] * cw[0:1]
        xc = xc + pltpu.roll(ext, 1, 0)[lo:hi] * cw[1:2]
        xc = xc + ext[lo:hi] * cw[2:3]
        xc = xc + pltpu.roll(ext, n_ext - 1, 0)[lo:hi] * cw[3:4]
        xb = xc.astype(BF16)
        for d, (a_s, b_s) in enumerate(((af, bf), (ab, bb))):
            r = jax.nn.sigmoid(_dot(xb, gw_ref[2 * d]) + gb[2 * d:2 * d + 1])
            gi = jax.nn.sigmoid(_dot(xb, gw_ref[2 * d + 1]) + gb[2 * d + 1:2 * d + 2])
            log_a = coef[d:d + 1] * r
            a_s[r0:r1, :] = jnp.exp(log_a)
            th = jnp.tanh(log_a)
            b_s[r0:r1, :] = jnp.sqrt(-2.0 * th / (1.0 - th)) * (gi * xc)

    def make_body(f_base, b_top):
        def body(k, carry):
            cf, cr = carry
            rf = pl.multiple_of((f_base + k) * SUBLANES, SUBLANES)
            hf, cf = _tile_scan(af[pl.ds(rf, SUBLANES), :], bf[pl.ds(rf, SUBLANES), :], cf, False)
            bf[pl.ds(rf, SUBLANES), :] = hf
            rb = pl.multiple_of((b_top - k) * SUBLANES, SUBLANES)
            hb, cr = _tile_scan(ab[pl.ds(rb, SUBLANES), :], bb[pl.ds(rb, SUBLANES), :], cr, True)
            bb[pl.ds(rb, SUBLANES), :] = hb
            return cf, cr
        return body

    z1 = jnp.zeros((1, W), F32)
    n_ctx = CTX // SUBLANES
    n_all = T // SUBLANES
    carry = lax.fori_loop(0, n_ctx, make_body(0, n_ctx - 1), (z1, z1))
    lax.fori_loop(0, n_all - n_ctx, make_body(n_ctx, n_all - 1), carry)

    for blk in range(T // LRU_RB):
        r0 = blk * LRU_RB
        r1 = r0 + LRU_RB
        h = bf[r0:r1, :] + bb[r0:r1, :]
        o_ref[r0:r1, :] = (h * jax.nn.gelu(ay_ref[r0:r1, :])).astype(BF16)


def _lru(z, layer, cw, cb, gw_bd, gb, lam):
    W = LRU_HALF
    nh = LRU_W // W
    return pl.pallas_call(
        _lru_kernel,
        grid=(B, nh),
        in_specs=[
            pl.BlockSpec((None, T, W), lambda b, c: (b, 0, c)),
            pl.BlockSpec((None, T, W), lambda b, c: (b, 0, nh + c)),
            pl.BlockSpec((None, LRU_TAPS, W), lambda b, c: (layer, 0, c)),
            pl.BlockSpec((None, 1, W), lambda b, c: (layer, 0, c)),
            pl.BlockSpec((None, None, 4, W, W), lambda b, c: (layer, c, 0, 0, 0)),
            pl.BlockSpec((None, 4, W), lambda b, c: (layer, 0, c)),
            pl.BlockSpec((None, 2, W), lambda b, c: (layer, 0, c)),
        ],
        out_specs=pl.BlockSpec((None, T, W), lambda b, c: (b, 0, c)),
        out_shape=jax.ShapeDtypeStruct((B, T, LRU_W), BF16),
        scratch_shapes=[pltpu.VMEM((T, W), F32)] * 4,
        compiler_params=_cparams(("parallel", "parallel")),
        name="lru",
    )(z, z, cw, cb, gw_bd, gb, lam)


def _attn_kernel(q_ref, k_ref, v_ref, cos_ref, sin_ref, bias_ref, o_ref,
                 qraw_s, qrot_s, qctx_s, krot_s, kc_s, v_s, sc_s, pc_s, ow_s):
    lane = lax.broadcasted_iota(jnp.int32, (1, LANES), 1)
    head0 = lane < HEAD
    qk_head0 = (lane % HEAD) < HEAD // 2
    scale = HEAD ** -0.5 * LOG2E
    W2 = 2 * GRID_W

    def rope(x, c, s):
        return x * c + pltpu.roll(x, LANES // 2, 1) * s

    def stack_heads(x):
        return jnp.concatenate([jnp.where(qk_head0, x, 0.0), jnp.where(qk_head0, 0.0, x)], axis=0).astype(BF16)

    def unstack_heads(y):
        n = y.shape[0] // 2
        return jnp.where(head0, y[:n], y[n:])

    qctx_s[...] = stack_heads(q_ref[0:CTX, :] * scale)
    kc_s[...] = k_ref[0:CTX, :].astype(BF16)
    v_s[0:CTX, :] = v_ref[0:CTX, :].astype(BF16)
    rows_per_blk = ATT_RB // GRID_W
    for blk in range(SEQ // ATT_RB):
        r0 = blk * ATT_RB
        r1 = r0 + ATT_RB
        c = cos_ref[r0:r1, :]
        s = sin_ref[r0:r1, :]
        q = q_ref[CTX + r0:CTX + r1, :]
        qs = q * scale
        qr = rope(q, c, s) * scale
        for j in range(rows_per_blk):
            qraw_s[blk * rows_per_blk + j] = stack_heads(qs[j * GRID_W:(j + 1) * GRID_W])
            qrot_s[blk * rows_per_blk + j] = stack_heads(qr[j * GRID_W:(j + 1) * GRID_W])
        krot_s[r0:r1, :] = rope(k_ref[CTX + r0:CTX + r1, :], c, s).astype(BF16)
        v_s[CTX + r0:CTX + r1, :] = v_ref[CTX + r0:CTX + r1, :].astype(BF16)

    s = _dot_nt(qctx_s[...], kc_s[...])
    e = jnp.exp2(s - jnp.max(s, axis=-1, keepdims=True))
    p = (e / jnp.sum(e, axis=-1, keepdims=True)).astype(BF16)
    o_ref[0:CTX, :] = unstack_heads(_dot(p, v_s[0:CTX, :])).astype(BF16)

    for g in range(GRID_H // ATT_GROUP):
        g0, g1 = g * ATT_GROUP, (g + 1) * ATT_GROUP
        sc = _dot_nt(qraw_s[g0:g1].reshape(ATT_GROUP * W2, LANES), kc_s[...])
        sc_s[g0:g1] = sc.reshape(ATT_GROUP, W2, CTX)

    win = NA_ROWS * GRID_W

    def lane_fold(op, *xs):
        cols = [x[:, c:c + LANES] for x in xs for c in range(0, x.shape[1], LANES)]
        acc = cols[0]
        for col in cols[1:]:
            acc = op(acc, col)
        return acc

    def body(t, _):
        rows = [t * ATT_UNROLL + j for j in range(ATT_UNROLL)]
        starts = [jnp.clip(r - NA_ROWS // 2, 0, GRID_H - NA_ROWS) for r in rows]
        sw = [_dot_nt(qrot_s[r], krot_s[pl.ds(pl.multiple_of(rs * GRID_W, GRID_W), win), :]) + bias_ref[r - rs]
              for r, rs in zip(rows, starts)]
        sc = [sc_s[r] for r in rows]
        m = [jnp.max(lane_fold(jnp.maximum, a, c), axis=-1, keepdims=True) for a, c in zip(sw, sc)]
        ew = [jnp.exp2(a - mm) for a, mm in zip(sw, m)]
        ec = [jnp.exp2(c - mm) for c, mm in zip(sc, m)]
        inv = [1.0 / jnp.sum(lane_fold(jnp.add, a, c), axis=-1, keepdims=True) for a, c in zip(ew, ec)]
        for r, rs, a, c, iv in zip(rows, starts, ew, ec, inv):
            kx = pl.multiple_of(CTX + rs * GRID_W, GRID_W)
            ow_s[r] = _dot(a.astype(BF16), v_s[pl.ds(kx, win), :]) * iv
            pc_s[r] = (c * iv).astype(BF16)
        return 0

    lax.fori_loop(0, GRID_H // ATT_UNROLL, body, 0)

    for g in range(GRID_H // ATT_GROUP):
        g0, g1 = g * ATT_GROUP, (g + 1) * ATT_GROUP
        oc = _dot(pc_s[g0:g1].reshape(ATT_GROUP * W2, CTX), v_s[0:CTX, :]).reshape(ATT_GROUP, W2, LANES)
        tot = oc + ow_s[g0:g1]
        for j in range(ATT_GROUP):
            row = CTX + (g0 + j) * GRID_W
            o_ref[row:row + GRID_W, :] = unstack_heads(tot[j]).astype(BF16)


def _attn(z, layer, cos_t, sin_t, biasmask):
    nhp = NA_W // LANES
    qb = 2 * LRU_W // LANES
    return pl.pallas_call(
        _attn_kernel,
        grid=(nhp, B),
        in_specs=[
            pl.BlockSpec((None, T, LANES), lambda p, b: (b, 0, qb + p)),
            pl.BlockSpec((None, T, LANES), lambda p, b: (b, 0, qb + nhp + p)),
            pl.BlockSpec((None, T, LANES), lambda p, b: (b, 0, qb + 2 * nhp + p)),
            pl.BlockSpec((SEQ, LANES), lambda p, b: (0, 0)),
            pl.BlockSpec((SEQ, LANES), lambda p, b: (0, 0)),
            pl.BlockSpec((None, None, NA_ROWS, 2 * GRID_W, NA_ROWS * GRID_W), lambda p, b: (layer, p, 0, 0, 0)),
        ],
        out_specs=pl.BlockSpec((None, T, LANES), lambda p, b: (b, 0, p)),
        out_shape=jax.ShapeDtypeStruct((B, T, NA_W), BF16),
        scratch_shapes=[
            pltpu.VMEM((GRID_H, 2 * GRID_W, LANES), BF16),
            pltpu.VMEM((GRID_H, 2 * GRID_W, LANES), BF16),
            pltpu.VMEM((2 * CTX, LANES), BF16),
            pltpu.VMEM((SEQ, LANES), BF16),
            pltpu.VMEM((CTX, LANES), BF16),
            pltpu.VMEM((T, LANES), BF16),
            pltpu.VMEM((GRID_H, 2 * GRID_W, CTX), F32),
            pltpu.VMEM((GRID_H, 2 * GRID_W, CTX), BF16),
            pltpu.VMEM((GRID_H, 2 * GRID_W, LANES), F32),
        ],
        compiler_params=_cparams(("parallel", "parallel")),
        name="attn",
    )(z, z, z, cos_t, sin_t, biasmask)


def _sgu_kernel(u_ref, v_ref, g_ref, b_ref, ws_ref, bs_ref, o_ref):
    lane = lax.broadcasted_iota(jnp.int32, (1, LANES), 1)
    first = lane < HEAD
    for n in range(TM // SGU_CHUNK):
        r0 = n * SGU_CHUNK
        r1 = r0 + SGU_CHUNK
        v = jax.nn.gelu(v_ref[r0:r1, :])
        mu = jnp.mean(v, axis=-1, keepdims=True)
        var = jnp.mean(jnp.square(v - mu), axis=-1, keepdims=True)
        vn = ((v - mu) * lax.rsqrt(var + EPS) * g_ref[...] + b_ref[...]).astype(BF16)
        for j in range(SGU_W // LANES):
            c0 = j * LANES
            c1 = c0 + LANES
            vp = vn[:, c0:c1]
            mixed = jnp.where(first, _dot(ws_ref[2 * j], vp), _dot(ws_ref[2 * j + 1], vp)) + bs_ref[:, c0:c1]
            o_ref[r0:r1, c0:c1] = (jax.nn.gelu(u_ref[r0:r1, c0:c1]) * mixed).astype(BF16)


def _sgu(z, layer, ln_g, ln_b, ws_bf, bs_full):
    ub = (2 * LRU_W + 3 * NA_W) // SGU_W
    return pl.pallas_call(
        _sgu_kernel,
        grid=(B, NT),
        in_specs=[
            pl.BlockSpec((None, TM, SGU_W), lambda b, i: (b, i, ub)),
            pl.BlockSpec((None, TM, SGU_W), lambda b, i: (b, i, ub + 1)),
            pl.BlockSpec((None, 1, SGU_W), lambda b, i: (layer, 0, 0)),
            pl.BlockSpec((None, 1, SGU_W), lambda b, i: (layer, 0, 0)),
            pl.BlockSpec((None, SGU_G, SGU_CHUNK, SGU_CHUNK), lambda b, i: (layer, 0, 0, 0)),
            pl.BlockSpec((None, SGU_CHUNK, SGU_W), lambda b, i: (layer, 0, 0)),
        ],
        out_specs=pl.BlockSpec((None, TM, SGU_W), lambda b, i: (b, i, 0)),
        out_shape=jax.ShapeDtypeStruct((B, T, SGU_W), BF16),
        compiler_params=_cparams(("parallel", "parallel")),
        name="sgu",
    )(z, z, ln_g, ln_b, ws_bf, bs_full)


def _outproj_kernel(x_ref, a_ref, b_ref, c_ref, gtb, gtc, w_ref, o_ref):
    i = pl.program_id(1)
    acc = _dot(a_ref[...], w_ref[0:LRU_W, :])
    acc = acc + _dot(b_ref[...], w_ref[LRU_W:LRU_W + NA_W, :])
    acc = acc + _dot(c_ref[...], w_ref[LRU_W + NA_W:D, :])
    is_ctx = (lax.broadcasted_iota(jnp.int32, (TM, 1), 0) + i * TM) < CTX
    o_ref[...] = x_ref[...] + jnp.where(is_ctx, gtc[...], gtb[...]) * acc


def _outproj(xall, oa, ob, oc, mod5, layer, w_all):
    tn = D
    return pl.pallas_call(
        _outproj_kernel,
        grid=(B, NT, D // tn),
        in_specs=[
            pl.BlockSpec((None, TM, tn), lambda b, i, j: (b, i, j)),
            pl.BlockSpec((None, TM, LRU_W), lambda b, i, j: (b, i, 0)),
            pl.BlockSpec((None, TM, NA_W), lambda b, i, j: (b, i, 0)),
            pl.BlockSpec((None, TM, SGU_W), lambda b, i, j: (b, i, 0)),
            _mod_spec(layer, 2, False, tn, 2), _mod_spec(layer, 2, True, tn, 2),
            pl.BlockSpec((None, D, tn), lambda b, i, j: (layer, 0, j), pipeline_mode=pl.Buffered(1)),
        ],
        out_specs=pl.BlockSpec((None, TM, tn), lambda b, i, j: (b, i, j)),
        out_shape=jax.ShapeDtypeStruct((B, T, D), F32),
        compiler_params=_cparams(("parallel", "parallel", "parallel"), VMEM_BIG),
        name="outproj",
    )(xall, oa, ob, oc, mod5, mod5, w_all)


def _ffn_kernel(x_ref, xp_ref, xn_ref, g_ref, shb, scb, gtb, shc, scc, gtc, wa_ref, wg_ref, cw_ref, cb_ref, wd_ref,
                o_ref, hx_s, u_s, act_s):
    i = pl.program_id(1)
    k = pl.program_id(2)
    rb = FFN_RB
    left_blocks = (0, CTX // rb)
    right_blocks = (CTX // rb - 1, TM // rb - 1)

    @pl.when(k == 0)
    def _():
        gain_b = g_ref[...] * (1.0 + scb[...])
        _norm_rows(xp_ref, hx_s, 0, 0, HALO, gain_b, shb[...])
        _norm_rows(xn_ref, hx_s, HALO + TM, 0, HALO, gain_b, shb[...])
        _norm_tile(i, x_ref, hx_s, HALO, g_ref, shb, scb, shc, scc)
        o_ref[...] = jnp.zeros_like(o_ref)

    def chunk(width):
        col_groups = (slice(0, width), slice(TH, TH + width))
        u_s[:, col_groups[0]] = _dot(hx_s[...], wa_ref[:, :width])
        u_s[:, col_groups[1]] = _dot(hx_s[...], wg_ref[:, :width])
        for blk in range(TM // rb):
            r0 = HALO + blk * rb
            rows = lax.broadcasted_iota(jnp.int32, (rb, 1), 0) + (i * TM + blk * rb)
            y = []
            for cols in col_groups:
                ul = u_s[r0 - 1:r0 - 1 + rb, cols]
                ur = u_s[r0 + 1:r0 + 1 + rb, cols]
                if blk in left_blocks:
                    ul = jnp.where(jnp.logical_and(rows != 0, rows != CTX), ul, 0.0)
                if blk in right_blocks:
                    ur = jnp.where(jnp.logical_and(rows != CTX - 1, rows != T - 1), ur, 0.0)
                t = cb_ref[:, cols] + ul * cw_ref[0:1, cols]
                t = t + u_s[r0:r0 + rb, cols] * cw_ref[1:2, cols]
                y.append(t + ur * cw_ref[2:3, cols])
            a, gg = y
            act_s[blk * rb:(blk + 1) * rb, :width] = (gg * jax.nn.sigmoid(gg) * a).astype(BF16)
        o_ref[...] += _dot(act_s[:, :width], wd_ref[:width, :])

    @pl.when(k < NK - 1)
    def _():
        chunk(TH)

    @pl.when(k == NK - 1)
    def _():
        chunk(FFN_H - (NK - 1) * TH)

        def finish(lo, hi, gate):
            o_ref[lo:hi, :] = x_ref[lo:hi, :] + gate * o_ref[lo:hi, :]
        _per_segment(i, finish, gtb[...], gtc[...])


def _ffn(xall, mod5, layer, g, wa, wg, cw_r, cb_r, wd_p):
    hb = TM // HALO
    last = T // HALO - 1
    col_map = lambda b, i, k: (layer, 0, k)
    return pl.pallas_call(
        _ffn_kernel,
        grid=(B, NT, NK),
        in_specs=[
            pl.BlockSpec((None, TM, D), lambda b, i, k: (b, i, 0)),
            pl.BlockSpec((None, HALO, D), lambda b, i, k: (b, jnp.maximum(i * hb - 1, 0), 0)),
            pl.BlockSpec((None, HALO, D), lambda b, i, k: (b, jnp.minimum((i + 1) * hb, last), 0)),
            pl.BlockSpec((None, 1, D), lambda b, i, k: (layer, 0, 0)),
            _mod_spec(layer, 3, False), _mod_spec(layer, 4, False), _mod_spec(layer, 5, False),
            _mod_spec(layer, 3, True), _mod_spec(layer, 4, True), _mod_spec(layer, 5, True),
            pl.BlockSpec((None, D, TH), col_map),
            pl.BlockSpec((None, D, TH), col_map),
            pl.BlockSpec((None, 3, 2 * TH), col_map),
            pl.BlockSpec((None, 1, 2 * TH), col_map),
            pl.BlockSpec((None, TH, D), lambda b, i, k: (layer, k, 0)),
        ],
        out_specs=pl.BlockSpec((None, TM, D), lambda b, i, k: (b, i, 0)),
        out_shape=jax.ShapeDtypeStruct((B, T, D), F32),
        scratch_shapes=[
            pltpu.VMEM((TM + 2 * HALO, D), BF16),
            pltpu.VMEM((TM + 2 * HALO, 2 * TH), F32),
            pltpu.VMEM((TM, TH), BF16),
        ],
        compiler_params=_cparams(("parallel", "parallel", "arbitrary"), VMEM_BIG),
        name="ffn",
    )(xall, xall, xall, g, mod5, mod5, mod5, mod5, mod5, mod5, wa, wg, cw_r, cb_r, wd_p)


def _final_kernel(x_ref, g_ref, o_ref):
    x = x_ref[...]
    o_ref[...] = x * lax.rsqrt(jnp.mean(x * x, axis=-1, keepdims=True) + EPS) * g_ref[...]


def _final_norm(xall, g):
    tm = CTX
    return pl.pallas_call(
        _final_kernel,
        grid=(B, SEQ // tm),
        in_specs=[
            pl.BlockSpec((None, tm, D), lambda b, i: (b, i + 1, 0)),
            pl.BlockSpec((1, D), lambda b, i: (0, 0)),
        ],
        out_specs=pl.BlockSpec((None, tm, D), lambda b, i: (b, i, 0)),
        out_shape=jax.ShapeDtypeStruct((B, SEQ, D), F32),
        compiler_params=_cparams(("parallel", "parallel")),
        name="final_norm",
    )(xall, g.reshape(1, D))


def _rope_tables():
    t = jnp.arange(SEQ)
    pos = jnp.stack([t // GRID_W, t % GRID_W], axis=-1).astype(F32)
    inv = ROPE_BASE ** (-jnp.arange(ROPE_F, dtype=F32) / ROPE_F)
    ang = pos[:, :, None] * inv
    cos = jnp.cos(ang).reshape(SEQ, HEAD // 2)
    sin = jnp.sin(ang).reshape(SEQ, HEAD // 2)
    cos_t = jnp.tile(cos, (1, 2 * LANES // HEAD))
    sin_t = jnp.concatenate([-sin, -sin, sin, sin], axis=-1)
    return cos_t, sin_t


def _rope_lane_order(w):
    q0 = 2 * LRU_W
    qk = w[..., q0:q0 + 2 * NA_W]
    lead = qk.shape[:-1]
    qk = qk.reshape(*lead, 2 * NA_W // LANES, 2, 2, 2, ROPE_F)
    qk = jnp.swapaxes(jnp.swapaxes(qk, -2, -3), -3, -4)
    return w.at[..., q0:q0 + 2 * NA_W].set(qk.reshape(*lead, 2 * NA_W))


def _bias_tables(rpb):
    nl = rpb.shape[0]
    nr, nc = 2 * NA_ROWS - 1, 2 * NA_KC - 1
    pairs = rpb.reshape(nl, NA_HEADS // 2, 2, nr, nc) * LOG2E
    pairs = jnp.pad(pairs, ((0, 0), (0, 0), (0, 0), (0, 2 * NA_ROWS - nr), (0, LANES - nc)))
    return pl.pallas_call(
        _bias_kernel,
        grid=(nl, NA_HEADS // 2),
        in_specs=[pl.BlockSpec((None, None, 2, 2 * NA_ROWS, LANES), lambda l, p: (l, p, 0, 0, 0))],
        out_specs=pl.BlockSpec((None, None, NA_ROWS, 2 * GRID_W, NA_ROWS * GRID_W), lambda l, p: (l, p, 0, 0, 0)),
        out_shape=jax.ShapeDtypeStruct((nl, NA_HEADS // 2, NA_ROWS, 2 * GRID_W, NA_ROWS * GRID_W), F32),
        compiler_params=_cparams(("parallel", "parallel")),
        name="bias_table",
    )(pairs)


def _bias_kernel(rp_ref, o_ref):
    c = lax.broadcasted_iota(jnp.int32, (GRID_W, LANES), 0)
    lane = lax.broadcasted_iota(jnp.int32, (GRID_W, LANES), 1)
    kc = lane % GRID_W
    col_start = jnp.clip(c - NA_KC // 2, 0, GRID_W - NA_KC)
    in_win = jnp.logical_and(kc >= col_start, kc < col_start + NA_KC)
    first = lane < GRID_W
    per_blk = LANES // GRID_W
    for cls in range(NA_ROWS):
        for hd in range(2):
            for blk in range(NA_ROWS // per_blk):
                parts = []
                for h in range(per_blk):
                    m = blk * per_blk + h - cls + NA_ROWS - 1
                    row = jnp.broadcast_to(rp_ref[hd, m:m + 1, :], (GRID_W, LANES))
                    shift = (LANES - (NA_KC - 1) + h * GRID_W) % LANES
                    parts.append(pltpu.roll(row, shift, 1, stride=1, stride_axis=0))
                tile = jnp.where(in_win, jnp.where(first, parts[0], parts[1]), NEG)
                o_ref[cls, hd * GRID_W:(hd + 1) * GRID_W, blk * LANES:(blk + 1) * LANES] = tile


def _gate_blockdiag(gate_w):
    nl = gate_w.shape[0]
    nh = LRU_W // LRU_HALF
    gper = LRU_HALF // HEAD
    w = gate_w.reshape(nl, 4, nh, gper, HEAD, HEAD)
    eye = jnp.eye(gper, dtype=gate_w.dtype)
    bd = jnp.einsum('lkhgio,gj->lkhgijo', w, eye).reshape(nl, 4, nh, LRU_HALF, LRU_HALF)
    return jnp.transpose(bd, (0, 2, 1, 3, 4)).astype(BF16)


def _ffn_layout(w_up, conv_w, conv_b, w_down):
    pad = FFN_HP - FFN_H
    nl = w_up.shape[0]

    def inter(m):
        r = m.shape[1]
        a = jnp.pad(m[..., :FFN_H], ((0, 0), (0, 0), (0, pad))).reshape(nl, r, NK, 1, TH)
        g = jnp.pad(m[..., FFN_H:], ((0, 0), (0, 0), (0, pad))).reshape(nl, r, NK, 1, TH)
        return jnp.concatenate([a, g], axis=3).reshape(nl, r, NK * 2 * TH)

    wa = w_up[..., :FFN_H].astype(BF16)
    wg = w_up[..., FFN_H:].astype(BF16)
    return wa, wg, inter(conv_w), inter(conv_b[:, None, :]), w_down.astype(BF16)


def kernel(x, c, ctx, c_ctx, w_ada, b_ada, norm_mix_g, norm_ffn_g, w_in, lru_conv_w, lru_conv_b, lru_gate_w, lru_gate_b, lru_lambda, na_rpb, sgu_ln_g, sgu_ln_b, sgu_w, sgu_b, w_out, ffn_up, ffn_conv_w, ffn_conv_b, ffn_down, final_norm_g):
    xall = jnp.concatenate([ctx, x], axis=1)
    cc = jnp.concatenate([c, c_ctx[None], jnp.zeros((MOD_ROWS - B - 1, D), F32)], axis=0)
    mod5 = _adaln(cc, w_ada, b_ada).reshape(DEPTH, 6, MOD_ROWS, 1, D)
    cos_t, sin_t = _rope_tables()

    g_mix = norm_mix_g.reshape(DEPTH, 1, D)
    g_ffn = norm_ffn_g.reshape(DEPTH, 1, D)
    w_in_bf = _rope_lane_order(w_in.astype(BF16))
    w_out_bf = w_out.astype(BF16)
    lru_cb = lru_conv_b.reshape(DEPTH, 1, LRU_W)
    lru_gw = _gate_blockdiag(lru_gate_w)
    lru_gb = lru_gate_b.reshape(DEPTH, 4, LRU_W)
    bias = _bias_tables(na_rpb)
    sgu_g = sgu_ln_g.reshape(DEPTH, 1, SGU_W)
    sgu_bb = sgu_ln_b.reshape(DEPTH, 1, SGU_W)
    sgu_w_bf = sgu_w.astype(BF16)
    sgu_bias = jnp.repeat(jnp.swapaxes(sgu_b, 1, 2), HEAD, axis=2)
    ffn_params = _ffn_layout(ffn_up, ffn_conv_w, ffn_conv_b, ffn_down)

    for l in range(DEPTH):
        z = _inproj(xall, mod5, l, g_mix, w_in_bf)
        oa = _lru(z, l, lru_conv_w, lru_cb, lru_gw, lru_gb, lru_lambda)
        ob = _attn(z, l, cos_t, sin_t, bias)
        oc = _sgu(z, l, sgu_g, sgu_bb, sgu_w_bf, sgu_bias)
        xall = _outproj(xall, oa, ob, oc, mod5, l, w_out_bf)
        xall = _ffn(xall, mod5, l, g_ffn, *ffn_params)
    return _final_norm(xall, final_norm_g)
```

```python
import functools

import jax
import jax.numpy as jnp
from jax import lax
from jax.experimental import pallas as pl
from jax.experimental.pallas import tpu as pltpu

F32 = jnp.float32
BF16 = jnp.bfloat16

D = 2048
B = 8
SEQ = 2048
DEPTH = 4
GRID_W = 64
GRID_H = SEQ // GRID_W
CTX = 256
T = CTX + SEQ
HEAD = 64
LRU_W = 512
LRU_TAPS = 4
LRU_C = 8.0
NA_W = 1024
NA_HEADS = NA_W // HEAD
NA_ROWS = 8
NA_KC = 16
SGU_W = 512
SGU_G = SGU_W // HEAD
SGU_CHUNK = 128
PROJ = 2 * LRU_W + 3 * NA_W + 2 * SGU_W
FFN_H = 5504
EPS = 1e-6
NEG = -1e30
LOG2E = 1.4426950408889634
ROPE_BASE = 10000.0
ROPE_F = HEAD // 4
MOD_ROWS = 16
CTX_ROW = B

LANES = 128
SUBLANES = 8
VMEM_BIG = 56 * 1024 * 1024
VMEM_MID = 40 * 1024 * 1024

TM = 768
NT = T // TM
HALO = 16
TH = 512
FFN_HP = 5632
NK = FFN_HP // TH
LRU_HALF = 256
LRU_RB = 128
ATT_RB = 256
NORM_RB = 16
NORM_UNROLL = 4
FFN_RB = 128
ATT_UNROLL = 4
ATT_GROUP = 4


def _cparams(sem, vmem=VMEM_MID):
    return pltpu.CompilerParams(dimension_semantics=sem, vmem_limit_bytes=vmem)


def _dot(a, b):
    return jnp.dot(a, b, preferred_element_type=F32)


def _dot_nt(a, b):
    return lax.dot_general(a, b, (((1,), (1,)), ((), ())), preferred_element_type=F32)


def _mod_spec(layer, which, ctx, width=D, col=None):
    def imap(*ids):
        b = ids[0]
        row = CTX_ROW if ctx else b
        c = 0 if col is None else ids[col]
        return (layer, which, row, 0, c)
    return pl.BlockSpec((None, None, None, 1, width), imap)


def _per_segment(i, fn, latent_arg, ctx_arg):
    @pl.when(i == 0)
    def _():
        fn(0, CTX, ctx_arg)
        fn(CTX, TM, latent_arg)

    @pl.when(i != 0)
    def _():
        fn(0, TM, latent_arg)


def _norm_rows(x_ref, dst_ref, dst_off, lo, hi, gain, shift):
    def body(j, _):
        r = pl.multiple_of(lo + j * NORM_RB, NORM_RB)
        x = x_ref[pl.ds(r, NORM_RB), :]
        inv = lax.rsqrt(jnp.mean(x * x, axis=-1, keepdims=True) + EPS)
        dst_ref[pl.ds(dst_off + r, NORM_RB), :] = ((x * inv) * gain + shift).astype(BF16)
        return 0
    n = (hi - lo) // NORM_RB
    lax.fori_loop(0, n, body, 0, unroll=min(n, NORM_UNROLL))


def _norm_tile(i, x_ref, dst_ref, dst_off, g_ref, shb, scb, shc, scc):
    g = g_ref[...]

    def run(lo, hi, mod):
        _norm_rows(x_ref, dst_ref, dst_off, lo, hi, g * (1.0 + mod[0]), mod[1])
    _per_segment(i, run, (scb[...], shb[...]), (scc[...], shc[...]))


def _adaln_kernel(c_ref, w_ref, b_ref, o_ref):
    c = c_ref[...]
    s = (c * jax.nn.sigmoid(c)).astype(BF16)
    o_ref[...] = _dot(s, w_ref[...].astype(BF16)) + b_ref[...]


def _adaln(cc, w_ada, b_ada):
    tn = 1024
    nj = D // tn
    return pl.pallas_call(
        _adaln_kernel,
        grid=(DEPTH, 6 * nj),
        in_specs=[
            pl.BlockSpec((MOD_ROWS, D), lambda l, j: (0, 0)),
            pl.BlockSpec((None, D, tn), lambda l, j: (l, 0, j)),
            pl.BlockSpec((None, 1, tn), lambda l, j: (l, 0, j)),
        ],
        out_specs=pl.BlockSpec((None, None, MOD_ROWS, tn), lambda l, j: (l, j // nj, 0, j % nj)),
        out_shape=jax.ShapeDtypeStruct((DEPTH, 6, MOD_ROWS, D), F32),
        compiler_params=_cparams(("parallel", "parallel")),
        name="adaln",
    )(cc, w_ada, b_ada.reshape(DEPTH, 1, 6 * D))


def _inproj_kernel(x_ref, g_ref, shb, scb, shc, scc, w_ref, o_ref, hx_s):
    i = pl.program_id(1)

    @pl.when(pl.program_id(2) == 0)
    def _():
        _norm_tile(i, x_ref, hx_s, 0, g_ref, shb, scb, shc, scc)

    o_ref[...] = _dot(hx_s[...], w_ref[...])


def _inproj(xall, mod5, layer, g_all, w_all):
    tn = 1024
    return pl.pallas_call(
        _inproj_kernel,
        grid=(B, NT, PROJ // tn),
        in_specs=[
            pl.BlockSpec((None, TM, D), lambda b, i, j: (b, i, 0)),
            pl.BlockSpec((None, 1, D), lambda b, i, j: (layer, 0, 0)),
            _mod_spec(layer, 0, False), _mod_spec(layer, 1, False),
            _mod_spec(layer, 0, True), _mod_spec(layer, 1, True),
            pl.BlockSpec((None, D, tn), lambda b, i, j: (layer, 0, j)),
        ],
        out_specs=pl.BlockSpec((None, TM, tn), lambda b, i, j: (b, i, j)),
        out_shape=jax.ShapeDtypeStruct((B, T, PROJ), F32),
        scratch_shapes=[pltpu.VMEM((TM, D), BF16)],
        compiler_params=_cparams(("parallel", "parallel", "arbitrary")),
        name="inproj",
    )(xall, g_all, mod5, mod5, mod5, mod5, w_all)


def _tile_scan(a, b, carry, rev):
    rid = lax.broadcasted_iota(jnp.int32, a.shape, 0)
    for d in (1, 2, 4):
        sh = SUBLANES - d if rev else d
        keep = (rid < SUBLANES - d) if rev else (rid >= d)
        a_s = jnp.where(keep, pltpu.roll(a, sh, 0), 1.0)
        b_s = jnp.where(keep, pltpu.roll(b, sh, 0), 0.0)
        b = a * b_s + b
        a = a * a_s
    h = a * carry + b
    return h, (h[0:1] if rev else h[SUBLANES - 1:SUBLANES])


def _lru_kernel(ax_ref, ay_ref, cw_ref, cb_ref, gw_ref, gb_ref, lam_ref, o_ref, af, bf, ab, bb):
    W = LRU_HALF
    cw = cw_ref[...]
    cb = cb_ref[...]
    gb = gb_ref[...]
    nl = -lam_ref[...]
    softplus = jnp.maximum(nl, 0.0) + jnp.log(1.0 + jnp.exp(-jnp.abs(nl)))
    coef = -LRU_C * softplus
    zeros8 = jnp.zeros((SUBLANES, W), F32)
    n_ext = LRU_RB + 2 * SUBLANES

    for blk in range(T // LRU_RB):
        r0 = blk * LRU_RB
        r1 = r0 + LRU_RB
        prev = zeros8 if r0 in (0, CTX) else ax_ref[r0 - SUBLANES:r0, :]
        nxt = zeros8 if r1 in (CTX, T) else ax_ref[r1:r1 + SUBLANES, :]
        ext = jnp.concatenate([prev, ax_ref[r0:r1, :], nxt], axis=0)
        lo, hi = SUBLANES, SUBLANES + LRU_RB
        xc = cb + pltpu.roll(ext, 2, 0)[lo:hi] * cw[0:1]
        xc = xc + pltpu.roll(ext, 1, 0)[lo:hi] * cw[1:2]
        xc = xc + ext[lo:hi] * cw[2:3]
        xc = xc + pltpu.roll(ext, n_ext - 1, 0)[lo:hi] * cw[3:4]
        xb = xc.astype(BF16)
        for d, (a_s, b_s) in enumerate(((af, bf), (ab, bb))):
            r = jax.nn.sigmoid(_dot(xb, gw_ref[2 * d]) + gb[2 * d:2 * d + 1])
            gi = jax.nn.sigmoid(_dot(xb, gw_ref[2 * d + 1]) + gb[2 * d + 1:2 * d + 2])
            log_a = coef[d:d + 1] * r
            a_s[r0:r1, :] = jnp.exp(log_a)
            th = jnp.tanh(log_a)
            b_s[r0:r1, :] = jnp.sqrt(-2.0 * th / (1.0 - th)) * (gi * xc)

    def make_body(f_base, b_top):
        def body(k, carry):
            cf, cr = carry
            rf = pl.multiple_of((f_base + k) * SUBLANES, SUBLANES)
            hf, cf = _tile_scan(af[pl.ds(rf, SUBLANES), :], bf[pl.ds(rf, SUBLANES), :], cf, False)
            bf[pl.ds(rf, SUBLANES), :] = hf
            rb = pl.multiple_of((b_top - k) * SUBLANES, SUBLANES)
            hb, cr = _tile_scan(ab[pl.ds(rb, SUBLANES), :], bb[pl.ds(rb, SUBLANES), :], cr, True)
            bb[pl.ds(rb, SUBLANES), :] = hb
            return cf, cr
        return body

    z1 = jnp.zeros((1, W), F32)
    n_ctx = CTX // SUBLANES
    n_all = T // SUBLANES
    carry = lax.fori_loop(0, n_ctx, make_body(0, n_ctx - 1), (z1, z1))
    lax.fori_loop(0, n_all - n_ctx, make_body(n_ctx, n_all - 1), carry)

    for blk in range(T // LRU_RB):
        r0 = blk * LRU_RB
        r1 = r0 + LRU_RB
        h = bf[r0:r1, :] + bb[r0:r1, :]
        o_ref[r0:r1, :] = (h * jax.nn.gelu(ay_ref[r0:r1, :])).astype(BF16)


def _lru(z, layer, cw, cb, gw_bd, gb, lam):
    W = LRU_HALF
    nh = LRU_W // W
    return pl.pallas_call(
        _lru_kernel,
        grid=(B, nh),
        in_specs=[
            pl.BlockSpec((None, T, W), lambda b, c: (b, 0, c)),
            pl.BlockSpec((None, T, W), lambda b, c: (b, 0, nh + c)),
            pl.BlockSpec((None, LRU_TAPS, W), lambda b, c: (layer, 0, c)),
            pl.BlockSpec((None, 1, W), lambda b, c: (layer, 0, c)),
            pl.BlockSpec((None, None, 4, W, W), lambda b, c: (layer, c, 0, 0, 0)),
            pl.BlockSpec((None, 4, W), lambda b, c: (layer, 0, c)),
            pl.BlockSpec((None, 2, W), lambda b, c: (layer, 0, c)),
        ],
        out_specs=pl.BlockSpec((None, T, W), lambda b, c: (b, 0, c)),
        out_shape=jax.ShapeDtypeStruct((B, T, LRU_W), BF16),
        scratch_shapes=[pltpu.VMEM((T, W), F32)] * 4,
        compiler_params=_cparams(("parallel", "parallel")),
        name="lru",
    )(z, z, cw, cb, gw_bd, gb, lam)


def _attn_kernel(q_ref, k_ref, v_ref, cos_ref, sin_ref, bias_ref, o_ref,
                 qraw_s, qrot_s, qctx_s, krot_s, kc_s, v_s, sc_s, pc_s, ow_s):
    lane = lax.broadcasted_iota(jnp.int32, (1, LANES), 1)
    head0 = lane < HEAD
    qk_head0 = (lane % HEAD) < HEAD // 2
    scale = HEAD ** -0.5 * LOG2E
    W2 = 2 * GRID_W

    def rope(x, c, s):
        return x * c + pltpu.roll(x, LANES // 2, 1) * s

    def stack_heads(x):
        return jnp.concatenate([jnp.where(qk_head0, x, 0.0), jnp.where(qk_head0, 0.0, x)], axis=0).astype(BF16)

    def unstack_heads(y):
        n = y.shape[0] // 2
        return jnp.where(head0, y[:n], y[n:])

    qctx_s[...] = stack_heads(q_ref[0:CTX, :] * scale)
    kc_s[...] = k_ref[0:CTX, :].astype(BF16)
    v_s[0:CTX, :] = v_ref[0:CTX, :].astype(BF16)
    rows_per_blk = ATT_RB // GRID_W
    for blk in range(SEQ // ATT_RB):
        r0 = blk * ATT_RB
        r1 = r0 + ATT_RB
        c = cos_ref[r0:r1, :]
        s = sin_ref[r0:r1, :]
        q = q_ref[CTX + r0:CTX + r1, :]
        qs = q * scale
        qr = rope(q, c, s) * scale
        for j in range(rows_per_blk):
            qraw_s[blk * rows_per_blk + j] = stack_heads(qs[j * GRID_W:(j + 1) * GRID_W])
            qrot_s[blk * rows_per_blk + j] = stack_heads(qr[j * GRID_W:(j + 1) * GRID_W])
        krot_s[r0:r1, :] = rope(k_ref[CTX + r0:CTX + r1, :], c, s).astype(BF16)
        v_s[CTX + r0:CTX + r1, :] = v_ref[CTX + r0:CTX + r1, :].astype(BF16)

    s = _dot_nt(qctx_s[...], kc_s[...])
    e = jnp.exp2(s - jnp.max(s, axis=-1, keepdims=True))
    p = (e / jnp.sum(e, axis=-1, keepdims=True)).astype(BF16)
    o_ref[0:CTX, :] = unstack_heads(_dot(p, v_s[0:CTX, :])).astype(BF16)

    for g in range(GRID_H // ATT_GROUP):
        g0, g1 = g * ATT_GROUP, (g + 1) * ATT_GROUP
        sc = _dot_nt(qraw_s[g0:g1].reshape(ATT_GROUP * W2, LANES), kc_s[...])
        sc_s[g0:g1] = sc.reshape(ATT_GROUP, W2, CTX)

    win = NA_ROWS * GRID_W

    def lane_fold(op, *xs):
        cols = [x[:, c:c + LANES] for x in xs for c in range(0, x.shape[1], LANES)]
        acc = cols[0]
        for col in cols[1:]:
            acc = op(acc, col)
        return acc

    def body(t, _):
        rows = [t * ATT_UNROLL + j for j in range(ATT_UNROLL)]
        starts = [jnp.clip(r - NA_ROWS // 2, 0, GRID_H - NA_ROWS) for r in rows]
        sw = [_dot_nt(qrot_s[r], krot_s[pl.ds(pl.multiple_of(rs * GRID_W, GRID_W), win), :]) + bias_ref[r - rs]
              for r, rs in zip(rows, starts)]
        sc = [sc_s[r] for r in rows]
        m = [jnp.max(lane_fold(jnp.maximum, a, c), axis=-1, keepdims=True) for a, c in zip(sw, sc)]
        ew = [jnp.exp2(a - mm) for a, mm in zip(sw, m)]
        ec = [jnp.exp2(c - mm) for c, mm in zip(sc, m)]
        inv = [1.0 / jnp.sum(lane_fold(jnp.add, a, c), axis=-1, keepdims=True) for a, c in zip(ew, ec)]
        for r, rs, a, c, iv in zip(rows, starts, ew, ec, inv):
            kx = pl.multiple_of(CTX + rs * GRID_W, GRID_W)
            ow_s[r] = _dot(a.astype(BF16), v_s[pl.ds(kx, win), :]) * iv
            pc_s[r] = (c * iv).astype(BF16)
        return 0

    lax.fori_loop(0, GRID_H // ATT_UNROLL, body, 0)

    for g in range(GRID_H // ATT_GROUP):
        g0, g1 = g * ATT_GROUP, (g + 1) * ATT_GROUP
        oc = _dot(pc_s[g0:g1].reshape(ATT_GROUP * W2, CTX), v_s[0:CTX, :]).reshape(ATT_GROUP, W2, LANES)
        tot = oc + ow_s[g0:g1]
        for j in range(ATT_GROUP):
            row = CTX + (g0 + j) * GRID_W
            o_ref[row:row + GRID_W, :] = unstack_heads(tot[j]).astype(BF16)


def _attn(z, layer, cos_t, sin_t, biasmask):
    nhp = NA_W // LANES
    qb = 2 * LRU_W // LANES
    return pl.pallas_call(
        _attn_kernel,
        grid=(nhp, B),
        in_specs=[
            pl.BlockSpec((None, T, LANES), lambda p, b: (b, 0, qb + p)),
            pl.BlockSpec((None, T, LANES), lambda p, b: (b, 0, qb + nhp + p)),
            pl.BlockSpec((None, T, LANES), lambda p, b: (b, 0, qb + 2 * nhp + p)),
            pl.BlockSpec((SEQ, LANES), lambda p, b: (0, 0)),
            pl.BlockSpec((SEQ, LANES), lambda p, b: (0, 0)),
            pl.BlockSpec((None, None, NA_ROWS, 2 * GRID_W, NA_ROWS * GRID_W), lambda p, b: (layer, p, 0, 0, 0)),
        ],
        out_specs=pl.BlockSpec((None, T, LANES), lambda p, b: (b, 0, p)),
        out_shape=jax.ShapeDtypeStruct((B, T, NA_W), BF16),
        scratch_shapes=[
            pltpu.VMEM((GRID_H, 2 * GRID_W, LANES), BF16),
            pltpu.VMEM((GRID_H, 2 * GRID_W, LANES), BF16),
            pltpu.VMEM((2 * CTX, LANES), BF16),
            pltpu.VMEM((SEQ, LANES), BF16),
            pltpu.VMEM((CTX, LANES), BF16),
            pltpu.VMEM((T, LANES), BF16),
            pltpu.VMEM((GRID_H, 2 * GRID_W, CTX), F32),
            pltpu.VMEM((GRID_H, 2 * GRID_W, CTX), BF16),
            pltpu.VMEM((GRID_H, 2 * GRID_W, LANES), F32),
        ],
        compiler_params=_cparams(("parallel", "parallel")),
        name="attn",
    )(z, z, z, cos_t, sin_t, biasmask)


def _sgu_kernel(u_ref, v_ref, g_ref, b_ref, ws_ref, bs_ref, o_ref):
    lane = lax.broadcasted_iota(jnp.int32, (1, LANES), 1)
    first = lane < HEAD
    for n in range(TM // SGU_CHUNK):
        r0 = n * SGU_CHUNK
        r1 = r0 + SGU_CHUNK
        v = jax.nn.gelu(v_ref[r0:r1, :])
        mu = jnp.mean(v, axis=-1, keepdims=True)
        var = jnp.mean(jnp.square(v - mu), axis=-1, keepdims=True)
        vn = ((v - mu) * lax.rsqrt(var + EPS) * g_ref[...] + b_ref[...]).astype(BF16)
        for j in range(SGU_W // LANES):
            c0 = j * LANES
            c1 = c0 + LANES
            vp = vn[:, c0:c1]
            mixed = jnp.where(first, _dot(ws_ref[2 * j], vp), _dot(ws_ref[2 * j + 1], vp)) + bs_ref[:, c0:c1]
            o_ref[r0:r1, c0:c1] = (jax.nn.gelu(u_ref[r0:r1, c0:c1]) * mixed).astype(BF16)


def _sgu(z, layer, ln_g, ln_b, ws_bf, bs_full):
    ub = (2 * LRU_W + 3 * NA_W) // SGU_W
    return pl.pallas_call(
        _sgu_kernel,
        grid=(B, NT),
        in_specs=[
            pl.BlockSpec((None, TM, SGU_W), lambda b, i: (b, i, ub)),
            pl.BlockSpec((None, TM, SGU_W), lambda b, i: (b, i, ub + 1)),
            pl.BlockSpec((None, 1, SGU_W), lambda b, i: (layer, 0, 0)),
            pl.BlockSpec((None, 1, SGU_W), lambda b, i: (layer, 0, 0)),
            pl.BlockSpec((None, SGU_G, SGU_CHUNK, SGU_CHUNK), lambda b, i: (layer, 0, 0, 0)),
            pl.BlockSpec((None, SGU_CHUNK, SGU_W), lambda b, i: (layer, 0, 0)),
        ],
        out_specs=pl.BlockSpec((None, TM, SGU_W), lambda b, i: (b, i, 0)),
        out_shape=jax.ShapeDtypeStruct((B, T, SGU_W), BF16),
        compiler_params=_cparams(("parallel", "parallel")),
        name="sgu",
    )(z, z, ln_g, ln_b, ws_bf, bs_full)


def _outproj_kernel(x_ref, a_ref, b_ref, c_ref, gtb, gtc, w_ref, o_ref):
    i = pl.program_id(1)
    acc = _dot(a_ref[...], w_ref[0:LRU_W, :])
    acc = acc + _dot(b_ref[...], w_ref[LRU_W:LRU_W + NA_W, :])
    acc = acc + _dot(c_ref[...], w_ref[LRU_W + NA_W:D, :])
    is_ctx = (lax.broadcasted_iota(jnp.int32, (TM, 1), 0) + i * TM) < CTX
    o_ref[...] = x_ref[...] + jnp.where(is_ctx, gtc[...], gtb[...]) * acc


def _outproj(xall, oa, ob, oc, mod5, layer, w_all):
    tn = D
    return pl.pallas_call(
        _outproj_kernel,
        grid=(B, NT, D // tn),
        in_specs=[
            pl.BlockSpec((None, TM, tn), lambda b, i, j: (b, i, j)),
            pl.BlockSpec((None, TM, LRU_W), lambda b, i, j: (b, i, 0)),
            pl.BlockSpec((None, TM, NA_W), lambda b, i, j: (b, i, 0)),
            pl.BlockSpec((None, TM, SGU_W), lambda b, i, j: (b, i, 0)),
            _mod_spec(layer, 2, False, tn, 2), _mod_spec(layer, 2, True, tn, 2),
            pl.BlockSpec((None, D, tn), lambda b, i, j: (layer, 0, j), pipeline_mode=pl.Buffered(1)),
        ],
        out_specs=pl.BlockSpec((None, TM, tn), lambda b, i, j: (b, i, j)),
        out_shape=jax.ShapeDtypeStruct((B, T, D), F32),
        compiler_params=_cparams(("parallel", "parallel", "parallel"), VMEM_BIG),
        name="outproj",
    )(xall, oa, ob, oc, mod5, mod5, w_all)


def _ffn_kernel(x_ref, xp_ref, xn_ref, g_ref, shb, scb, gtb, shc, scc, gtc, wa_ref, wg_ref, cw_ref, cb_ref, wd_ref,
                o_ref, hx_s, u_s, act_s):
    i = pl.program_id(1)
    k = pl.program_id(2)
    rb = FFN_RB
    left_blocks = (0, CTX // rb)
    right_blocks = (CTX // rb - 1, TM // rb - 1)

    @pl.when(k == 0)
    def _():
        gain_b = g_ref[...] * (1.0 + scb[...])
        _norm_rows(xp_ref, hx_s, 0, 0, HALO, gain_b, shb[...])
        _norm_rows(xn_ref, hx_s, HALO + TM, 0, HALO, gain_b, shb[...])
        _norm_tile(i, x_ref, hx_s, HALO, g_ref, shb, scb, shc, scc)
        o_ref[...] = jnp.zeros_like(o_ref)

    u_s[:, :TH] = _dot(hx_s[...], wa_ref[...])
    u_s[:, TH:] = _dot(hx_s[...], wg_ref[...])
    cw = cw_ref[...]
    cb = cb_ref[...]
    for blk in range(TM // rb):
        r0 = HALO + blk * rb
        rows = lax.broadcasted_iota(jnp.int32, (rb, 1), 0) + (i * TM + blk * rb)
        ul = u_s[r0 - 1:r0 - 1 + rb, :]
        ur = u_s[r0 + 1:r0 + 1 + rb, :]
        if blk in left_blocks:
            ul = jnp.where(jnp.logical_and(rows != 0, rows != CTX), ul, 0.0)
        if blk in right_blocks:
            ur = jnp.where(jnp.logical_and(rows != CTX - 1, rows != T - 1), ur, 0.0)
        y = cb + ul * cw[0:1]
        y = y + u_s[r0:r0 + rb, :] * cw[1:2]
        y = y + ur * cw[2:3]
        a = y[:, :TH]
        gg = y[:, TH:]
        act_s[blk * rb:(blk + 1) * rb, :] = (gg * jax.nn.sigmoid(gg) * a).astype(BF16)
    o_ref[...] += _dot(act_s[...], wd_ref[...])

    @pl.when(k == NK - 1)
    def _():
        def finish(lo, hi, gate):
            o_ref[lo:hi, :] = x_ref[lo:hi, :] + gate * o_ref[lo:hi, :]
        _per_segment(i, finish, gtb[...], gtc[...])


def _ffn(xall, mod5, layer, g, wa, wg, cw_r, cb_r, wd_p):
    hb = TM // HALO
    last = T // HALO - 1
    col_map = lambda b, i, k: (layer, 0, k)
    return pl.pallas_call(
        _ffn_kernel,
        grid=(B, NT, NK),
        in_specs=[
            pl.BlockSpec((None, TM, D), lambda b, i, k: (b, i, 0)),
            pl.BlockSpec((None, HALO, D), lambda b, i, k: (b, jnp.maximum(i * hb - 1, 0), 0)),
            pl.BlockSpec((None, HALO, D), lambda b, i, k: (b, jnp.minimum((i + 1) * hb, last), 0)),
            pl.BlockSpec((None, 1, D), lambda b, i, k: (layer, 0, 0)),
            _mod_spec(layer, 3, False), _mod_spec(layer, 4, False), _mod_spec(layer, 5, False),
            _mod_spec(layer, 3, True), _mod_spec(layer, 4, True), _mod_spec(layer, 5, True),
            pl.BlockSpec((None, D, TH), col_map),
            pl.BlockSpec((None, D, TH), col_map),
            pl.BlockSpec((None, 3, 2 * TH), col_map),
            pl.BlockSpec((None, 1, 2 * TH), col_map),
            pl.BlockSpec((None, TH, D), lambda b, i, k: (layer, k, 0)),
        ],
        out_specs=pl.BlockSpec((None, TM, D), lambda b, i, k: (b, i, 0)),
        out_shape=jax.ShapeDtypeStruct((B, T, D), F32),
        scratch_shapes=[
            pltpu.VMEM((TM + 2 * HALO, D), BF16),
            pltpu.VMEM((TM + 2 * HALO, 2 * TH), F32),
            pltpu.VMEM((TM, TH), BF16),
        ],
        compiler_params=_cparams(("parallel", "parallel", "arbitrary"), VMEM_BIG),
        name="ffn",
    )(xall, xall, xall, g, mod5, mod5, mod5, mod5, mod5, mod5, wa, wg, cw_r, cb_r, wd_p)


def _final_kernel(x_ref, g_ref, o_ref):
    x = x_ref[...]
    o_ref[...] = x * lax.rsqrt(jnp.mean(x * x, axis=-1, keepdims=True) + EPS) * g_ref[...]


def _final_norm(xall, g):
    tm = CTX
    return pl.pallas_call(
        _final_kernel,
        grid=(B, SEQ // tm),
        in_specs=[
            pl.BlockSpec((None, tm, D), lambda b, i: (b, i + 1, 0)),
            pl.BlockSpec((1, D), lambda b, i: (0, 0)),
        ],
        out_specs=pl.BlockSpec((None, tm, D), lambda b, i: (b, i, 0)),
        out_shape=jax.ShapeDtypeStruct((B, SEQ, D), F32),
        compiler_params=_cparams(("parallel", "parallel")),
        name="final_norm",
    )(xall, g.reshape(1, D))


def _rope_tables():
    t = jnp.arange(SEQ)
    pos = jnp.stack([t // GRID_W, t % GRID_W], axis=-1).astype(F32)
    inv = ROPE_BASE ** (-jnp.arange(ROPE_F, dtype=F32) / ROPE_F)
    ang = pos[:, :, None] * inv
    cos = jnp.cos(ang).reshape(SEQ, HEAD // 2)
    sin = jnp.sin(ang).reshape(SEQ, HEAD // 2)
    cos_t = jnp.tile(cos, (1, 2 * LANES // HEAD))
    sin_t = jnp.concatenate([-sin, -sin, sin, sin], axis=-1)
    return cos_t, sin_t


def _rope_lane_order(w):
    q0 = 2 * LRU_W
    qk = w[..., q0:q0 + 2 * NA_W]
    lead = qk.shape[:-1]
    qk = qk.reshape(*lead, 2 * NA_W // LANES, 2, 2, 2, ROPE_F)
    qk = jnp.swapaxes(jnp.swapaxes(qk, -2, -3), -3, -4)
    return w.at[..., q0:q0 + 2 * NA_W].set(qk.reshape(*lead, 2 * NA_W))


def _bias_tables(rpb):
    nl = rpb.shape[0]
    nr, nc = 2 * NA_ROWS - 1, 2 * NA_KC - 1
    pairs = rpb.reshape(nl, NA_HEADS // 2, 2, nr, nc) * LOG2E
    pairs = jnp.pad(pairs, ((0, 0), (0, 0), (0, 0), (0, 2 * NA_ROWS - nr), (0, LANES - nc)))
    return pl.pallas_call(
        _bias_kernel,
        grid=(nl, NA_HEADS // 2),
        in_specs=[pl.BlockSpec((None, None, 2, 2 * NA_ROWS, LANES), lambda l, p: (l, p, 0, 0, 0))],
        out_specs=pl.BlockSpec((None, None, NA_ROWS, 2 * GRID_W, NA_ROWS * GRID_W), lambda l, p: (l, p, 0, 0, 0)),
        out_shape=jax.ShapeDtypeStruct((nl, NA_HEADS // 2, NA_ROWS, 2 * GRID_W, NA_ROWS * GRID_W), F32),
        compiler_params=_cparams(("parallel", "parallel")),
        name="bias_table",
    )(pairs)


def _bias_kernel(rp_ref, o_ref):
    c = lax.broadcasted_iota(jnp.int32, (GRID_W, LANES), 0)
    lane = lax.broadcasted_iota(jnp.int32, (GRID_W, LANES), 1)
    kc = lane % GRID_W
    col_start = jnp.clip(c - NA_KC // 2, 0, GRID_W - NA_KC)
    in_win = jnp.logical_and(kc >= col_start, kc < col_start + NA_KC)
    first = lane < GRID_W
    per_blk = LANES // GRID_W
    for cls in range(NA_ROWS):
        for hd in range(2):
            for blk in range(NA_ROWS // per_blk):
                parts = []
                for h in range(per_blk):
                    m = blk * per_blk + h - cls + NA_ROWS - 1
                    row = jnp.broadcast_to(rp_ref[hd, m:m + 1, :], (GRID_W, LANES))
                    shift = (LANES - (NA_KC - 1) + h * GRID_W) % LANES
                    parts.append(pltpu.roll(row, shift, 1, stride=1, stride_axis=0))
                tile = jnp.where(in_win, jnp.where(first, parts[0], parts[1]), NEG)
                o_ref[cls, hd * GRID_W:(hd + 1) * GRID_W, blk * LANES:(blk + 1) * LANES] = tile


def _gate_blockdiag(gate_w):
    nl = gate_w.shape[0]
    nh = LRU_W // LRU_HALF
    gper = LRU_HALF // HEAD
    w = gate_w.reshape(nl, 4, nh, gper, HEAD, HEAD)
    eye = jnp.eye(gper, dtype=gate_w.dtype)
    bd = jnp.einsum('lkhgio,gj->lkhgijo', w, eye).reshape(nl, 4, nh, LRU_HALF, LRU_HALF)
    return jnp.transpose(bd, (0, 2, 1, 3, 4)).astype(BF16)


def _ffn_layout(w_up, conv_w, conv_b, w_down):
    pad = FFN_HP - FFN_H
    nl = w_up.shape[0]

    def inter(m):
        r = m.shape[1]
        a = jnp.pad(m[..., :FFN_H], ((0, 0), (0, 0), (0, pad))).reshape(nl, r, NK, 1, TH)
        g = jnp.pad(m[..., FFN_H:], ((0, 0), (0, 0), (0, pad))).reshape(nl, r, NK, 1, TH)
        return jnp.concatenate([a, g], axis=3).reshape(nl, r, NK * 2 * TH)

    zc = jnp.zeros((nl, D, pad), BF16)
    wa = jnp.concatenate([w_up[..., :FFN_H].astype(BF16), zc], axis=2)
    wg = jnp.concatenate([w_up[..., FFN_H:].astype(BF16), zc], axis=2)
    wd = jnp.concatenate([w_down.astype(BF16), jnp.zeros((nl, pad, D), BF16)], axis=1)
    return wa, wg, inter(conv_w), inter(conv_b[:, None, :]), wd


def kernel(x, c, ctx, c_ctx, w_ada, b_ada, norm_mix_g, norm_ffn_g, w_in, lru_conv_w, lru_conv_b, lru_gate_w, lru_gate_b, lru_lambda, na_rpb, sgu_ln_g, sgu_ln_b, sgu_w, sgu_b, w_out, ffn_up, ffn_conv_w, ffn_conv_b, ffn_down, final_norm_g):
    xall = jnp.concatenate([ctx, x], axis=1)
    cc = jnp.concatenate([c, c_ctx[None], jnp.zeros((MOD_ROWS - B - 1, D), F32)], axis=0)
    mod5 = _adaln(cc, w_ada, b_ada).reshape(DEPTH, 6, MOD_ROWS, 1, D)
    cos_t, sin_t = _rope_tables()

    g_mix = norm_mix_g.reshape(DEPTH, 1, D)
    g_ffn = norm_ffn_g.reshape(DEPTH, 1, D)
    w_in_bf = _rope_lane_order(w_in.astype(BF16))
    w_out_bf = w_out.astype(BF16)
    lru_cb = lru_conv_b.reshape(DEPTH, 1, LRU_W)
    lru_gw = _gate_blockdiag(lru_gate_w)
    lru_gb = lru_gate_b.reshape(DEPTH, 4, LRU_W)
    bias = _bias_tables(na_rpb)
    sgu_g = sgu_ln_g.reshape(DEPTH, 1, SGU_W)
    sgu_bb = sgu_ln_b.reshape(DEPTH, 1, SGU_W)
    sgu_w_bf = sgu_w.astype(BF16)
    sgu_bias = jnp.repeat(jnp.swapaxes(sgu_b, 1, 2), HEAD, axis=2)
    ffn_params = _ffn_layout(ffn_up, ffn_conv_w, ffn_conv_b, ffn_down)

    for l in range(DEPTH):
        z = _inproj(xall, mod5, l, g_mix, w_in_bf)
        oa = _lru(z, l, lru_conv_w, lru_cb, lru_gw, lru_gb, lru_lambda)
        ob = _attn(z, l, cos_t, sin_t, bias)
        oc = _sgu(z, l, sgu_g, sgu_bb, sgu_w_bf, sgu_bias)
        xall = _outproj(xall, oa, ob, oc, mod5, l, w_out_bf)
        xall = _ffn(xall, mod5, l, g_ffn, *ffn_params)
    return _final_norm(xall, final_norm_g)
```

```python
import functools

import jax
import jax.numpy as jnp
from jax import lax
from jax.experimental import pallas as pl
from jax.experimental.pallas import tpu as pltpu

F32 = jnp.float32
BF16 = jnp.bfloat16

D = 2048
B = 8
SEQ = 2048
DEPTH = 4
GRID_W = 64
GRID_H = SEQ // GRID_W
CTX = 256
T = CTX + SEQ
HEAD = 64
LRU_W = 512
LRU_TAPS = 4
LRU_C = 8.0
NA_W = 1024
NA_HEADS = NA_W // HEAD
NA_ROWS = 8
NA_KC = 16
SGU_W = 512
SGU_G = SGU_W // HEAD
SGU_CHUNK = 128
PROJ = 2 * LRU_W + 3 * NA_W + 2 * SGU_W
FFN_H = 5504
EPS = 1e-6
NEG = -1e30
LOG2E = 1.4426950408889634
ROPE_BASE = 10000.0
ROPE_F = HEAD // 4
MOD_ROWS = 16
CTX_ROW = B

LANES = 128
SUBLANES = 8
VMEM_BIG = 56 * 1024 * 1024
VMEM_MID = 40 * 1024 * 1024

TM = 768
NT = T // TM
HALO = 16
TH = 512
FFN_HP = 5632
NK = FFN_HP // TH
LRU_HALF = 256
LRU_RB = 128
ATT_RB = 256
NORM_RB = 16
NORM_UNROLL = 4
FFN_RB = 128
FFN_SPLIT = 2
ATT_UNROLL = 4
ATT_GROUP = 4


def _cparams(sem, vmem=VMEM_MID):
    return pltpu.CompilerParams(dimension_semantics=sem, vmem_limit_bytes=vmem)


def _dot(a, b):
    return jnp.dot(a, b, preferred_element_type=F32)


def _dot_nt(a, b):
    return lax.dot_general(a, b, (((1,), (1,)), ((), ())), preferred_element_type=F32)


def _mod_spec(layer, which, ctx, width=D, col=None):
    def imap(*ids):
        b = ids[0]
        row = CTX_ROW if ctx else b
        c = 0 if col is None else ids[col]
        return (layer, which, row, 0, c)
    return pl.BlockSpec((None, None, None, 1, width), imap)


def _per_segment(i, fn, latent_arg, ctx_arg):
    @pl.when(i == 0)
    def _():
        fn(0, CTX, ctx_arg)
        fn(CTX, TM, latent_arg)

    @pl.when(i != 0)
    def _():
        fn(0, TM, latent_arg)


def _norm_rows(x_ref, dst_ref, dst_off, lo, hi, gain, shift):
    def body(j, _):
        r = pl.multiple_of(lo + j * NORM_RB, NORM_RB)
        x = x_ref[pl.ds(r, NORM_RB), :]
        inv = lax.rsqrt(jnp.mean(x * x, axis=-1, keepdims=True) + EPS)
        dst_ref[pl.ds(dst_off + r, NORM_RB), :] = ((x * inv) * gain + shift).astype(BF16)
        return 0
    n = (hi - lo) // NORM_RB
    lax.fori_loop(0, n, body, 0, unroll=min(n, NORM_UNROLL))


def _norm_tile(i, x_ref, dst_ref, dst_off, g_ref, shb, scb, shc, scc):
    g = g_ref[...]

    def run(lo, hi, mod):
        _norm_rows(x_ref, dst_ref, dst_off, lo, hi, g * (1.0 + mod[0]), mod[1])
    _per_segment(i, run, (scb[...], shb[...]), (scc[...], shc[...]))


def _adaln_kernel(c_ref, w_ref, b_ref, o_ref):
    c = c_ref[...]
    s = (c * jax.nn.sigmoid(c)).astype(BF16)
    o_ref[...] = _dot(s, w_ref[...].astype(BF16)) + b_ref[...]


def _adaln(cc, w_ada, b_ada):
    tn = 1024
    nj = D // tn
    return pl.pallas_call(
        _adaln_kernel,
        grid=(DEPTH, 6 * nj),
        in_specs=[
            pl.BlockSpec((MOD_ROWS, D), lambda l, j: (0, 0)),
            pl.BlockSpec((None, D, tn), lambda l, j: (l, 0, j)),
            pl.BlockSpec((None, 1, tn), lambda l, j: (l, 0, j)),
        ],
        out_specs=pl.BlockSpec((None, None, MOD_ROWS, tn), lambda l, j: (l, j // nj, 0, j % nj)),
        out_shape=jax.ShapeDtypeStruct((DEPTH, 6, MOD_ROWS, D), F32),
        compiler_params=_cparams(("parallel", "parallel")),
        name="adaln",
    )(cc, w_ada, b_ada.reshape(DEPTH, 1, 6 * D))


def _inproj_kernel(x_ref, g_ref, shb, scb, shc, scc, w_ref, o_ref, hx_s):
    i = pl.program_id(1)

    @pl.when(pl.program_id(2) == 0)
    def _():
        _norm_tile(i, x_ref, hx_s, 0, g_ref, shb, scb, shc, scc)

    o_ref[...] = _dot(hx_s[...], w_ref[...])


def _inproj(xall, mod5, layer, g_all, w_all):
    tn = 1024
    return pl.pallas_call(
        _inproj_kernel,
        grid=(B, NT, PROJ // tn),
        in_specs=[
            pl.BlockSpec((None, TM, D), lambda b, i, j: (b, i, 0)),
            pl.BlockSpec((None, 1, D), lambda b, i, j: (layer, 0, 0)),
            _mod_spec(layer, 0, False), _mod_spec(layer, 1, False),
            _mod_spec(layer, 0, True), _mod_spec(layer, 1, True),
            pl.BlockSpec((None, D, tn), lambda b, i, j: (layer, 0, j)),
        ],
        out_specs=pl.BlockSpec((None, TM, tn), lambda b, i, j: (b, i, j)),
        out_shape=jax.ShapeDtypeStruct((B, T, PROJ), F32),
        scratch_shapes=[pltpu.VMEM((TM, D), BF16)],
        compiler_params=_cparams(("parallel", "parallel", "arbitrary")),
        name="inproj",
    )(xall, g_all, mod5, mod5, mod5, mod5, w_all)


def _tile_scan(a, b, carry, rev):
    rid = lax.broadcasted_iota(jnp.int32, a.shape, 0)
    for d in (1, 2, 4):
        sh = SUBLANES - d if rev else d
        keep = (rid < SUBLANES - d) if rev else (rid >= d)
        a_s = jnp.where(keep, pltpu.roll(a, sh, 0), 1.0)
        b_s = jnp.where(keep, pltpu.roll(b, sh, 0), 0.0)
        b = a * b_s + b
        a = a * a_s
    h = a * carry + b
    return h, (h[0:1] if rev else h[SUBLANES - 1:SUBLANES])


def _lru_kernel(ax_ref, ay_ref, cw_ref, cb_ref, gw_ref, gb_ref, lam_ref, o_ref, af, bf, ab, bb):
    W = LRU_HALF
    cw = cw_ref[...]
    cb = cb_ref[...]
    gb = gb_ref[...]
    nl = -lam_ref[...]
    softplus = jnp.maximum(nl, 0.0) + jnp.log(1.0 + jnp.exp(-jnp.abs(nl)))
    coef = -LRU_C * softplus
    zeros8 = jnp.zeros((SUBLANES, W), F32)
    n_ext = LRU_RB + 2 * SUBLANES

    for blk in range(T // LRU_RB):
        r0 = blk * LRU_RB
        r1 = r0 + LRU_RB
        prev = zeros8 if r0 in (0, CTX) else ax_ref[r0 - SUBLANES:r0, :]
        nxt = zeros8 if r1 in (CTX, T) else ax_ref[r1:r1 + SUBLANES, :]
        ext = jnp.concatenate([prev, ax_ref[r0:r1, :], nxt], axis=0)
        lo, hi = SUBLANES, SUBLANES + LRU_RB
        xc = cb + pltpu.roll(ext, 2, 0)[lo:hi] * cw[0:1]
        xc = xc + pltpu.roll(ext, 1, 0)[lo:hi] * cw[1:2]
        xc = xc + ext[lo:hi] * cw[2:3]
        xc = xc + pltpu.roll(ext, n_ext - 1, 0)[lo:hi] * cw[3:4]
        xb = xc.astype(BF16)
        for d, (a_s, b_s) in enumerate(((af, bf), (ab, bb))):
            r = jax.nn.sigmoid(_dot(xb, gw_ref[2 * d]) + gb[2 * d:2 * d + 1])
            gi = jax.nn.sigmoid(_dot(xb, gw_ref[2 * d + 1]) + gb[2 * d + 1:2 * d + 2])
            log_a = coef[d:d + 1] * r
            a_s[r0:r1, :] = jnp.exp(log_a)
            th = jnp.tanh(log_a)
            b_s[r0:r1, :] = jnp.sqrt(-2.0 * th / (1.0 - th)) * (gi * xc)

    def make_body(f_base, b_top):
        def body(k, carry):
            cf, cr = carry
            rf = pl.multiple_of((f_base + k) * SUBLANES, SUBLANES)
            hf, cf = _tile_scan(af[pl.ds(rf, SUBLANES), :], bf[pl.ds(rf, SUBLANES), :], cf, False)
            bf[pl.ds(rf, SUBLANES), :] = hf
            rb = pl.multiple_of((b_top - k) * SUBLANES, SUBLANES)
            hb, cr = _tile_scan(ab[pl.ds(rb, SUBLANES), :], bb[pl.ds(rb, SUBLANES), :], cr, True)
            bb[pl.ds(rb, SUBLANES), :] = hb
            return cf, cr
        return body

    z1 = jnp.zeros((1, W), F32)
    n_ctx = CTX // SUBLANES
    n_all = T // SUBLANES
    carry = lax.fori_loop(0, n_ctx, make_body(0, n_ctx - 1), (z1, z1))
    lax.fori_loop(0, n_all - n_ctx, make_body(n_ctx, n_all - 1), carry)

    for blk in range(T // LRU_RB):
        r0 = blk * LRU_RB
        r1 = r0 + LRU_RB
        h = bf[r0:r1, :] + bb[r0:r1, :]
        o_ref[r0:r1, :] = (h * jax.nn.gelu(ay_ref[r0:r1, :])).astype(BF16)


def _lru(z, layer, cw, cb, gw_bd, gb, lam):
    W = LRU_HALF
    nh = LRU_W // W
    return pl.pallas_call(
        _lru_kernel,
        grid=(B, nh),
        in_specs=[
            pl.BlockSpec((None, T, W), lambda b, c: (b, 0, c)),
            pl.BlockSpec((None, T, W), lambda b, c: (b, 0, nh + c)),
            pl.BlockSpec((None, LRU_TAPS, W), lambda b, c: (layer, 0, c)),
            pl.BlockSpec((None, 1, W), lambda b, c: (layer, 0, c)),
            pl.BlockSpec((None, None, 4, W, W), lambda b, c: (layer, c, 0, 0, 0)),
            pl.BlockSpec((None, 4, W), lambda b, c: (layer, 0, c)),
            pl.BlockSpec((None, 2, W), lambda b, c: (layer, 0, c)),
        ],
        out_specs=pl.BlockSpec((None, T, W), lambda b, c: (b, 0, c)),
        out_shape=jax.ShapeDtypeStruct((B, T, LRU_W), BF16),
        scratch_shapes=[pltpu.VMEM((T, W), F32)] * 4,
        compiler_params=_cparams(("parallel", "parallel")),
        name="lru",
    )(z, z, cw, cb, gw_bd, gb, lam)


def _attn_kernel(q_ref, k_ref, v_ref, cos_ref, sin_ref, bias_ref, o_ref,
                 qraw_s, qrot_s, qctx_s, krot_s, kc_s, v_s, sc_s, pc_s, ow_s):
    lane = lax.broadcasted_iota(jnp.int32, (1, LANES), 1)
    head0 = lane < HEAD
    second16 = ((lane // ROPE_F) % 2) == 1
    scale = HEAD ** -0.5 * LOG2E
    W2 = 2 * GRID_W

    def rope(x, c, s):
        partner = jnp.where(second16, pltpu.roll(x, ROPE_F, 1), pltpu.roll(x, LANES - ROPE_F, 1))
        return x * c + partner * s

    def stack_heads(x):
        return jnp.concatenate([jnp.where(head0, x, 0.0), jnp.where(head0, 0.0, x)], axis=0).astype(BF16)

    def unstack_heads(y):
        n = y.shape[0] // 2
        return jnp.where(head0, y[:n], y[n:])

    qctx_s[...] = stack_heads(q_ref[0:CTX, :] * scale)
    kc_s[...] = k_ref[0:CTX, :].astype(BF16)
    v_s[0:CTX, :] = v_ref[0:CTX, :].astype(BF16)
    rows_per_blk = ATT_RB // GRID_W
    for blk in range(SEQ // ATT_RB):
        r0 = blk * ATT_RB
        r1 = r0 + ATT_RB
        c = cos_ref[r0:r1, :]
        s = sin_ref[r0:r1, :]
        q = q_ref[CTX + r0:CTX + r1, :]
        qs = q * scale
        qr = rope(q, c, s) * scale
        for j in range(rows_per_blk):
            qraw_s[blk * rows_per_blk + j] = stack_heads(qs[j * GRID_W:(j + 1) * GRID_W])
            qrot_s[blk * rows_per_blk + j] = stack_heads(qr[j * GRID_W:(j + 1) * GRID_W])
        krot_s[r0:r1, :] = rope(k_ref[CTX + r0:CTX + r1, :], c, s).astype(BF16)
        v_s[CTX + r0:CTX + r1, :] = v_ref[CTX + r0:CTX + r1, :].astype(BF16)

    s = _dot_nt(qctx_s[...], kc_s[...])
    e = jnp.exp2(s - jnp.max(s, axis=-1, keepdims=True))
    p = (e / jnp.sum(e, axis=-1, keepdims=True)).astype(BF16)
    o_ref[0:CTX, :] = unstack_heads(_dot(p, v_s[0:CTX, :])).astype(BF16)

    for g in range(GRID_H // ATT_GROUP):
        g0, g1 = g * ATT_GROUP, (g + 1) * ATT_GROUP
        sc = _dot_nt(qraw_s[g0:g1].reshape(ATT_GROUP * W2, LANES), kc_s[...])
        sc_s[g0:g1] = sc.reshape(ATT_GROUP, W2, CTX)

    win = NA_ROWS * GRID_W

    def lane_fold(op, *xs):
        cols = [x[:, c:c + LANES] for x in xs for c in range(0, x.shape[1], LANES)]
        acc = cols[0]
        for col in cols[1:]:
            acc = op(acc, col)
        return acc

    def body(t, _):
        rows = [t * ATT_UNROLL + j for j in range(ATT_UNROLL)]
        starts = [jnp.clip(r - NA_ROWS // 2, 0, GRID_H - NA_ROWS) for r in rows]
        sw = [_dot_nt(qrot_s[r], krot_s[pl.ds(pl.multiple_of(rs * GRID_W, GRID_W), win), :]) + bias_ref[r - rs]
              for r, rs in zip(rows, starts)]
        sc = [sc_s[r] for r in rows]
        m = [jnp.max(lane_fold(jnp.maximum, a, c), axis=-1, keepdims=True) for a, c in zip(sw, sc)]
        ew = [jnp.exp2(a - mm) for a, mm in zip(sw, m)]
        ec = [jnp.exp2(c - mm) for c, mm in zip(sc, m)]
        inv = [1.0 / jnp.sum(lane_fold(jnp.add, a, c), axis=-1, keepdims=True) for a, c in zip(ew, ec)]
        for r, rs, a, c, iv in zip(rows, starts, ew, ec, inv):
            kx = pl.multiple_of(CTX + rs * GRID_W, GRID_W)
            ow_s[r] = _dot(a.astype(BF16), v_s[pl.ds(kx, win), :]) * iv
            pc_s[r] = (c * iv).astype(BF16)
        return 0

    lax.fori_loop(0, GRID_H // ATT_UNROLL, body, 0)

    for g in range(GRID_H // ATT_GROUP):
        g0, g1 = g * ATT_GROUP, (g + 1) * ATT_GROUP
        oc = _dot(pc_s[g0:g1].reshape(ATT_GROUP * W2, CTX), v_s[0:CTX, :]).reshape(ATT_GROUP, W2, LANES)
        tot = oc + ow_s[g0:g1]
        for j in range(ATT_GROUP):
            row = CTX + (g0 + j) * GRID_W
            o_ref[row:row + GRID_W, :] = unstack_heads(tot[j]).astype(BF16)


def _attn(z, layer, cos_t, sin_t, biasmask):
    nhp = NA_W // LANES
    qb = 2 * LRU_W // LANES
    return pl.pallas_call(
        _attn_kernel,
        grid=(nhp, B),
        in_specs=[
            pl.BlockSpec((None, T, LANES), lambda p, b: (b, 0, qb + p)),
            pl.BlockSpec((None, T, LANES), lambda p, b: (b, 0, qb + nhp + p)),
            pl.BlockSpec((None, T, LANES), lambda p, b: (b, 0, qb + 2 * nhp + p)),
            pl.BlockSpec((SEQ, LANES), lambda p, b: (0, 0)),
            pl.BlockSpec((SEQ, LANES), lambda p, b: (0, 0)),
            pl.BlockSpec((None, None, NA_ROWS, 2 * GRID_W, NA_ROWS * GRID_W), lambda p, b: (layer, p, 0, 0, 0)),
        ],
        out_specs=pl.BlockSpec((None, T, LANES), lambda p, b: (b, 0, p)),
        out_shape=jax.ShapeDtypeStruct((B, T, NA_W), BF16),
        scratch_shapes=[
            pltpu.VMEM((GRID_H, 2 * GRID_W, LANES), BF16),
            pltpu.VMEM((GRID_H, 2 * GRID_W, LANES), BF16),
            pltpu.VMEM((2 * CTX, LANES), BF16),
            pltpu.VMEM((SEQ, LANES), BF16),
            pltpu.VMEM((CTX, LANES), BF16),
            pltpu.VMEM((T, LANES), BF16),
            pltpu.VMEM((GRID_H, 2 * GRID_W, CTX), F32),
            pltpu.VMEM((GRID_H, 2 * GRID_W, CTX), BF16),
            pltpu.VMEM((GRID_H, 2 * GRID_W, LANES), F32),
        ],
        compiler_params=_cparams(("parallel", "parallel")),
        name="attn",
    )(z, z, z, cos_t, sin_t, biasmask)


def _sgu_kernel(u_ref, v_ref, g_ref, b_ref, ws_ref, bs_ref, o_ref):
    lane = lax.broadcasted_iota(jnp.int32, (1, LANES), 1)
    first = lane < HEAD
    for n in range(TM // SGU_CHUNK):
        r0 = n * SGU_CHUNK
        r1 = r0 + SGU_CHUNK
        v = jax.nn.gelu(v_ref[r0:r1, :])
        mu = jnp.mean(v, axis=-1, keepdims=True)
        var = jnp.mean(jnp.square(v - mu), axis=-1, keepdims=True)
        vn = ((v - mu) * lax.rsqrt(var + EPS) * g_ref[...] + b_ref[...]).astype(BF16)
        for j in range(SGU_W // LANES):
            c0 = j * LANES
            c1 = c0 + LANES
            vp = vn[:, c0:c1]
            mixed = jnp.where(first, _dot(ws_ref[2 * j], vp), _dot(ws_ref[2 * j + 1], vp)) + bs_ref[:, c0:c1]
            o_ref[r0:r1, c0:c1] = (jax.nn.gelu(u_ref[r0:r1, c0:c1]) * mixed).astype(BF16)


def _sgu(z, layer, ln_g, ln_b, ws_bf, bs_full):
    ub = (2 * LRU_W + 3 * NA_W) // SGU_W
    return pl.pallas_call(
        _sgu_kernel,
        grid=(B, NT),
        in_specs=[
            pl.BlockSpec((None, TM, SGU_W), lambda b, i: (b, i, ub)),
            pl.BlockSpec((None, TM, SGU_W), lambda b, i: (b, i, ub + 1)),
            pl.BlockSpec((None, 1, SGU_W), lambda b, i: (layer, 0, 0)),
            pl.BlockSpec((None, 1, SGU_W), lambda b, i: (layer, 0, 0)),
            pl.BlockSpec((None, SGU_G, SGU_CHUNK, SGU_CHUNK), lambda b, i: (layer, 0, 0, 0)),
            pl.BlockSpec((None, SGU_CHUNK, SGU_W), lambda b, i: (layer, 0, 0)),
        ],
        out_specs=pl.BlockSpec((None, TM, SGU_W), lambda b, i: (b, i, 0)),
        out_shape=jax.ShapeDtypeStruct((B, T, SGU_W), BF16),
        compiler_params=_cparams(("parallel", "parallel")),
        name="sgu",
    )(z, z, ln_g, ln_b, ws_bf, bs_full)


def _outproj_kernel(x_ref, a_ref, b_ref, c_ref, gtb, gtc, w_ref, o_ref):
    i = pl.program_id(1)
    acc = _dot(a_ref[...], w_ref[0:LRU_W, :])
    acc = acc + _dot(b_ref[...], w_ref[LRU_W:LRU_W + NA_W, :])
    acc = acc + _dot(c_ref[...], w_ref[LRU_W + NA_W:D, :])
    is_ctx = (lax.broadcasted_iota(jnp.int32, (TM, 1), 0) + i * TM) < CTX
    o_ref[...] = x_ref[...] + jnp.where(is_ctx, gtc[...], gtb[...]) * acc


def _outproj(xall, oa, ob, oc, mod5, layer, w_all):
    tn = D
    return pl.pallas_call(
        _outproj_kernel,
        grid=(B, NT, D // tn),
        in_specs=[
            pl.BlockSpec((None, TM, tn), lambda b, i, j: (b, i, j)),
            pl.BlockSpec((None, TM, LRU_W), lambda b, i, j: (b, i, 0)),
            pl.BlockSpec((None, TM, NA_W), lambda b, i, j: (b, i, 0)),
            pl.BlockSpec((None, TM, SGU_W), lambda b, i, j: (b, i, 0)),
            _mod_spec(layer, 2, False, tn, 2), _mod_spec(layer, 2, True, tn, 2),
            pl.BlockSpec((None, D, tn), lambda b, i, j: (layer, 0, j), pipeline_mode=pl.Buffered(1)),
        ],
        out_specs=pl.BlockSpec((None, TM, tn), lambda b, i, j: (b, i, j)),
        out_shape=jax.ShapeDtypeStruct((B, T, D), F32),
        compiler_params=_cparams(("parallel", "parallel", "parallel"), VMEM_BIG),
        name="outproj",
    )(xall, oa, ob, oc, mod5, mod5, w_all)


def _ffn_kernel(x_ref, xp_ref, xn_ref, g_ref, shb, scb, gtb, shc, scc, gtc, wa_ref, wg_ref, cw_ref, cb_ref, wd_ref,
                o_ref, hx_s, u_s, act_s):
    i = pl.program_id(1)
    k = pl.program_id(2)
    rb = FFN_RB
    left_blocks = (0, CTX // rb)
    right_blocks = (CTX // rb - 1, TM // rb - 1)

    @pl.when(k == 0)
    def _():
        gain_b = g_ref[...] * (1.0 + scb[...])
        _norm_rows(xp_ref, hx_s, 0, 0, HALO, gain_b, shb[...])
        _norm_rows(xn_ref, hx_s, HALO + TM, 0, HALO, gain_b, shb[...])
        _norm_tile(i, x_ref, hx_s, HALO, g_ref, shb, scb, shc, scc)
        o_ref[...] = jnp.zeros_like(o_ref)

    slab = TM // FFN_SPLIT
    for h in range(FFN_SPLIT):
        q0, q1 = h * slab, (h + 1) * slab + 2 * HALO
        u_s[q0:q1, :TH] = _dot(hx_s[q0:q1, :], wa_ref[...])
        u_s[q0:q1, TH:] = _dot(hx_s[q0:q1, :], wg_ref[...])
    cw = cw_ref[...]
    cb = cb_ref[...]
    for blk in range(TM // rb):
        r0 = HALO + blk * rb
        rows = lax.broadcasted_iota(jnp.int32, (rb, 1), 0) + (i * TM + blk * rb)
        ul = u_s[r0 - 1:r0 - 1 + rb, :]
        ur = u_s[r0 + 1:r0 + 1 + rb, :]
        if blk in left_blocks:
            ul = jnp.where(jnp.logical_and(rows != 0, rows != CTX), ul, 0.0)
        if blk in right_blocks:
            ur = jnp.where(jnp.logical_and(rows != CTX - 1, rows != T - 1), ur, 0.0)
        y = cb + ul * cw[0:1]
        y = y + u_s[r0:r0 + rb, :] * cw[1:2]
        y = y + ur * cw[2:3]
        a = y[:, :TH]
        gg = y[:, TH:]
        act_s[blk * rb:(blk + 1) * rb, :] = (gg * jax.nn.sigmoid(gg) * a).astype(BF16)
        if (blk + 1) * rb % slab == 0:
            s0 = (blk + 1) * rb - slab
            o_ref[s0:s0 + slab, :] += _dot(act_s[s0:s0 + slab, :], wd_ref[...])

    @pl.when(k == NK - 1)
    def _():
        def finish(lo, hi, gate):
            o_ref[lo:hi, :] = x_ref[lo:hi, :] + gate * o_ref[lo:hi, :]
        _per_segment(i, finish, gtb[...], gtc[...])


def _ffn(xall, mod5, layer, g, wa, wg, cw_r, cb_r, wd_p):
    hb = TM // HALO
    last = T // HALO - 1
    col_map = lambda b, i, k: (layer, 0, k)
    return pl.pallas_call(
        _ffn_kernel,
        grid=(B, NT, NK),
        in_specs=[
            pl.BlockSpec((None, TM, D), lambda b, i, k: (b, i, 0)),
            pl.BlockSpec((None, HALO, D), lambda b, i, k: (b, jnp.maximum(i * hb - 1, 0), 0)),
            pl.BlockSpec((None, HALO, D), lambda b, i, k: (b, jnp.minimum((i + 1) * hb, last), 0)),
            pl.BlockSpec((None, 1, D), lambda b, i, k: (layer, 0, 0)),
            _mod_spec(layer, 3, False), _mod_spec(layer, 4, False), _mod_spec(layer, 5, False),
            _mod_spec(layer, 3, True), _mod_spec(layer, 4, True), _mod_spec(layer, 5, True),
            pl.BlockSpec((None, D, TH), col_map),
            pl.BlockSpec((None, D, TH), col_map),
            pl.BlockSpec((None, 3, 2 * TH), col_map),
            pl.BlockSpec((None, 1, 2 * TH), col_map),
            pl.BlockSpec((None, TH, D), lambda b, i, k: (layer, k, 0)),
        ],
        out_specs=pl.BlockSpec((None, TM, D), lambda b, i, k: (b, i, 0)),
        out_shape=jax.ShapeDtypeStruct((B, T, D), F32),
        scratch_shapes=[
            pltpu.VMEM((TM + 2 * HALO, D), BF16),
            pltpu.VMEM((TM + 2 * HALO, 2 * TH), F32),
            pltpu.VMEM((TM, TH), BF16),
        ],
        compiler_params=_cparams(("parallel", "parallel", "arbitrary"), VMEM_BIG),
        name="ffn",
    )(xall, xall, xall, g, mod5, mod5, mod5, mod5, mod5, mod5, wa, wg, cw_r, cb_r, wd_p)


def _final_kernel(x_ref, g_ref, o_ref):
    x = x_ref[...]
    o_ref[...] = x * lax.rsqrt(jnp.mean(x * x, axis=-1, keepdims=True) + EPS) * g_ref[...]


def _final_norm(xall, g):
    tm = CTX
    return pl.pallas_call(
        _final_kernel,
        grid=(B, SEQ // tm),
        in_specs=[
            pl.BlockSpec((None, tm, D), lambda b, i: (b, i + 1, 0)),
            pl.BlockSpec((1, D), lambda b, i: (0, 0)),
        ],
        out_specs=pl.BlockSpec((None, tm, D), lambda b, i: (b, i, 0)),
        out_shape=jax.ShapeDtypeStruct((B, SEQ, D), F32),
        compiler_params=_cparams(("parallel", "parallel")),
        name="final_norm",
    )(xall, g.reshape(1, D))


def _rope_tables():
    t = jnp.arange(SEQ)
    pos = jnp.stack([t // GRID_W, t % GRID_W], axis=-1).astype(F32)
    inv = ROPE_BASE ** (-jnp.arange(ROPE_F, dtype=F32) / ROPE_F)
    ang = pos[:, :, None] * inv
    cos, sin = jnp.cos(ang), jnp.sin(ang)
    cos_h = jnp.concatenate([cos[:, 0], cos[:, 0], cos[:, 1], cos[:, 1]], axis=-1)
    sin_h = jnp.concatenate([-sin[:, 0], sin[:, 0], -sin[:, 1], sin[:, 1]], axis=-1)
    reps = LANES // HEAD
    return jnp.tile(cos_h, (1, reps)), jnp.tile(sin_h, (1, reps))


def _bias_tables(rpb):
    nl = rpb.shape[0]
    nr, nc = 2 * NA_ROWS - 1, 2 * NA_KC - 1
    pairs = rpb.reshape(nl, NA_HEADS // 2, 2, nr, nc) * LOG2E
    pairs = jnp.pad(pairs, ((0, 0), (0, 0), (0, 0), (0, 2 * NA_ROWS - nr), (0, LANES - nc)))
    return pl.pallas_call(
        _bias_kernel,
        grid=(nl, NA_HEADS // 2),
        in_specs=[pl.BlockSpec((None, None, 2, 2 * NA_ROWS, LANES), lambda l, p: (l, p, 0, 0, 0))],
        out_specs=pl.BlockSpec((None, None, NA_ROWS, 2 * GRID_W, NA_ROWS * GRID_W), lambda l, p: (l, p, 0, 0, 0)),
        out_shape=jax.ShapeDtypeStruct((nl, NA_HEADS // 2, NA_ROWS, 2 * GRID_W, NA_ROWS * GRID_W), F32),
        compiler_params=_cparams(("parallel", "parallel")),
        name="bias_table",
    )(pairs)


def _bias_kernel(rp_ref, o_ref):
    c = lax.broadcasted_iota(jnp.int32, (GRID_W, LANES), 0)
    lane = lax.broadcasted_iota(jnp.int32, (GRID_W, LANES), 1)
    kc = lane % GRID_W
    col_start = jnp.clip(c - NA_KC // 2, 0, GRID_W - NA_KC)
    in_win = jnp.logical_and(kc >= col_start, kc < col_start + NA_KC)
    first = lane < GRID_W
    per_blk = LANES // GRID_W
    for cls in range(NA_ROWS):
        for hd in range(2):
            for blk in range(NA_ROWS // per_blk):
                parts = []
                for h in range(per_blk):
                    m = blk * per_blk + h - cls + NA_ROWS - 1
                    row = jnp.broadcast_to(rp_ref[hd, m:m + 1, :], (GRID_W, LANES))
                    shift = (LANES - (NA_KC - 1) + h * GRID_W) % LANES
                    parts.append(pltpu.roll(row, shift, 1, stride=1, stride_axis=0))
                tile = jnp.where(in_win, jnp.where(first, parts[0], parts[1]), NEG)
                o_ref[cls, hd * GRID_W:(hd + 1) * GRID_W, blk * LANES:(blk + 1) * LANES] = tile


def _gate_blockdiag(gate_w):
    nl = gate_w.shape[0]
    nh = LRU_W // LRU_HALF
    gper = LRU_HALF // HEAD
    w = gate_w.reshape(nl, 4, nh, gper, HEAD, HEAD)
    eye = jnp.eye(gper, dtype=gate_w.dtype)
    bd = jnp.einsum('lkhgio,gj->lkhgijo', w, eye).reshape(nl, 4, nh, LRU_HALF, LRU_HALF)
    return jnp.transpose(bd, (0, 2, 1, 3, 4)).astype(BF16)


def _ffn_layout(w_up, conv_w, conv_b, w_down):
    pad = FFN_HP - FFN_H
    nl = w_up.shape[0]

    def inter(m):
        r = m.shape[1]
        a = jnp.pad(m[..., :FFN_H], ((0, 0), (0, 0), (0, pad))).reshape(nl, r, NK, 1, TH)
        g = jnp.pad(m[..., FFN_H:], ((0, 0), (0, 0), (0, pad))).reshape(nl, r, NK, 1, TH)
        return jnp.concatenate([a, g], axis=3).reshape(nl, r, NK * 2 * TH)

    zc = jnp.zeros((nl, D, pad), BF16)
    wa = jnp.concatenate([w_up[..., :FFN_H].astype(BF16), zc], axis=2)
    wg = jnp.concatenate([w_up[..., FFN_H:].astype(BF16), zc], axis=2)
    wd = jnp.concatenate([w_down.astype(BF16), jnp.zeros((nl, pad, D), BF16)], axis=1)
    return wa, wg, inter(conv_w), inter(conv_b[:, None, :]), wd


def kernel(x, c, ctx, c_ctx, w_ada, b_ada, norm_mix_g, norm_ffn_g, w_in, lru_conv_w, lru_conv_b, lru_gate_w, lru_gate_b, lru_lambda, na_rpb, sgu_ln_g, sgu_ln_b, sgu_w, sgu_b, w_out, ffn_up, ffn_conv_w, ffn_conv_b, ffn_down, final_norm_g):
    xall = jnp.concatenate([ctx, x], axis=1)
    cc = jnp.concatenate([c, c_ctx[None], jnp.zeros((MOD_ROWS - B - 1, D), F32)], axis=0)
    mod5 = _adaln(cc, w_ada, b_ada).reshape(DEPTH, 6, MOD_ROWS, 1, D)
    cos_t, sin_t = _rope_tables()

    g_mix = norm_mix_g.reshape(DEPTH, 1, D)
    g_ffn = norm_ffn_g.reshape(DEPTH, 1, D)
    w_in_bf = w_in.astype(BF16)
    w_out_bf = w_out.astype(BF16)
    lru_cb = lru_conv_b.reshape(DEPTH, 1, LRU_W)
    lru_gw = _gate_blockdiag(lru_gate_w)
    lru_gb = lru_gate_b.reshape(DEPTH, 4, LRU_W)
    bias = _bias_tables(na_rpb)
    sgu_g = sgu_ln_g.reshape(DEPTH, 1, SGU_W)
    sgu_bb = sgu_ln_b.reshape(DEPTH, 1, SGU_W)
    sgu_w_bf = sgu_w.astype(BF16)
    sgu_bias = jnp.repeat(jnp.swapaxes(sgu_b, 1, 2), HEAD, axis=2)
    ffn_params = _ffn_layout(ffn_up, ffn_conv_w, ffn_conv_b, ffn_down)

    for l in range(DEPTH):
        z = _inproj(xall, mod5, l, g_mix, w_in_bf)
        oa = _lru(z, l, lru_conv_w, lru_cb, lru_gw, lru_gb, lru_lambda)
        ob = _attn(z, l, cos_t, sin_t, bias)
        oc = _sgu(z, l, sgu_g, sgu_bb, sgu_w_bf, sgu_bias)
        xall = _outproj(xall, oa, ob, oc, mod5, l, w_out_bf)
        xall = _ffn(xall, mod5, l, g_ffn, *ffn_params)
    return _final_norm(xall, final_norm_g)
```

```python
import functools

import jax
import jax.numpy as jnp
from jax import lax
from jax.experimental import pallas as pl
from jax.experimental.pallas import tpu as pltpu

F32 = jnp.float32
BF16 = jnp.bfloat16

D = 2048
B = 8
SEQ = 2048
DEPTH = 4
GRID_W = 64
GRID_H = SEQ // GRID_W
CTX = 256
T = CTX + SEQ
HEAD = 64
LRU_W = 512
LRU_TAPS = 4
LRU_C = 8.0
NA_W = 1024
NA_HEADS = NA_W // HEAD
NA_ROWS = 8
NA_KC = 16
SGU_W = 512
SGU_G = SGU_W // HEAD
SGU_CHUNK = 128
PROJ = 2 * LRU_W + 3 * NA_W + 2 * SGU_W
FFN_H = 5504
EPS = 1e-6
NEG = -1e30
LOG2E = 1.4426950408889634
ROPE_BASE = 10000.0
ROPE_F = HEAD // 4
MOD_ROWS = 16
CTX_ROW = B

LANES = 128
SUBLANES = 8
VMEM_BIG = 56 * 1024 * 1024
VMEM_MID = 40 * 1024 * 1024

TM = 768
NT = T // TM
HALO = 16
TH = 512
FFN_HP = 5632
NK = FFN_HP // TH
LRU_HALF = 256
LRU_RB = 128
ATT_RB = 256
NORM_RB = 16
NORM_UNROLL = 4
FFN_RB = 128
FFN_SPLIT = 2
ATT_UNROLL = 4
ATT_GROUP = 4


def _cparams(sem, vmem=VMEM_MID):
    return pltpu.CompilerParams(dimension_semantics=sem, vmem_limit_bytes=vmem)


def _dot(a, b):
    return jnp.dot(a, b, preferred_element_type=F32)


def _dot_nt(a, b):
    return lax.dot_general(a, b, (((1,), (1,)), ((), ())), preferred_element_type=F32)


def _mod_spec(layer, which, ctx, width=D, col=None):
    def imap(*ids):
        b = ids[0]
        row = CTX_ROW if ctx else b
        c = 0 if col is None else ids[col]
        return (layer, which, row, 0, c)
    return pl.BlockSpec((None, None, None, 1, width), imap)


def _per_segment(i, fn, latent_arg, ctx_arg):
    @pl.when(i == 0)
    def _():
        fn(0, CTX, ctx_arg)
        fn(CTX, TM, latent_arg)

    @pl.when(i != 0)
    def _():
        fn(0, TM, latent_arg)


def _norm_rows(x_ref, dst_ref, dst_off, lo, hi, gain, shift):
    def body(j, _):
        r = pl.multiple_of(lo + j * NORM_RB, NORM_RB)
        x = x_ref[pl.ds(r, NORM_RB), :]
        inv = lax.rsqrt(jnp.mean(x * x, axis=-1, keepdims=True) + EPS)
        dst_ref[pl.ds(dst_off + r, NORM_RB), :] = ((x * inv) * gain + shift).astype(BF16)
        return 0
    n = (hi - lo) // NORM_RB
    lax.fori_loop(0, n, body, 0, unroll=min(n, NORM_UNROLL))


def _norm_tile(i, x_ref, dst_ref, dst_off, g_ref, shb, scb, shc, scc):
    g = g_ref[...]

    def run(lo, hi, mod):
        _norm_rows(x_ref, dst_ref, dst_off, lo, hi, g * (1.0 + mod[0]), mod[1])
    _per_segment(i, run, (scb[...], shb[...]), (scc[...], shc[...]))


def _adaln_kernel(c_ref, w_ref, b_ref, o_ref):
    c = c_ref[...]
    s = (c * jax.nn.sigmoid(c)).astype(BF16)
    o_ref[...] = _dot(s, w_ref[...].astype(BF16)) + b_ref[...]


def _adaln(cc, w_ada, b_ada):
    tn = 1024
    nj = D // tn
    return pl.pallas_call(
        _adaln_kernel,
        grid=(DEPTH, 6 * nj),
        in_specs=[
            pl.BlockSpec((MOD_ROWS, D), lambda l, j: (0, 0)),
            pl.BlockSpec((None, D, tn), lambda l, j: (l, 0, j)),
            pl.BlockSpec((None, 1, tn), lambda l, j: (l, 0, j)),
        ],
        out_specs=pl.BlockSpec((None, None, MOD_ROWS, tn), lambda l, j: (l, j // nj, 0, j % nj)),
        out_shape=jax.ShapeDtypeStruct((DEPTH, 6, MOD_ROWS, D), F32),
        compiler_params=_cparams(("parallel", "parallel")),
        name="adaln",
    )(cc, w_ada, b_ada.reshape(DEPTH, 1, 6 * D))


def _inproj_kernel(x_ref, g_ref, shb, scb, shc, scc, w_ref, o_ref, hx_s):
    i = pl.program_id(1)

    @pl.when(pl.program_id(2) == 0)
    def _():
        _norm_tile(i, x_ref, hx_s, 0, g_ref, shb, scb, shc, scc)

    o_ref[...] = _dot(hx_s[...], w_ref[...])


def _inproj(xall, mod5, layer, g_all, w_all):
    tn = PROJ // 2
    return pl.pallas_call(
        _inproj_kernel,
        grid=(B, NT, PROJ // tn),
        in_specs=[
            pl.BlockSpec((None, TM, D), lambda b, i, j: (b, i, 0)),
            pl.BlockSpec((None, 1, D), lambda b, i, j: (layer, 0, 0)),
            _mod_spec(layer, 0, False), _mod_spec(layer, 1, False),
            _mod_spec(layer, 0, True), _mod_spec(layer, 1, True),
            pl.BlockSpec((None, D, tn), lambda b, i, j: (layer, 0, j)),
        ],
        out_specs=pl.BlockSpec((None, TM, tn), lambda b, i, j: (b, i, j)),
        out_shape=jax.ShapeDtypeStruct((B, T, PROJ), F32),
        scratch_shapes=[pltpu.VMEM((TM, D), BF16)],
        compiler_params=_cparams(("parallel", "parallel", "arbitrary"), VMEM_BIG),
        name="inproj",
    )(xall, g_all, mod5, mod5, mod5, mod5, w_all)


def _tile_scan(a, b, carry, rev):
    rid = lax.broadcasted_iota(jnp.int32, a.shape, 0)
    for d in (1, 2, 4):
        sh = SUBLANES - d if rev else d
        keep = (rid < SUBLANES - d) if rev else (rid >= d)
        a_s = jnp.where(keep, pltpu.roll(a, sh, 0), 1.0)
        b_s = jnp.where(keep, pltpu.roll(b, sh, 0), 0.0)
        b = a * b_s + b
        a = a * a_s
    h = a * carry + b
    return h, (h[0:1] if rev else h[SUBLANES - 1:SUBLANES])


def _lru_kernel(ax_ref, ay_ref, cw_ref, cb_ref, gw_ref, gb_ref, lam_ref, o_ref, af, bf, ab, bb):
    W = LRU_HALF
    cw = cw_ref[...]
    cb = cb_ref[...]
    gb = gb_ref[...]
    nl = -lam_ref[...]
    softplus = jnp.maximum(nl, 0.0) + jnp.log(1.0 + jnp.exp(-jnp.abs(nl)))
    coef = -LRU_C * softplus
    zeros8 = jnp.zeros((SUBLANES, W), F32)
    n_ext = LRU_RB + 2 * SUBLANES

    for blk in range(T // LRU_RB):
        r0 = blk * LRU_RB
        r1 = r0 + LRU_RB
        prev = zeros8 if r0 in (0, CTX) else ax_ref[r0 - SUBLANES:r0, :]
        nxt = zeros8 if r1 in (CTX, T) else ax_ref[r1:r1 + SUBLANES, :]
        ext = jnp.concatenate([prev, ax_ref[r0:r1, :], nxt], axis=0)
        lo, hi = SUBLANES, SUBLANES + LRU_RB
        xc = cb + pltpu.roll(ext, 2, 0)[lo:hi] * cw[0:1]
        xc = xc + pltpu.roll(ext, 1, 0)[lo:hi] * cw[1:2]
        xc = xc + ext[lo:hi] * cw[2:3]
        xc = xc + pltpu.roll(ext, n_ext - 1, 0)[lo:hi] * cw[3:4]
        xb = xc.astype(BF16)
        for d, (a_s, b_s) in enumerate(((af, bf), (ab, bb))):
            r = jax.nn.sigmoid(_dot(xb, gw_ref[2 * d]) + gb[2 * d:2 * d + 1])
            gi = jax.nn.sigmoid(_dot(xb, gw_ref[2 * d + 1]) + gb[2 * d + 1:2 * d + 2])
            log_a = coef[d:d + 1] * r
            a_s[r0:r1, :] = jnp.exp(log_a)
            th = jnp.tanh(log_a)
            b_s[r0:r1, :] = jnp.sqrt(-2.0 * th / (1.0 - th)) * (gi * xc)

    def make_body(f_base, b_top):
        def body(k, carry):
            cf, cr = carry
            rf = pl.multiple_of((f_base + k) * SUBLANES, SUBLANES)
            hf, cf = _tile_scan(af[pl.ds(rf, SUBLANES), :], bf[pl.ds(rf, SUBLANES), :], cf, False)
            bf[pl.ds(rf, SUBLANES), :] = hf
            rb = pl.multiple_of((b_top - k) * SUBLANES, SUBLANES)
            hb, cr = _tile_scan(ab[pl.ds(rb, SUBLANES), :], bb[pl.ds(rb, SUBLANES), :], cr, True)
            bb[pl.ds(rb, SUBLANES), :] = hb
            return cf, cr
        return body

    z1 = jnp.zeros((1, W), F32)
    n_ctx = CTX // SUBLANES
    n_all = T // SUBLANES
    carry = lax.fori_loop(0, n_ctx, make_body(0, n_ctx - 1), (z1, z1))
    lax.fori_loop(0, n_all - n_ctx, make_body(n_ctx, n_all - 1), carry)

    for blk in range(T // LRU_RB):
        r0 = blk * LRU_RB
        r1 = r0 + LRU_RB
        h = bf[r0:r1, :] + bb[r0:r1, :]
        o_ref[r0:r1, :] = (h * jax.nn.gelu(ay_ref[r0:r1, :])).astype(BF16)


def _lru(z, layer, cw, cb, gw_bd, gb, lam):
    W = LRU_HALF
    nh = LRU_W // W
    return pl.pallas_call(
        _lru_kernel,
        grid=(B, nh),
        in_specs=[
            pl.BlockSpec((None, T, W), lambda b, c: (b, 0, c)),
            pl.BlockSpec((None, T, W), lambda b, c: (b, 0, nh + c)),
            pl.BlockSpec((None, LRU_TAPS, W), lambda b, c: (layer, 0, c)),
            pl.BlockSpec((None, 1, W), lambda b, c: (layer, 0, c)),
            pl.BlockSpec((None, None, 4, W, W), lambda b, c: (layer, c, 0, 0, 0)),
            pl.BlockSpec((None, 4, W), lambda b, c: (layer, 0, c)),
            pl.BlockSpec((None, 2, W), lambda b, c: (layer, 0, c)),
        ],
        out_specs=pl.BlockSpec((None, T, W), lambda b, c: (b, 0, c)),
        out_shape=jax.ShapeDtypeStruct((B, T, LRU_W), BF16),
        scratch_shapes=[pltpu.VMEM((T, W), F32)] * 4,
        compiler_params=_cparams(("parallel", "parallel")),
        name="lru",
    )(z, z, cw, cb, gw_bd, gb, lam)


def _attn_kernel(q_ref, k_ref, v_ref, cos_ref, sin_ref, bias_ref, o_ref,
                 qraw_s, qrot_s, qctx_s, krot_s, kc_s, v_s, sc_s, pc_s, ow_s):
    lane = lax.broadcasted_iota(jnp.int32, (1, LANES), 1)
    head0 = lane < HEAD
    second16 = ((lane // ROPE_F) % 2) == 1
    scale = HEAD ** -0.5 * LOG2E
    W2 = 2 * GRID_W

    def rope(x, c, s):
        partner = jnp.where(second16, pltpu.roll(x, ROPE_F, 1), pltpu.roll(x, LANES - ROPE_F, 1))
        return x * c + partner * s

    def stack_heads(x):
        return jnp.concatenate([jnp.where(head0, x, 0.0), jnp.where(head0, 0.0, x)], axis=0).astype(BF16)

    def unstack_heads(y):
        n = y.shape[0] // 2
        return jnp.where(head0, y[:n], y[n:])

    qctx_s[...] = stack_heads(q_ref[0:CTX, :] * scale)
    kc_s[...] = k_ref[0:CTX, :].astype(BF16)
    v_s[0:CTX, :] = v_ref[0:CTX, :].astype(BF16)
    rows_per_blk = ATT_RB // GRID_W
    for blk in range(SEQ // ATT_RB):
        r0 = blk * ATT_RB
        r1 = r0 + ATT_RB
        c = cos_ref[r0:r1, :]
        s = sin_ref[r0:r1, :]
        q = q_ref[CTX + r0:CTX + r1, :]
        qs = q * scale
        qr = rope(q, c, s) * scale
        for j in range(rows_per_blk):
            qraw_s[blk * rows_per_blk + j] = stack_heads(qs[j * GRID_W:(j + 1) * GRID_W])
            qrot_s[blk * rows_per_blk + j] = stack_heads(qr[j * GRID_W:(j + 1) * GRID_W])
        krot_s[r0:r1, :] = rope(k_ref[CTX + r0:CTX + r1, :], c, s).astype(BF16)
        v_s[CTX + r0:CTX + r1, :] = v_ref[CTX + r0:CTX + r1, :].astype(BF16)

    s = _dot_nt(qctx_s[...], kc_s[...])
    e = jnp.exp2(s - jnp.max(s, axis=-1, keepdims=True))
    p = (e / jnp.sum(e, axis=-1, keepdims=True)).astype(BF16)
    o_ref[0:CTX, :] = unstack_heads(_dot(p, v_s[0:CTX, :])).astype(BF16)

    for g in range(GRID_H // ATT_GROUP):
        g0, g1 = g * ATT_GROUP, (g + 1) * ATT_GROUP
        sc = _dot_nt(qraw_s[g0:g1].reshape(ATT_GROUP * W2, LANES), kc_s[...])
        sc_s[g0:g1] = sc.reshape(ATT_GROUP, W2, CTX)

    win = NA_ROWS * GRID_W

    def lane_fold(op, *xs):
        cols = [x[:, c:c + LANES] for x in xs for c in range(0, x.shape[1], LANES)]
        acc = cols[0]
        for col in cols[1:]:
            acc = op(acc, col)
        return acc

    def body(t, _):
        rows = [t * ATT_UNROLL + j for j in range(ATT_UNROLL)]
        starts = [jnp.clip(r - NA_ROWS // 2, 0, GRID_H - NA_ROWS) for r in rows]
        sw = [_dot_nt(qrot_s[r], krot_s[pl.ds(pl.multiple_of(rs * GRID_W, GRID_W), win), :]) + bias_ref[r - rs]
              for r, rs in zip(rows, starts)]
        sc = [sc_s[r] for r in rows]
        m = [jnp.max(lane_fold(jnp.maximum, a, c), axis=-1, keepdims=True) for a, c in zip(sw, sc)]
        ew = [jnp.exp2(a - mm) for a, mm in zip(sw, m)]
        ec = [jnp.exp2(c - mm) for c, mm in zip(sc, m)]
        inv = [1.0 / jnp.sum(lane_fold(jnp.add, a, c), axis=-1, keepdims=True) for a, c in zip(ew, ec)]
        for r, rs, a, c, iv in zip(rows, starts, ew, ec, inv):
            kx = pl.multiple_of(CTX + rs * GRID_W, GRID_W)
            ow_s[r] = _dot(a.astype(BF16), v_s[pl.ds(kx, win), :]) * iv
            pc_s[r] = (c * iv).astype(BF16)
        return 0

    lax.fori_loop(0, GRID_H // ATT_UNROLL, body, 0)

    for g in range(GRID_H // ATT_GROUP):
        g0, g1 = g * ATT_GROUP, (g + 1) * ATT_GROUP
        oc = _dot(pc_s[g0:g1].reshape(ATT_GROUP * W2, CTX), v_s[0:CTX, :]).reshape(ATT_GROUP, W2, LANES)
        tot = oc + ow_s[g0:g1]
        for j in range(ATT_GROUP):
            row = CTX + (g0 + j) * GRID_W
            o_ref[row:row + GRID_W, :] = unstack_heads(tot[j]).astype(BF16)


def _attn(z, layer, cos_t, sin_t, biasmask):
    nhp = NA_W // LANES
    qb = 2 * LRU_W // LANES
    return pl.pallas_call(
        _attn_kernel,
        grid=(nhp, B),
        in_specs=[
            pl.BlockSpec((None, T, LANES), lambda p, b: (b, 0, qb + p)),
            pl.BlockSpec((None, T, LANES), lambda p, b: (b, 0, qb + nhp + p)),
            pl.BlockSpec((None, T, LANES), lambda p, b: (b, 0, qb + 2 * nhp + p)),
            pl.BlockSpec((SEQ, LANES), lambda p, b: (0, 0)),
            pl.BlockSpec((SEQ, LANES), lambda p, b: (0, 0)),
            pl.BlockSpec((None, None, NA_ROWS, 2 * GRID_W, NA_ROWS * GRID_W), lambda p, b: (layer, p, 0, 0, 0)),
        ],
        out_specs=pl.BlockSpec((None, T, LANES), lambda p, b: (b, 0, p)),
        out_shape=jax.ShapeDtypeStruct((B, T, NA_W), BF16),
        scratch_shapes=[
            pltpu.VMEM((GRID_H, 2 * GRID_W, LANES), BF16),
            pltpu.VMEM((GRID_H, 2 * GRID_W, LANES), BF16),
            pltpu.VMEM((2 * CTX, LANES), BF16),
            pltpu.VMEM((SEQ, LANES), BF16),
            pltpu.VMEM((CTX, LANES), BF16),
            pltpu.VMEM((T, LANES), BF16),
            pltpu.VMEM((GRID_H, 2 * GRID_W, CTX), F32),
            pltpu.VMEM((GRID_H, 2 * GRID_W, CTX), BF16),
            pltpu.VMEM((GRID_H, 2 * GRID_W, LANES), F32),
        ],
        compiler_params=_cparams(("parallel", "parallel")),
        name="attn",
    )(z, z, z, cos_t, sin_t, biasmask)


def _sgu_kernel(u_ref, v_ref, g_ref, b_ref, ws_ref, bs_ref, o_ref):
    lane = lax.broadcasted_iota(jnp.int32, (1, LANES), 1)
    first = lane < HEAD
    for n in range(TM // SGU_CHUNK):
        r0 = n * SGU_CHUNK
        r1 = r0 + SGU_CHUNK
        v = jax.nn.gelu(v_ref[r0:r1, :])
        mu = jnp.mean(v, axis=-1, keepdims=True)
        var = jnp.mean(jnp.square(v - mu), axis=-1, keepdims=True)
        vn = ((v - mu) * lax.rsqrt(var + EPS) * g_ref[...] + b_ref[...]).astype(BF16)
        for j in range(SGU_W // LANES):
            c0 = j * LANES
            c1 = c0 + LANES
            vp = vn[:, c0:c1]
            mixed = jnp.where(first, _dot(ws_ref[2 * j], vp), _dot(ws_ref[2 * j + 1], vp)) + bs_ref[:, c0:c1]
            o_ref[r0:r1, c0:c1] = (jax.nn.gelu(u_ref[r0:r1, c0:c1]) * mixed).astype(BF16)


def _sgu(z, layer, ln_g, ln_b, ws_bf, bs_full):
    ub = (2 * LRU_W + 3 * NA_W) // SGU_W
    return pl.pallas_call(
        _sgu_kernel,
        grid=(B, NT),
        in_specs=[
            pl.BlockSpec((None, TM, SGU_W), lambda b, i: (b, i, ub)),
            pl.BlockSpec((None, TM, SGU_W), lambda b, i: (b, i, ub + 1)),
            pl.BlockSpec((None, 1, SGU_W), lambda b, i: (layer, 0, 0)),
            pl.BlockSpec((None, 1, SGU_W), lambda b, i: (layer, 0, 0)),
            pl.BlockSpec((None, SGU_G, SGU_CHUNK, SGU_CHUNK), lambda b, i: (layer, 0, 0, 0)),
            pl.BlockSpec((None, SGU_CHUNK, SGU_W), lambda b, i: (layer, 0, 0)),
        ],
        out_specs=pl.BlockSpec((None, TM, SGU_W), lambda b, i: (b, i, 0)),
        out_shape=jax.ShapeDtypeStruct((B, T, SGU_W), BF16),
        compiler_params=_cparams(("parallel", "parallel")),
        name="sgu",
    )(z, z, ln_g, ln_b, ws_bf, bs_full)


def _outproj_kernel(x_ref, a_ref, b_ref, c_ref, gtb, gtc, w_ref, o_ref):
    i = pl.program_id(1)
    acc = _dot(a_ref[...], w_ref[0:LRU_W, :])
    acc = acc + _dot(b_ref[...], w_ref[LRU_W:LRU_W + NA_W, :])
    acc = acc + _dot(c_ref[...], w_ref[LRU_W + NA_W:D, :])
    is_ctx = (lax.broadcasted_iota(jnp.int32, (TM, 1), 0) + i * TM) < CTX
    o_ref[...] = x_ref[...] + jnp.where(is_ctx, gtc[...], gtb[...]) * acc


def _outproj(xall, oa, ob, oc, mod5, layer, w_all):
    tn = D
    return pl.pallas_call(
        _outproj_kernel,
        grid=(B, NT, D // tn),
        in_specs=[
            pl.BlockSpec((None, TM, tn), lambda b, i, j: (b, i, j)),
            pl.BlockSpec((None, TM, LRU_W), lambda b, i, j: (b, i, 0)),
            pl.BlockSpec((None, TM, NA_W), lambda b, i, j: (b, i, 0)),
            pl.BlockSpec((None, TM, SGU_W), lambda b, i, j: (b, i, 0)),
            _mod_spec(layer, 2, False, tn, 2), _mod_spec(layer, 2, True, tn, 2),
            pl.BlockSpec((None, D, tn), lambda b, i, j: (layer, 0, j), pipeline_mode=pl.Buffered(1)),
        ],
        out_specs=pl.BlockSpec((None, TM, tn), lambda b, i, j: (b, i, j)),
        out_shape=jax.ShapeDtypeStruct((B, T, D), F32),
        compiler_params=_cparams(("parallel", "parallel", "parallel"), VMEM_BIG),
        name="outproj",
    )(xall, oa, ob, oc, mod5, mod5, w_all)


def _ffn_kernel(x_ref, xp_ref, xn_ref, g_ref, shb, scb, gtb, shc, scc, gtc, wa_ref, wg_ref, cw_ref, cb_ref, wd_ref,
                o_ref, hx_s, u_s, act_s):
    i = pl.program_id(1)
    k = pl.program_id(2)
    rb = FFN_RB
    left_blocks = (0, CTX // rb)
    right_blocks = (CTX // rb - 1, TM // rb - 1)

    @pl.when(k == 0)
    def _():
        gain_b = g_ref[...] * (1.0 + scb[...])

        def normed(x):
            return (x * lax.rsqrt(jnp.mean(x * x, axis=-1, keepdims=True) + EPS)) * gain_b + shb[...]
        hx_s[0:HALO, :] = jnp.concatenate([normed(xn_ref[...]), normed(xp_ref[...])], axis=0).astype(BF16)
        _norm_tile(i, x_ref, hx_s, HALO, g_ref, shb, scb, shc, scc)
        o_ref[...] = jnp.zeros_like(o_ref)

    slab = TM // FFN_SPLIT
    for h in range(FFN_SPLIT):
        q0, q1 = h * slab, min((h + 1) * slab + 2 * HALO, HALO + TM)
        u_s[q0:q1, :TH] = _dot(hx_s[q0:q1, :], wa_ref[...])
        u_s[q0:q1, TH:] = _dot(hx_s[q0:q1, :], wg_ref[...])
    u_s[HALO + TM:HALO + TM + SUBLANES, :] = u_s[0:SUBLANES, :]
    cw = cw_ref[...]
    cb = cb_ref[...]
    for blk in range(TM // rb):
        r0 = HALO + blk * rb
        rows = lax.broadcasted_iota(jnp.int32, (rb, 1), 0) + (i * TM + blk * rb)
        ul = u_s[r0 - 1:r0 - 1 + rb, :]
        ur = u_s[r0 + 1:r0 + 1 + rb, :]
        if blk in left_blocks:
            ul = jnp.where(jnp.logical_and(rows != 0, rows != CTX), ul, 0.0)
        if blk in right_blocks:
            ur = jnp.where(jnp.logical_and(rows != CTX - 1, rows != T - 1), ur, 0.0)
        y = cb + ul * cw[0:1]
        y = y + u_s[r0:r0 + rb, :] * cw[1:2]
        y = y + ur * cw[2:3]
        a = y[:, :TH]
        gg = y[:, TH:]
        act_s[blk * rb:(blk + 1) * rb, :] = (gg * jax.nn.sigmoid(gg) * a).astype(BF16)
        if (blk + 1) * rb % slab == 0:
            s0 = (blk + 1) * rb - slab
            o_ref[s0:s0 + slab, :] += _dot(act_s[s0:s0 + slab, :], wd_ref[...])

    @pl.when(k == NK - 1)
    def _():
        def finish(lo, hi, gate):
            o_ref[lo:hi, :] = x_ref[lo:hi, :] + gate * o_ref[lo:hi, :]
        _per_segment(i, finish, gtb[...], gtc[...])


def _ffn(xall, mod5, layer, g, wa, wg, cw_r, cb_r, wd_p):
    hb = TM // SUBLANES
    last = T // SUBLANES - 1
    col_map = lambda b, i, k: (layer, 0, k)
    return pl.pallas_call(
        _ffn_kernel,
        grid=(B, NT, NK),
        in_specs=[
            pl.BlockSpec((None, TM, D), lambda b, i, k: (b, i, 0)),
            pl.BlockSpec((None, SUBLANES, D), lambda b, i, k: (b, jnp.maximum(i * hb - 1, 0), 0)),
            pl.BlockSpec((None, SUBLANES, D), lambda b, i, k: (b, jnp.minimum((i + 1) * hb, last), 0)),
            pl.BlockSpec((None, 1, D), lambda b, i, k: (layer, 0, 0)),
            _mod_spec(layer, 3, False), _mod_spec(layer, 4, False), _mod_spec(layer, 5, False),
            _mod_spec(layer, 3, True), _mod_spec(layer, 4, True), _mod_spec(layer, 5, True),
            pl.BlockSpec((None, D, TH), col_map),
            pl.BlockSpec((None, D, TH), col_map),
            pl.BlockSpec((None, 3, 2 * TH), col_map),
            pl.BlockSpec((None, 1, 2 * TH), col_map),
            pl.BlockSpec((None, TH, D), lambda b, i, k: (layer, k, 0)),
        ],
        out_specs=pl.BlockSpec((None, TM, D), lambda b, i, k: (b, i, 0)),
        out_shape=jax.ShapeDtypeStruct((B, T, D), F32),
        scratch_shapes=[
            pltpu.VMEM((HALO + TM, D), BF16),
            pltpu.VMEM((HALO + TM + SUBLANES, 2 * TH), F32),
            pltpu.VMEM((TM, TH), BF16),
        ],
        compiler_params=_cparams(("parallel", "parallel", "arbitrary"), VMEM_BIG),
        name="ffn",
    )(xall, xall, xall, g, mod5, mod5, mod5, mod5, mod5, mod5, wa, wg, cw_r, cb_r, wd_p)


def _final_kernel(x_ref, g_ref, o_ref):
    x = x_ref[...]
    o_ref[...] = x * lax.rsqrt(jnp.mean(x * x, axis=-1, keepdims=True) + EPS) * g_ref[...]


def _final_norm(xall, g):
    tm = CTX
    return pl.pallas_call(
        _final_kernel,
        grid=(B, SEQ // tm),
        in_specs=[
            pl.BlockSpec((None, tm, D), lambda b, i: (b, i + 1, 0)),
            pl.BlockSpec((1, D), lambda b, i: (0, 0)),
        ],
        out_specs=pl.BlockSpec((None, tm, D), lambda b, i: (b, i, 0)),
        out_shape=jax.ShapeDtypeStruct((B, SEQ, D), F32),
        compiler_params=_cparams(("parallel", "parallel")),
        name="final_norm",
    )(xall, g.reshape(1, D))


def _rope_tables():
    t = jnp.arange(SEQ)
    pos = jnp.stack([t // GRID_W, t % GRID_W], axis=-1).astype(F32)
    inv = ROPE_BASE ** (-jnp.arange(ROPE_F, dtype=F32) / ROPE_F)
    ang = pos[:, :, None] * inv
    cos, sin = jnp.cos(ang), jnp.sin(ang)
    cos_h = jnp.concatenate([cos[:, 0], cos[:, 0], cos[:, 1], cos[:, 1]], axis=-1)
    sin_h = jnp.concatenate([-sin[:, 0], sin[:, 0], -sin[:, 1], sin[:, 1]], axis=-1)
    reps = LANES // HEAD
    return jnp.tile(cos_h, (1, reps)), jnp.tile(sin_h, (1, reps))


def _bias_tables(rpb):
    nl = rpb.shape[0]
    nr, nc = 2 * NA_ROWS - 1, 2 * NA_KC - 1
    pairs = rpb.reshape(nl, NA_HEADS // 2, 2, nr, nc) * LOG2E
    pairs = jnp.pad(pairs, ((0, 0), (0, 0), (0, 0), (0, 2 * NA_ROWS - nr), (0, LANES - nc)))
    return pl.pallas_call(
        _bias_kernel,
        grid=(nl, NA_HEADS // 2),
        in_specs=[pl.BlockSpec((None, None, 2, 2 * NA_ROWS, LANES), lambda l, p: (l, p, 0, 0, 0))],
        out_specs=pl.BlockSpec((None, None, NA_ROWS, 2 * GRID_W, NA_ROWS * GRID_W), lambda l, p: (l, p, 0, 0, 0)),
        out_shape=jax.ShapeDtypeStruct((nl, NA_HEADS // 2, NA_ROWS, 2 * GRID_W, NA_ROWS * GRID_W), F32),
        compiler_params=_cparams(("parallel", "parallel")),
        name="bias_table",
    )(pairs)


def _bias_kernel(rp_ref, o_ref):
    c = lax.broadcasted_iota(jnp.int32, (GRID_W, LANES), 0)
    lane = lax.broadcasted_iota(jnp.int32, (GRID_W, LANES), 1)
    kc = lane % GRID_W
    col_start = jnp.clip(c - NA_KC // 2, 0, GRID_W - NA_KC)
    in_win = jnp.logical_and(kc >= col_start, kc < col_start + NA_KC)
    first = lane < GRID_W
    per_blk = LANES // GRID_W
    for cls in range(NA_ROWS):
        for hd in range(2):
            for blk in range(NA_ROWS // per_blk):
                parts = []
                for h in range(per_blk):
                    m = blk * per_blk + h - cls + NA_ROWS - 1
                    row = jnp.broadcast_to(rp_ref[hd, m:m + 1, :], (GRID_W, LANES))
                    shift = (LANES - (NA_KC - 1) + h * GRID_W) % LANES
                    parts.append(pltpu.roll(row, shift, 1, stride=1, stride_axis=0))
                tile = jnp.where(in_win, jnp.where(first, parts[0], parts[1]), NEG)
                o_ref[cls, hd * GRID_W:(hd + 1) * GRID_W, blk * LANES:(blk + 1) * LANES] = tile


def _gate_blockdiag(gate_w):
    nl = gate_w.shape[0]
    nh = LRU_W // LRU_HALF
    gper = LRU_HALF // HEAD
    w = gate_w.reshape(nl, 4, nh, gper, HEAD, HEAD)
    eye = jnp.eye(gper, dtype=gate_w.dtype)
    bd = jnp.einsum('lkhgio,gj->lkhgijo', w, eye).reshape(nl, 4, nh, LRU_HALF, LRU_HALF)
    return jnp.transpose(bd, (0, 2, 1, 3, 4)).astype(BF16)


def _ffn_layout(w_up, conv_w, conv_b, w_down):
    pad = FFN_HP - FFN_H
    nl = w_up.shape[0]

    def inter(m):
        r = m.shape[1]
        a = jnp.pad(m[..., :FFN_H], ((0, 0), (0, 0), (0, pad))).reshape(nl, r, NK, 1, TH)
        g = jnp.pad(m[..., FFN_H:], ((0, 0), (0, 0), (0, pad))).reshape(nl, r, NK, 1, TH)
        return jnp.concatenate([a, g], axis=3).reshape(nl, r, NK * 2 * TH)

    zc = jnp.zeros((nl, D, pad), BF16)
    wa = jnp.concatenate([w_up[..., :FFN_H].astype(BF16), zc], axis=2)
    wg = jnp.concatenate([w_up[..., FFN_H:].astype(BF16), zc], axis=2)
    wd = jnp.concatenate([w_down.astype(BF16), jnp.zeros((nl, pad, D), BF16)], axis=1)
    return wa, wg, inter(conv_w), inter(conv_b[:, None, :]), wd


def kernel(x, c, ctx, c_ctx, w_ada, b_ada, norm_mix_g, norm_ffn_g, w_in, lru_conv_w, lru_conv_b, lru_gate_w, lru_gate_b, lru_lambda, na_rpb, sgu_ln_g, sgu_ln_b, sgu_w, sgu_b, w_out, ffn_up, ffn_conv_w, ffn_conv_b, ffn_down, final_norm_g):
    xall = jnp.concatenate([ctx, x], axis=1)
    cc = jnp.concatenate([c, c_ctx[None], jnp.zeros((MOD_ROWS - B - 1, D), F32)], axis=0)
    mod5 = _adaln(cc, w_ada, b_ada).reshape(DEPTH, 6, MOD_ROWS, 1, D)
    cos_t, sin_t = _rope_tables()

    g_mix = norm_mix_g.reshape(DEPTH, 1, D)
    g_ffn = norm_ffn_g.reshape(DEPTH, 1, D)
    w_in_bf = w_in.astype(BF16)
    w_out_bf = w_out.astype(BF16)
    lru_cb = lru_conv_b.reshape(DEPTH, 1, LRU_W)
    lru_gw = _gate_blockdiag(lru_gate_w)
    lru_gb = lru_gate_b.reshape(DEPTH, 4, LRU_W)
    bias = _bias_tables(na_rpb)
    sgu_g = sgu_ln_g.reshape(DEPTH, 1, SGU_W)
    sgu_bb = sgu_ln_b.reshape(DEPTH, 1, SGU_W)
    sgu_w_bf = sgu_w.astype(BF16)
    sgu_bias = jnp.repeat(jnp.swapaxes(sgu_b, 1, 2), HEAD, axis=2)
    ffn_params = _ffn_layout(ffn_up, ffn_conv_w, ffn_conv_b, ffn_down)

    for l in range(DEPTH):
        z = _inproj(xall, mod5, l, g_mix, w_in_bf)
        oa = _lru(z, l, lru_conv_w, lru_cb, lru_gw, lru_gb, lru_lambda)
        ob = _attn(z, l, cos_t, sin_t, bias)
        oc = _sgu(z, l, sgu_g, sgu_bb, sgu_w_bf, sgu_bias)
        xall = _outproj(xall, oa, ob, oc, mod5, l, w_out_bf)
        xall = _ffn(xall, mod5, l, g_ffn, *ffn_params)
    return _final_norm(xall, final_norm_g)
```

```python
import functools

import jax
import jax.numpy as jnp
from jax import lax
from jax.experimental import pallas as pl
from jax.experimental.pallas import tpu as pltpu

F32 = jnp.float32
BF16 = jnp.bfloat16

D = 2048
B = 8
SEQ = 2048
DEPTH = 4
GRID_W = 64
GRID_H = SEQ // GRID_W
CTX = 256
T = CTX + SEQ
HEAD = 64
LRU_W = 512
LRU_TAPS = 4
LRU_C = 8.0
NA_W = 1024
NA_HEADS = NA_W // HEAD
NA_ROWS = 8
NA_KC = 16
SGU_W = 512
SGU_G = SGU_W // HEAD
SGU_CHUNK = 128
PROJ = 2 * LRU_W + 3 * NA_W + 2 * SGU_W
FFN_H = 5504
EPS = 1e-6
NEG = -1e30
LOG2E = 1.4426950408889634
ROPE_BASE = 10000.0
ROPE_F = HEAD // 4
MOD_ROWS = 16
CTX_ROW = B

LANES = 128
SUBLANES = 8
VMEM_BIG = 56 * 1024 * 1024
VMEM_MID = 40 * 1024 * 1024

TM = 768
NT = T // TM
HALO = 16
TH = 512
FFN_HP = 5632
NK = FFN_HP // TH
LRU_HALF = 256
LRU_RB = 128
ATT_RB = 256
NORM_RB = 16
NORM_UNROLL = 4
FFN_RB = 128
TM_FFN = 1152
TM_SGU = 1152
FFN_SLAB = 384
ATT_UNROLL = 4
ATT_GROUP = 4


def _cparams(sem, vmem=VMEM_MID):
    return pltpu.CompilerParams(dimension_semantics=sem, vmem_limit_bytes=vmem)


def _dot(a, b):
    return jnp.dot(a, b, preferred_element_type=F32)


def _dot_nt(a, b):
    return lax.dot_general(a, b, (((1,), (1,)), ((), ())), preferred_element_type=F32)


def _mod_spec(layer, which, ctx, width=D, col=None):
    def imap(*ids):
        b = ids[0]
        row = CTX_ROW if ctx else b
        c = 0 if col is None else ids[col]
        return (layer, which, row, 0, c)
    return pl.BlockSpec((None, None, None, 1, width), imap)


def _per_segment(i, tm, fn, latent_arg, ctx_arg):
    @pl.when(i == 0)
    def _():
        fn(0, CTX, ctx_arg)
        fn(CTX, tm, latent_arg)

    @pl.when(i != 0)
    def _():
        fn(0, tm, latent_arg)


def _norm_rows(x_ref, dst_ref, dst_off, lo, hi, gain, shift):
    def body(j, _):
        r = pl.multiple_of(lo + j * NORM_RB, NORM_RB)
        x = x_ref[pl.ds(r, NORM_RB), :]
        inv = lax.rsqrt(jnp.mean(x * x, axis=-1, keepdims=True) + EPS)
        dst_ref[pl.ds(dst_off + r, NORM_RB), :] = ((x * inv) * gain + shift).astype(BF16)
        return 0
    n = (hi - lo) // NORM_RB
    lax.fori_loop(0, n, body, 0, unroll=min(n, NORM_UNROLL))


def _norm_tile(i, x_ref, dst_ref, dst_off, g_ref, shb, scb, shc, scc):
    g = g_ref[...]

    def run(lo, hi, mod):
        _norm_rows(x_ref, dst_ref, dst_off, lo, hi, g * (1.0 + mod[0]), mod[1])
    _per_segment(i, x_ref.shape[0], run, (scb[...], shb[...]), (scc[...], shc[...]))


def _adaln_kernel(c_ref, w_ref, b_ref, o_ref):
    c = c_ref[...]
    s = (c * jax.nn.sigmoid(c)).astype(BF16)
    o_ref[...] = _dot(s, w_ref[...].astype(BF16)) + b_ref[...]


def _adaln(cc, w_ada, b_ada):
    tn = 1024
    nj = D // tn
    return pl.pallas_call(
        _adaln_kernel,
        grid=(DEPTH, 6 * nj),
        in_specs=[
            pl.BlockSpec((MOD_ROWS, D), lambda l, j: (0, 0)),
            pl.BlockSpec((None, D, tn), lambda l, j: (l, 0, j)),
            pl.BlockSpec((None, 1, tn), lambda l, j: (l, 0, j)),
        ],
        out_specs=pl.BlockSpec((None, None, MOD_ROWS, tn), lambda l, j: (l, j // nj, 0, j % nj)),
        out_shape=jax.ShapeDtypeStruct((DEPTH, 6, MOD_ROWS, D), F32),
        compiler_params=_cparams(("parallel", "parallel")),
        name="adaln",
    )(cc, w_ada, b_ada.reshape(DEPTH, 1, 6 * D))


def _inproj_kernel(x_ref, g_ref, shb, scb, shc, scc, w_ref, o_ref, hx_s):
    i = pl.program_id(1)

    @pl.when(pl.program_id(2) == 0)
    def _():
        _norm_tile(i, x_ref, hx_s, 0, g_ref, shb, scb, shc, scc)

    o_ref[...] = _dot(hx_s[...], w_ref[...])


def _inproj(xall, mod5, layer, g_all, w_all):
    tn = PROJ // 2
    return pl.pallas_call(
        _inproj_kernel,
        grid=(B, NT, PROJ // tn),
        in_specs=[
            pl.BlockSpec((None, TM, D), lambda b, i, j: (b, i, 0)),
            pl.BlockSpec((None, 1, D), lambda b, i, j: (layer, 0, 0)),
            _mod_spec(layer, 0, False), _mod_spec(layer, 1, False),
            _mod_spec(layer, 0, True), _mod_spec(layer, 1, True),
            pl.BlockSpec((None, D, tn), lambda b, i, j: (layer, 0, j)),
        ],
        out_specs=pl.BlockSpec((None, TM, tn), lambda b, i, j: (b, i, j)),
        out_shape=jax.ShapeDtypeStruct((B, T, PROJ), F32),
        scratch_shapes=[pltpu.VMEM((TM, D), BF16)],
        compiler_params=_cparams(("parallel", "parallel", "arbitrary"), VMEM_BIG),
        name="inproj",
    )(xall, g_all, mod5, mod5, mod5, mod5, w_all)


def _tile_scan(a, b, carry, rev):
    rid = lax.broadcasted_iota(jnp.int32, a.shape, 0)
    for d in (1, 2, 4):
        sh = SUBLANES - d if rev else d
        keep = (rid < SUBLANES - d) if rev else (rid >= d)
        a_s = jnp.where(keep, pltpu.roll(a, sh, 0), 1.0)
        b_s = jnp.where(keep, pltpu.roll(b, sh, 0), 0.0)
        b = a * b_s + b
        a = a * a_s
    h = a * carry + b
    return h, (h[0:1] if rev else h[SUBLANES - 1:SUBLANES])


def _lru_kernel(ax_ref, ay_ref, cw_ref, cb_ref, gw_ref, gb_ref, lam_ref, o_ref, af, bf, ab, bb):
    W = LRU_HALF
    cw = cw_ref[...]
    cb = cb_ref[...]
    gb = gb_ref[...]
    nl = -lam_ref[...]
    softplus = jnp.maximum(nl, 0.0) + jnp.log(1.0 + jnp.exp(-jnp.abs(nl)))
    coef = -LRU_C * softplus
    zeros8 = jnp.zeros((SUBLANES, W), F32)
    n_ext = LRU_RB + 2 * SUBLANES

    for blk in range(T // LRU_RB):
        r0 = blk * LRU_RB
        r1 = r0 + LRU_RB
        prev = zeros8 if r0 in (0, CTX) else ax_ref[r0 - SUBLANES:r0, :]
        nxt = zeros8 if r1 in (CTX, T) else ax_ref[r1:r1 + SUBLANES, :]
        ext = jnp.concatenate([prev, ax_ref[r0:r1, :], nxt], axis=0)
        lo, hi = SUBLANES, SUBLANES + LRU_RB
        xc = cb + pltpu.roll(ext, 2, 0)[lo:hi] * cw[0:1]
        xc = xc + pltpu.roll(ext, 1, 0)[lo:hi] * cw[1:2]
        xc = xc + ext[lo:hi] * cw[2:3]
        xc = xc + pltpu.roll(ext, n_ext - 1, 0)[lo:hi] * cw[3:4]
        xb = xc.astype(BF16)
        for d, (a_s, b_s) in enumerate(((af, bf), (ab, bb))):
            r = jax.nn.sigmoid(_dot(xb, gw_ref[2 * d]) + gb[2 * d:2 * d + 1])
            gi = jax.nn.sigmoid(_dot(xb, gw_ref[2 * d + 1]) + gb[2 * d + 1:2 * d + 2])
            log_a = coef[d:d + 1] * r
            a_s[r0:r1, :] = jnp.exp(log_a)
            th = jnp.tanh(log_a)
            b_s[r0:r1, :] = jnp.sqrt(-2.0 * th / (1.0 - th)) * (gi * xc)

    def make_body(f_base, b_top):
        def body(k, carry):
            cf, cr = carry
            rf = pl.multiple_of((f_base + k) * SUBLANES, SUBLANES)
            hf, cf = _tile_scan(af[pl.ds(rf, SUBLANES), :], bf[pl.ds(rf, SUBLANES), :], cf, False)
            bf[pl.ds(rf, SUBLANES), :] = hf
            rb = pl.multiple_of((b_top - k) * SUBLANES, SUBLANES)
            hb, cr = _tile_scan(ab[pl.ds(rb, SUBLANES), :], bb[pl.ds(rb, SUBLANES), :], cr, True)
            bb[pl.ds(rb, SUBLANES), :] = hb
            return cf, cr
        return body

    z1 = jnp.zeros((1, W), F32)
    n_ctx = CTX // SUBLANES
    n_all = T // SUBLANES
    carry = lax.fori_loop(0, n_ctx, make_body(0, n_ctx - 1), (z1, z1))
    lax.fori_loop(0, n_all - n_ctx, make_body(n_ctx, n_all - 1), carry)

    for blk in range(T // LRU_RB):
        r0 = blk * LRU_RB
        r1 = r0 + LRU_RB
        h = bf[r0:r1, :] + bb[r0:r1, :]
        o_ref[r0:r1, :] = (h * jax.nn.gelu(ay_ref[r0:r1, :])).astype(BF16)


def _lru(z, layer, cw, cb, gw_bd, gb, lam):
    W = LRU_HALF
    nh = LRU_W // W
    return pl.pallas_call(
        _lru_kernel,
        grid=(B, nh),
        in_specs=[
            pl.BlockSpec((None, T, W), lambda b, c: (b, 0, c)),
            pl.BlockSpec((None, T, W), lambda b, c: (b, 0, nh + c)),
            pl.BlockSpec((None, LRU_TAPS, W), lambda b, c: (layer, 0, c)),
            pl.BlockSpec((None, 1, W), lambda b, c: (layer, 0, c)),
            pl.BlockSpec((None, None, 4, W, W), lambda b, c: (layer, c, 0, 0, 0)),
            pl.BlockSpec((None, 4, W), lambda b, c: (layer, 0, c)),
            pl.BlockSpec((None, 2, W), lambda b, c: (layer, 0, c)),
        ],
        out_specs=pl.BlockSpec((None, T, W), lambda b, c: (b, 0, c)),
        out_shape=jax.ShapeDtypeStruct((B, T, LRU_W), BF16),
        scratch_shapes=[pltpu.VMEM((T, W), F32)] * 4,
        compiler_params=_cparams(("parallel", "parallel")),
        name="lru",
    )(z, z, cw, cb, gw_bd, gb, lam)


def _attn_kernel(q_ref, k_ref, v_ref, cos_ref, sin_ref, bias_ref, o_ref,
                 qraw_s, qrot_s, qctx_s, krot_s, kc_s, v_s, sc_s, pc_s, ow_s):
    lane = lax.broadcasted_iota(jnp.int32, (1, LANES), 1)
    head0 = lane < HEAD
    second16 = ((lane // ROPE_F) % 2) == 1
    scale = HEAD ** -0.5 * LOG2E
    W2 = 2 * GRID_W

    def rope(x, c, s):
        partner = jnp.where(second16, pltpu.roll(x, ROPE_F, 1), pltpu.roll(x, LANES - ROPE_F, 1))
        return x * c + partner * s

    def stack_heads(x):
        return jnp.concatenate([jnp.where(head0, x, 0.0), jnp.where(head0, 0.0, x)], axis=0).astype(BF16)

    def unstack_heads(y):
        n = y.shape[0] // 2
        return jnp.where(head0, y[:n], y[n:])

    qctx_s[...] = stack_heads(q_ref[0:CTX, :] * scale)
    kc_s[...] = k_ref[0:CTX, :].astype(BF16)
    v_s[0:CTX, :] = v_ref[0:CTX, :].astype(BF16)
    rows_per_blk = ATT_RB // GRID_W
    for blk in range(SEQ // ATT_RB):
        r0 = blk * ATT_RB
        r1 = r0 + ATT_RB
        c = cos_ref[r0:r1, :]
        s = sin_ref[r0:r1, :]
        q = q_ref[CTX + r0:CTX + r1, :]
        qs = q * scale
        qr = rope(q, c, s) * scale
        for j in range(rows_per_blk):
            qraw_s[blk * rows_per_blk + j] = stack_heads(qs[j * GRID_W:(j + 1) * GRID_W])
            qrot_s[blk * rows_per_blk + j] = stack_heads(qr[j * GRID_W:(j + 1) * GRID_W])
        krot_s[r0:r1, :] = rope(k_ref[CTX + r0:CTX + r1, :], c, s).astype(BF16)
        v_s[CTX + r0:CTX + r1, :] = v_ref[CTX + r0:CTX + r1, :].astype(BF16)

    s = _dot_nt(qctx_s[...], kc_s[...])
    e = jnp.exp2(s - jnp.max(s, axis=-1, keepdims=True))
    p = (e / jnp.sum(e, axis=-1, keepdims=True)).astype(BF16)
    o_ref[0:CTX, :] = unstack_heads(_dot(p, v_s[0:CTX, :])).astype(BF16)

    for g in range(GRID_H // ATT_GROUP):
        g0, g1 = g * ATT_GROUP, (g + 1) * ATT_GROUP
        sc = _dot_nt(qraw_s[g0:g1].reshape(ATT_GROUP * W2, LANES), kc_s[...])
        sc_s[g0:g1] = sc.reshape(ATT_GROUP, W2, CTX)

    win = NA_ROWS * GRID_W

    def lane_fold(op, *xs):
        cols = [x[:, c:c + LANES] for x in xs for c in range(0, x.shape[1], LANES)]
        acc = cols[0]
        for col in cols[1:]:
            acc = op(acc, col)
        return acc

    def body(t, _):
        rows = [t * ATT_UNROLL + j for j in range(ATT_UNROLL)]
        starts = [jnp.clip(r - NA_ROWS // 2, 0, GRID_H - NA_ROWS) for r in rows]
        sw = [_dot_nt(qrot_s[r], krot_s[pl.ds(pl.multiple_of(rs * GRID_W, GRID_W), win), :]) + bias_ref[r - rs]
              for r, rs in zip(rows, starts)]
        sc = [sc_s[r] for r in rows]
        m = [jnp.max(lane_fold(jnp.maximum, a, c), axis=-1, keepdims=True) for a, c in zip(sw, sc)]
        ew = [jnp.exp2(a - mm) for a, mm in zip(sw, m)]
        ec = [jnp.exp2(c - mm) for c, mm in zip(sc, m)]
        inv = [1.0 / jnp.sum(lane_fold(jnp.add, a, c), axis=-1, keepdims=True) for a, c in zip(ew, ec)]
        for r, rs, a, c, iv in zip(rows, starts, ew, ec, inv):
            kx = pl.multiple_of(CTX + rs * GRID_W, GRID_W)
            ow_s[r] = _dot(a.astype(BF16), v_s[pl.ds(kx, win), :]) * iv
            pc_s[r] = (c * iv).astype(BF16)
        return 0

    lax.fori_loop(0, GRID_H // ATT_UNROLL, body, 0)

    for g in range(GRID_H // ATT_GROUP):
        g0, g1 = g * ATT_GROUP, (g + 1) * ATT_GROUP
        oc = _dot(pc_s[g0:g1].reshape(ATT_GROUP * W2, CTX), v_s[0:CTX, :]).reshape(ATT_GROUP, W2, LANES)
        tot = oc + ow_s[g0:g1]
        for j in range(ATT_GROUP):
            row = CTX + (g0 + j) * GRID_W
            o_ref[row:row + GRID_W, :] = unstack_heads(tot[j]).astype(BF16)


def _attn(z, layer, cos_t, sin_t, biasmask):
    nhp = NA_W // LANES
    qb = 2 * LRU_W // LANES
    return pl.pallas_call(
        _attn_kernel,
        grid=(nhp, B),
        in_specs=[
            pl.BlockSpec((None, T, LANES), lambda p, b: (b, 0, qb + p)),
            pl.BlockSpec((None, T, LANES), lambda p, b: (b, 0, qb + nhp + p)),
            pl.BlockSpec((None, T, LANES), lambda p, b: (b, 0, qb + 2 * nhp + p)),
            pl.BlockSpec((SEQ, LANES), lambda p, b: (0, 0)),
            pl.BlockSpec((SEQ, LANES), lambda p, b: (0, 0)),
            pl.BlockSpec((None, None, NA_ROWS, 2 * GRID_W, NA_ROWS * GRID_W), lambda p, b: (layer, p, 0, 0, 0)),
        ],
        out_specs=pl.BlockSpec((None, T, LANES), lambda p, b: (b, 0, p)),
        out_shape=jax.ShapeDtypeStruct((B, T, NA_W), BF16),
        scratch_shapes=[
            pltpu.VMEM((GRID_H, 2 * GRID_W, LANES), BF16),
            pltpu.VMEM((GRID_H, 2 * GRID_W, LANES), BF16),
            pltpu.VMEM((2 * CTX, LANES), BF16),
            pltpu.VMEM((SEQ, LANES), BF16),
            pltpu.VMEM((CTX, LANES), BF16),
            pltpu.VMEM((T, LANES), BF16),
            pltpu.VMEM((GRID_H, 2 * GRID_W, CTX), F32),
            pltpu.VMEM((GRID_H, 2 * GRID_W, CTX), BF16),
            pltpu.VMEM((GRID_H, 2 * GRID_W, LANES), F32),
        ],
        compiler_params=_cparams(("parallel", "parallel")),
        name="attn",
    )(z, z, z, cos_t, sin_t, biasmask)


def _sgu_kernel(u_ref, v_ref, g_ref, b_ref, ws_ref, bs_ref, o_ref):
    lane = lax.broadcasted_iota(jnp.int32, (1, LANES), 1)
    first = lane < HEAD
    for n in range(u_ref.shape[0] // SGU_CHUNK):
        r0 = n * SGU_CHUNK
        r1 = r0 + SGU_CHUNK
        v = jax.nn.gelu(v_ref[r0:r1, :])
        mu = jnp.mean(v, axis=-1, keepdims=True)
        var = jnp.mean(jnp.square(v - mu), axis=-1, keepdims=True)
        vn = ((v - mu) * lax.rsqrt(var + EPS) * g_ref[...] + b_ref[...]).astype(BF16)
        for j in range(SGU_W // LANES):
            c0 = j * LANES
            c1 = c0 + LANES
            vp = vn[:, c0:c1]
            mixed = jnp.where(first, _dot(ws_ref[2 * j], vp), _dot(ws_ref[2 * j + 1], vp)) + bs_ref[:, c0:c1]
            o_ref[r0:r1, c0:c1] = (jax.nn.gelu(u_ref[r0:r1, c0:c1]) * mixed).astype(BF16)


def _sgu(z, layer, ln_g, ln_b, ws_bf, bs_full):
    ub = (2 * LRU_W + 3 * NA_W) // SGU_W
    return pl.pallas_call(
        _sgu_kernel,
        grid=(B, T // TM_SGU),
        in_specs=[
            pl.BlockSpec((None, TM_SGU, SGU_W), lambda b, i: (b, i, ub)),
            pl.BlockSpec((None, TM_SGU, SGU_W), lambda b, i: (b, i, ub + 1)),
            pl.BlockSpec((None, 1, SGU_W), lambda b, i: (layer, 0, 0)),
            pl.BlockSpec((None, 1, SGU_W), lambda b, i: (layer, 0, 0)),
            pl.BlockSpec((None, SGU_G, SGU_CHUNK, SGU_CHUNK), lambda b, i: (layer, 0, 0, 0)),
            pl.BlockSpec((None, SGU_CHUNK, SGU_W), lambda b, i: (layer, 0, 0)),
        ],
        out_specs=pl.BlockSpec((None, TM_SGU, SGU_W), lambda b, i: (b, i, 0)),
        out_shape=jax.ShapeDtypeStruct((B, T, SGU_W), BF16),
        compiler_params=_cparams(("parallel", "parallel")),
        name="sgu",
    )(z, z, ln_g, ln_b, ws_bf, bs_full)


def _outproj_kernel(x_ref, a_ref, b_ref, c_ref, gtb, gtc, w_ref, o_ref):
    i = pl.program_id(1)
    acc = _dot(a_ref[...], w_ref[0:LRU_W, :])
    acc = acc + _dot(b_ref[...], w_ref[LRU_W:LRU_W + NA_W, :])
    acc = acc + _dot(c_ref[...], w_ref[LRU_W + NA_W:D, :])
    is_ctx = (lax.broadcasted_iota(jnp.int32, (TM, 1), 0) + i * TM) < CTX
    o_ref[...] = x_ref[...] + jnp.where(is_ctx, gtc[...], gtb[...]) * acc


def _outproj(xall, oa, ob, oc, mod5, layer, w_all):
    tn = D
    return pl.pallas_call(
        _outproj_kernel,
        grid=(B, NT, D // tn),
        in_specs=[
            pl.BlockSpec((None, TM, tn), lambda b, i, j: (b, i, j)),
            pl.BlockSpec((None, TM, LRU_W), lambda b, i, j: (b, i, 0)),
            pl.BlockSpec((None, TM, NA_W), lambda b, i, j: (b, i, 0)),
            pl.BlockSpec((None, TM, SGU_W), lambda b, i, j: (b, i, 0)),
            _mod_spec(layer, 2, False, tn, 2), _mod_spec(layer, 2, True, tn, 2),
            pl.BlockSpec((None, D, tn), lambda b, i, j: (layer, 0, j), pipeline_mode=pl.Buffered(1)),
        ],
        out_specs=pl.BlockSpec((None, TM, tn), lambda b, i, j: (b, i, j)),
        out_shape=jax.ShapeDtypeStruct((B, T, D), F32),
        compiler_params=_cparams(("parallel", "parallel", "parallel"), VMEM_BIG),
        name="outproj",
    )(xall, oa, ob, oc, mod5, mod5, w_all)


def _ffn_kernel(x_ref, xp_ref, xn_ref, g_ref, shb, scb, gtb, shc, scc, gtc, wa_ref, wg_ref, cw_ref, cb_ref, wd_ref,
                o_ref, hx_s, u_s, act_s):
    i = pl.program_id(1)
    k = pl.program_id(2)
    tm = x_ref.shape[0]
    rb = FFN_RB
    left_blocks = (0, CTX // rb)
    right_blocks = (CTX // rb - 1, tm // rb - 1)

    @pl.when(k == 0)
    def _():
        gain_b = g_ref[...] * (1.0 + scb[...])

        def normed(x):
            return (x * lax.rsqrt(jnp.mean(x * x, axis=-1, keepdims=True) + EPS)) * gain_b + shb[...]
        hx_s[0:HALO, :] = jnp.concatenate([normed(xn_ref[...]), normed(xp_ref[...])], axis=0).astype(BF16)
        _norm_tile(i, x_ref, hx_s, HALO, g_ref, shb, scb, shc, scc)
        o_ref[...] = jnp.zeros_like(o_ref)

    slab = FFN_SLAB
    for h in range(tm // slab):
        q0, q1 = h * slab, min((h + 1) * slab + 2 * HALO, HALO + tm)
        u_s[q0:q1, :TH] = _dot(hx_s[q0:q1, :], wa_ref[...])
        u_s[q0:q1, TH:] = _dot(hx_s[q0:q1, :], wg_ref[...])
    u_s[HALO + tm:HALO + tm + SUBLANES, :] = u_s[0:SUBLANES, :]
    cw = cw_ref[...]
    cb = cb_ref[...]
    for blk in range(tm // rb):
        r0 = HALO + blk * rb
        rows = lax.broadcasted_iota(jnp.int32, (rb, 1), 0) + (i * tm + blk * rb)
        ul = u_s[r0 - 1:r0 - 1 + rb, :]
        ur = u_s[r0 + 1:r0 + 1 + rb, :]
        if blk in left_blocks:
            ul = jnp.where(jnp.logical_and(rows != 0, rows != CTX), ul, 0.0)
        if blk in right_blocks:
            ur = jnp.where(jnp.logical_and(rows != CTX - 1, rows != T - 1), ur, 0.0)
        y = cb + ul * cw[0:1]
        y = y + u_s[r0:r0 + rb, :] * cw[1:2]
        y = y + ur * cw[2:3]
        a = y[:, :TH]
        gg = y[:, TH:]
        act_s[blk * rb:(blk + 1) * rb, :] = (gg * jax.nn.sigmoid(gg) * a).astype(BF16)
        if (blk + 1) * rb % slab == 0:
            s0 = (blk + 1) * rb - slab
            o_ref[s0:s0 + slab, :] += _dot(act_s[s0:s0 + slab, :], wd_ref[...])

    @pl.when(k == NK - 1)
    def _():
        def finish(lo, hi, gate):
            o_ref[lo:hi, :] = x_ref[lo:hi, :] + gate * o_ref[lo:hi, :]
        _per_segment(i, tm, finish, gtb[...], gtc[...])


def _ffn(xall, mod5, layer, g, wa, wg, cw_r, cb_r, wd_p):
    tm = TM_FFN
    hb = tm // SUBLANES
    last = T // SUBLANES - 1
    col_map = lambda b, i, k: (layer, 0, k)
    return pl.pallas_call(
        _ffn_kernel,
        grid=(B, T // tm, NK),
        in_specs=[
            pl.BlockSpec((None, tm, D), lambda b, i, k: (b, i, 0), pipeline_mode=pl.Buffered(1)),
            pl.BlockSpec((None, SUBLANES, D), lambda b, i, k: (b, jnp.maximum(i * hb - 1, 0), 0)),
            pl.BlockSpec((None, SUBLANES, D), lambda b, i, k: (b, jnp.minimum((i + 1) * hb, last), 0)),
            pl.BlockSpec((None, 1, D), lambda b, i, k: (layer, 0, 0)),
            _mod_spec(layer, 3, False), _mod_spec(layer, 4, False), _mod_spec(layer, 5, False),
            _mod_spec(layer, 3, True), _mod_spec(layer, 4, True), _mod_spec(layer, 5, True),
            pl.BlockSpec((None, D, TH), col_map),
            pl.BlockSpec((None, D, TH), col_map),
            pl.BlockSpec((None, 3, 2 * TH), col_map),
            pl.BlockSpec((None, 1, 2 * TH), col_map),
            pl.BlockSpec((None, TH, D), lambda b, i, k: (layer, k, 0)),
        ],
        out_specs=pl.BlockSpec((None, tm, D), lambda b, i, k: (b, i, 0)),
        out_shape=jax.ShapeDtypeStruct((B, T, D), F32),
        scratch_shapes=[
            pltpu.VMEM((HALO + tm, D), BF16),
            pltpu.VMEM((HALO + tm + SUBLANES, 2 * TH), F32),
            pltpu.VMEM((tm, TH), BF16),
        ],
        compiler_params=_cparams(("parallel", "parallel", "arbitrary"), VMEM_BIG),
        name="ffn",
    )(xall, xall, xall, g, mod5, mod5, mod5, mod5, mod5, mod5, wa, wg, cw_r, cb_r, wd_p)


def _final_kernel(x_ref, g_ref, o_ref):
    x = x_ref[...]
    o_ref[...] = x * lax.rsqrt(jnp.mean(x * x, axis=-1, keepdims=True) + EPS) * g_ref[...]


def _final_norm(xall, g):
    tm = CTX
    return pl.pallas_call(
        _final_kernel,
        grid=(B, SEQ // tm),
        in_specs=[
            pl.BlockSpec((None, tm, D), lambda b, i: (b, i + 1, 0)),
            pl.BlockSpec((1, D), lambda b, i: (0, 0)),
        ],
        out_specs=pl.BlockSpec((None, tm, D), lambda b, i: (b, i, 0)),
        out_shape=jax.ShapeDtypeStruct((B, SEQ, D), F32),
        compiler_params=_cparams(("parallel", "parallel")),
        name="final_norm",
    )(xall, g.reshape(1, D))


def _rope_tables():
    t = jnp.arange(SEQ)
    pos = jnp.stack([t // GRID_W, t % GRID_W], axis=-1).astype(F32)
    inv = ROPE_BASE ** (-jnp.arange(ROPE_F, dtype=F32) / ROPE_F)
    ang = pos[:, :, None] * inv
    cos, sin = jnp.cos(ang), jnp.sin(ang)
    cos_h = jnp.concatenate([cos[:, 0], cos[:, 0], cos[:, 1], cos[:, 1]], axis=-1)
    sin_h = jnp.concatenate([-sin[:, 0], sin[:, 0], -sin[:, 1], sin[:, 1]], axis=-1)
    reps = LANES // HEAD
    return jnp.tile(cos_h, (1, reps)), jnp.tile(sin_h, (1, reps))


def _bias_tables(rpb):
    nl = rpb.shape[0]
    nr, nc = 2 * NA_ROWS - 1, 2 * NA_KC - 1
    pairs = rpb.reshape(nl, NA_HEADS // 2, 2, nr, nc) * LOG2E
    pairs = jnp.pad(pairs, ((0, 0), (0, 0), (0, 0), (0, 2 * NA_ROWS - nr), (0, LANES - nc)))
    return pl.pallas_call(
        _bias_kernel,
        grid=(nl, NA_HEADS // 2),
        in_specs=[pl.BlockSpec((None, None, 2, 2 * NA_ROWS, LANES), lambda l, p: (l, p, 0, 0, 0))],
        out_specs=pl.BlockSpec((None, None, NA_ROWS, 2 * GRID_W, NA_ROWS * GRID_W), lambda l, p: (l, p, 0, 0, 0)),
        out_shape=jax.ShapeDtypeStruct((nl, NA_HEADS // 2, NA_ROWS, 2 * GRID_W, NA_ROWS * GRID_W), F32),
        compiler_params=_cparams(("parallel", "parallel")),
        name="bias_table",
    )(pairs)


def _bias_kernel(rp_ref, o_ref):
    c = lax.broadcasted_iota(jnp.int32, (GRID_W, LANES), 0)
    lane = lax.broadcasted_iota(jnp.int32, (GRID_W, LANES), 1)
    kc = lane % GRID_W
    col_start = jnp.clip(c - NA_KC // 2, 0, GRID_W - NA_KC)
    in_win = jnp.logical_and(kc >= col_start, kc < col_start + NA_KC)
    first = lane < GRID_W
    per_blk = LANES // GRID_W
    for cls in range(NA_ROWS):
        for hd in range(2):
            for blk in range(NA_ROWS // per_blk):
                parts = []
                for h in range(per_blk):
                    m = blk * per_blk + h - cls + NA_ROWS - 1
                    row = jnp.broadcast_to(rp_ref[hd, m:m + 1, :], (GRID_W, LANES))
                    shift = (LANES - (NA_KC - 1) + h * GRID_W) % LANES
                    parts.append(pltpu.roll(row, shift, 1, stride=1, stride_axis=0))
                tile = jnp.where(in_win, jnp.where(first, parts[0], parts[1]), NEG)
                o_ref[cls, hd * GRID_W:(hd + 1) * GRID_W, blk * LANES:(blk + 1) * LANES] = tile


def _gate_blockdiag(gate_w):
    nl = gate_w.shape[0]
    nh = LRU_W // LRU_HALF
    gper = LRU_HALF // HEAD
    w = gate_w.reshape(nl, 4, nh, gper, HEAD, HEAD)
    eye = jnp.eye(gper, dtype=gate_w.dtype)
    bd = jnp.einsum('lkhgio,gj->lkhgijo', w, eye).reshape(nl, 4, nh, LRU_HALF, LRU_HALF)
    return jnp.transpose(bd, (0, 2, 1, 3, 4)).astype(BF16)


def _ffn_layout(w_up, conv_w, conv_b, w_down):
    pad = FFN_HP - FFN_H
    nl = w_up.shape[0]

    def inter(m):
        r = m.shape[1]
        a = jnp.pad(m[..., :FFN_H], ((0, 0), (0, 0), (0, pad))).reshape(nl, r, NK, 1, TH)
        g = jnp.pad(m[..., FFN_H:], ((0, 0), (0, 0), (0, pad))).reshape(nl, r, NK, 1, TH)
        return jnp.concatenate([a, g], axis=3).reshape(nl, r, NK * 2 * TH)

    zc = jnp.zeros((nl, D, pad), BF16)
    wa = jnp.concatenate([w_up[..., :FFN_H].astype(BF16), zc], axis=2)
    wg = jnp.concatenate([w_up[..., FFN_H:].astype(BF16), zc], axis=2)
    wd = jnp.concatenate([w_down.astype(BF16), jnp.zeros((nl, pad, D), BF16)], axis=1)
    return wa, wg, inter(conv_w), inter(conv_b[:, None, :]), wd


def kernel(x, c, ctx, c_ctx, w_ada, b_ada, norm_mix_g, norm_ffn_g, w_in, lru_conv_w, lru_conv_b, lru_gate_w, lru_gate_b, lru_lambda, na_rpb, sgu_ln_g, sgu_ln_b, sgu_w, sgu_b, w_out, ffn_up, ffn_conv_w, ffn_conv_b, ffn_down, final_norm_g):
    xall = jnp.concatenate([ctx, x], axis=1)
    cc = jnp.concatenate([c, c_ctx[None], jnp.zeros((MOD_ROWS - B - 1, D), F32)], axis=0)
    mod5 = _adaln(cc, w_ada, b_ada).reshape(DEPTH, 6, MOD_ROWS, 1, D)
    cos_t, sin_t = _rope_tables()

    g_mix = norm_mix_g.reshape(DEPTH, 1, D)
    g_ffn = norm_ffn_g.reshape(DEPTH, 1, D)
    w_in_bf = w_in.astype(BF16)
    w_out_bf = w_out.astype(BF16)
    lru_cb = lru_conv_b.reshape(DEPTH, 1, LRU_W)
    lru_gw = _gate_blockdiag(lru_gate_w)
    lru_gb = lru_gate_b.reshape(DEPTH, 4, LRU_W)
    bias = _bias_tables(na_rpb)
    sgu_g = sgu_ln_g.reshape(DEPTH, 1, SGU_W)
    sgu_bb = sgu_ln_b.reshape(DEPTH, 1, SGU_W)
    sgu_w_bf = sgu_w.astype(BF16)
    sgu_bias = jnp.repeat(jnp.swapaxes(sgu_b, 1, 2), HEAD, axis=2)
    ffn_params = _ffn_layout(ffn_up, ffn_conv_w, ffn_conv_b, ffn_down)

    for l in range(DEPTH):
        z = _inproj(xall, mod5, l, g_mix, w_in_bf)
        oa = _lru(z, l, lru_conv_w, lru_cb, lru_gw, lru_gb, lru_lambda)
        ob = _attn(z, l, cos_t, sin_t, bias)
        oc = _sgu(z, l, sgu_g, sgu_bb, sgu_w_bf, sgu_bias)
        xall = _outproj(xall, oa, ob, oc, mod5, l, w_out_bf)
        xall = _ffn(xall, mod5, l, g_ffn, *ffn_params)
    return _final_norm(xall, final_norm_g)
```

```python
import functools

import jax
import jax.numpy as jnp
from jax import lax
from jax.experimental import pallas as pl
from jax.experimental.pallas import tpu as pltpu

F32 = jnp.float32
BF16 = jnp.bfloat16

D = 2048
B = 8
SEQ = 2048
DEPTH = 4
GRID_W = 64
GRID_H = SEQ // GRID_W
CTX = 256
T = CTX + SEQ
HEAD = 64
LRU_W = 512
LRU_TAPS = 4
LRU_C = 8.0
NA_W = 1024
NA_HEADS = NA_W // HEAD
NA_ROWS = 8
NA_KC = 16
SGU_W = 512
SGU_G = SGU_W // HEAD
SGU_CHUNK = 128
PROJ = 2 * LRU_W + 3 * NA_W + 2 * SGU_W
FFN_H = 5504
EPS = 1e-6
NEG = -1e30
LOG2E = 1.4426950408889634
ROPE_BASE = 10000.0
ROPE_F = HEAD // 4
MOD_ROWS = 16
CTX_ROW = B

LANES = 128
SUBLANES = 8
VMEM_BIG = 56 * 1024 * 1024
VMEM_MID = 40 * 1024 * 1024

TM = 768
NT = T // TM
HALO = 16
TH = 512
FFN_HP = 5632
NK = FFN_HP // TH
LRU_HALF = 256
LRU_RB = 128
ATT_RB = 256
NORM_RB = 16
NORM_UNROLL = 4
FFN_RB = 128
TM_FFN = 768
TM_SGU = 1152
FFN_SLAB = 384
ATT_UNROLL = 4
ATT_GROUP = 16


def _cparams(sem, vmem=VMEM_MID):
    return pltpu.CompilerParams(dimension_semantics=sem, vmem_limit_bytes=vmem)


def _dot(a, b):
    return jnp.dot(a, b, preferred_element_type=F32)


def _dot_nt(a, b):
    return lax.dot_general(a, b, (((1,), (1,)), ((), ())), preferred_element_type=F32)


def _mod_spec(layer, which, ctx, width=D, col=None):
    def imap(*ids):
        b = ids[0]
        row = CTX_ROW if ctx else b
        c = 0 if col is None else ids[col]
        return (layer, which, row, 0, c)
    return pl.BlockSpec((None, None, None, 1, width), imap)


def _per_segment(i, tm, fn, latent_arg, ctx_arg):
    @pl.when(i == 0)
    def _():
        fn(0, CTX, ctx_arg)
        fn(CTX, tm, latent_arg)

    @pl.when(i != 0)
    def _():
        fn(0, tm, latent_arg)


def _norm_rows(x_ref, dst_ref, dst_off, lo, hi, gain, shift):
    def body(j, _):
        r = pl.multiple_of(lo + j * NORM_RB, NORM_RB)
        x = x_ref[pl.ds(r, NORM_RB), :]
        inv = lax.rsqrt(jnp.mean(x * x, axis=-1, keepdims=True) + EPS)
        dst_ref[pl.ds(dst_off + r, NORM_RB), :] = ((x * inv) * gain + shift).astype(BF16)
        return 0
    n = (hi - lo) // NORM_RB
    lax.fori_loop(0, n, body, 0, unroll=min(n, NORM_UNROLL))


def _norm_tile(i, x_ref, dst_ref, dst_off, g_ref, shb, scb, shc, scc):
    g = g_ref[...]

    def run(lo, hi, mod):
        _norm_rows(x_ref, dst_ref, dst_off, lo, hi, g * (1.0 + mod[0]), mod[1])
    _per_segment(i, x_ref.shape[0], run, (scb[...], shb[...]), (scc[...], shc[...]))


def _adaln_kernel(c_ref, w_ref, b_ref, o_ref):
    c = c_ref[...]
    s = (c * jax.nn.sigmoid(c)).astype(BF16)
    o_ref[...] = _dot(s, w_ref[...].astype(BF16)) + b_ref[...]


def _adaln(cc, w_ada, b_ada):
    tn = 1024
    nj = D // tn
    return pl.pallas_call(
        _adaln_kernel,
        grid=(DEPTH, 6 * nj),
        in_specs=[
            pl.BlockSpec((MOD_ROWS, D), lambda l, j: (0, 0)),
            pl.BlockSpec((None, D, tn), lambda l, j: (l, 0, j)),
            pl.BlockSpec((None, 1, tn), lambda l, j: (l, 0, j)),
        ],
        out_specs=pl.BlockSpec((None, None, MOD_ROWS, tn), lambda l, j: (l, j // nj, 0, j % nj)),
        out_shape=jax.ShapeDtypeStruct((DEPTH, 6, MOD_ROWS, D), F32),
        compiler_params=_cparams(("parallel", "parallel")),
        name="adaln",
    )(cc, w_ada, b_ada.reshape(DEPTH, 1, 6 * D))


def _inproj_kernel(x_ref, g_ref, shb, scb, shc, scc, w_ref, o_ref, hx_s):
    i = pl.program_id(1)

    @pl.when(pl.program_id(2) == 0)
    def _():
        _norm_tile(i, x_ref, hx_s, 0, g_ref, shb, scb, shc, scc)

    o_ref[...] = _dot(hx_s[...], w_ref[...])


def _inproj(xall, mod5, layer, g_all, w_all):
    tn = PROJ // 2
    return pl.pallas_call(
        _inproj_kernel,
        grid=(B, NT, PROJ // tn),
        in_specs=[
            pl.BlockSpec((None, TM, D), lambda b, i, j: (b, i, 0)),
            pl.BlockSpec((None, 1, D), lambda b, i, j: (layer, 0, 0)),
            _mod_spec(layer, 0, False), _mod_spec(layer, 1, False),
            _mod_spec(layer, 0, True), _mod_spec(layer, 1, True),
            pl.BlockSpec((None, D, tn), lambda b, i, j: (layer, 0, j)),
        ],
        out_specs=pl.BlockSpec((None, TM, tn), lambda b, i, j: (b, i, j)),
        out_shape=jax.ShapeDtypeStruct((B, T, PROJ), F32),
        scratch_shapes=[pltpu.VMEM((TM, D), BF16)],
        compiler_params=_cparams(("parallel", "parallel", "arbitrary"), VMEM_BIG),
        name="inproj",
    )(xall, g_all, mod5, mod5, mod5, mod5, w_all)


def _tile_scan(a, b, carry, rev):
    rid = lax.broadcasted_iota(jnp.int32, a.shape, 0)
    for d in (1, 2, 4):
        sh = SUBLANES - d if rev else d
        keep = (rid < SUBLANES - d) if rev else (rid >= d)
        a_s = jnp.where(keep, pltpu.roll(a, sh, 0), 1.0)
        b_s = jnp.where(keep, pltpu.roll(b, sh, 0), 0.0)
        b = a * b_s + b
        a = a * a_s
    h = a * carry + b
    return h, (h[0:1] if rev else h[SUBLANES - 1:SUBLANES])


def _lru_kernel(ax_ref, ay_ref, cw_ref, cb_ref, gw_ref, gb_ref, lam_ref, o_ref, af, bf, ab, bb):
    W = LRU_HALF
    cw = cw_ref[...]
    cb = cb_ref[...]
    gb = gb_ref[...]
    nl = -lam_ref[...]
    softplus = jnp.maximum(nl, 0.0) + jnp.log(1.0 + jnp.exp(-jnp.abs(nl)))
    coef = -LRU_C * softplus
    zeros8 = jnp.zeros((SUBLANES, W), F32)
    n_ext = LRU_RB + 2 * SUBLANES

    for blk in range(T // LRU_RB):
        r0 = blk * LRU_RB
        r1 = r0 + LRU_RB
        prev = zeros8 if r0 in (0, CTX) else ax_ref[r0 - SUBLANES:r0, :]
        nxt = zeros8 if r1 in (CTX, T) else ax_ref[r1:r1 + SUBLANES, :]
        ext = jnp.concatenate([prev, ax_ref[r0:r1, :], nxt], axis=0)
        lo, hi = SUBLANES, SUBLANES + LRU_RB
        xc = cb + pltpu.roll(ext, 2, 0)[lo:hi] * cw[0:1]
        xc = xc + pltpu.roll(ext, 1, 0)[lo:hi] * cw[1:2]
        xc = xc + ext[lo:hi] * cw[2:3]
        xc = xc + pltpu.roll(ext, n_ext - 1, 0)[lo:hi] * cw[3:4]
        xb = xc.astype(BF16)
        for d, (a_s, b_s) in enumerate(((af, bf), (ab, bb))):
            r = jax.nn.sigmoid(_dot(xb, gw_ref[2 * d]) + gb[2 * d:2 * d + 1])
            gi = jax.nn.sigmoid(_dot(xb, gw_ref[2 * d + 1]) + gb[2 * d + 1:2 * d + 2])
            log_a = coef[d:d + 1] * r
            a_s[r0:r1, :] = jnp.exp(log_a)
            th = jnp.tanh(log_a)
            b_s[r0:r1, :] = jnp.sqrt(-2.0 * th / (1.0 - th)) * (gi * xc)

    def make_body(f_base, b_top):
        def body(k, carry):
            cf, cr = carry
            rf = pl.multiple_of((f_base + k) * SUBLANES, SUBLANES)
            hf, cf = _tile_scan(af[pl.ds(rf, SUBLANES), :], bf[pl.ds(rf, SUBLANES), :], cf, False)
            bf[pl.ds(rf, SUBLANES), :] = hf
            rb = pl.multiple_of((b_top - k) * SUBLANES, SUBLANES)
            hb, cr = _tile_scan(ab[pl.ds(rb, SUBLANES), :], bb[pl.ds(rb, SUBLANES), :], cr, True)
            bb[pl.ds(rb, SUBLANES), :] = hb
            return cf, cr
        return body

    z1 = jnp.zeros((1, W), F32)
    n_ctx = CTX // SUBLANES
    n_all = T // SUBLANES
    carry = lax.fori_loop(0, n_ctx, make_body(0, n_ctx - 1), (z1, z1))
    lax.fori_loop(0, n_all - n_ctx, make_body(n_ctx, n_all - 1), carry)

    for blk in range(T // LRU_RB):
        r0 = blk * LRU_RB
        r1 = r0 + LRU_RB
        h = bf[r0:r1, :] + bb[r0:r1, :]
        o_ref[r0:r1, :] = (h * jax.nn.gelu(ay_ref[r0:r1, :])).astype(BF16)


def _lru(z, layer, cw, cb, gw_bd, gb, lam):
    W = LRU_HALF
    nh = LRU_W // W
    return pl.pallas_call(
        _lru_kernel,
        grid=(B, nh),
        in_specs=[
            pl.BlockSpec((None, T, W), lambda b, c: (b, 0, c)),
            pl.BlockSpec((None, T, W), lambda b, c: (b, 0, nh + c)),
            pl.BlockSpec((None, LRU_TAPS, W), lambda b, c: (layer, 0, c)),
            pl.BlockSpec((None, 1, W), lambda b, c: (layer, 0, c)),
            pl.BlockSpec((None, None, 4, W, W), lambda b, c: (layer, c, 0, 0, 0)),
            pl.BlockSpec((None, 4, W), lambda b, c: (layer, 0, c)),
            pl.BlockSpec((None, 2, W), lambda b, c: (layer, 0, c)),
        ],
        out_specs=pl.BlockSpec((None, T, W), lambda b, c: (b, 0, c)),
        out_shape=jax.ShapeDtypeStruct((B, T, LRU_W), BF16),
        scratch_shapes=[pltpu.VMEM((T, W), F32)] * 4,
        compiler_params=_cparams(("parallel", "parallel")),
        name="lru",
    )(z, z, cw, cb, gw_bd, gb, lam)


def _attn_kernel(q_ref, k_ref, v_ref, cos_ref, sin_ref, bias_ref, o_ref,
                 qraw_s, qrot_s, qctx_s, krot_s, kc_s, v_s, sc_s, pc_s, ow_s):
    lane = lax.broadcasted_iota(jnp.int32, (1, LANES), 1)
    head0 = lane < HEAD
    second16 = ((lane // ROPE_F) % 2) == 1
    scale = HEAD ** -0.5 * LOG2E
    W2 = 2 * GRID_W

    def rope(x, c, s):
        partner = jnp.where(second16, pltpu.roll(x, ROPE_F, 1), pltpu.roll(x, LANES - ROPE_F, 1))
        return x * c + partner * s

    def stack_heads(x):
        return jnp.concatenate([jnp.where(head0, x, 0.0), jnp.where(head0, 0.0, x)], axis=0).astype(BF16)

    def unstack_heads(y):
        n = y.shape[0] // 2
        return jnp.where(head0, y[:n], y[n:])

    qctx_s[...] = stack_heads(q_ref[0:CTX, :] * scale)
    kc_s[...] = k_ref[0:CTX, :].astype(BF16)
    v_s[0:CTX, :] = v_ref[0:CTX, :].astype(BF16)
    rows_per_blk = ATT_RB // GRID_W
    for blk in range(SEQ // ATT_RB):
        r0 = blk * ATT_RB
        r1 = r0 + ATT_RB
        c = cos_ref[r0:r1, :]
        s = sin_ref[r0:r1, :]
        q = q_ref[CTX + r0:CTX + r1, :]
        qs = q * scale
        qr = rope(q, c, s) * scale
        for j in range(rows_per_blk):
            qraw_s[blk * rows_per_blk + j] = stack_heads(qs[j * GRID_W:(j + 1) * GRID_W])
            qrot_s[blk * rows_per_blk + j] = stack_heads(qr[j * GRID_W:(j + 1) * GRID_W])
        krot_s[r0:r1, :] = rope(k_ref[CTX + r0:CTX + r1, :], c, s).astype(BF16)
        v_s[CTX + r0:CTX + r1, :] = v_ref[CTX + r0:CTX + r1, :].astype(BF16)

    s = _dot_nt(qctx_s[...], kc_s[...])
    e = jnp.exp2(s - jnp.max(s, axis=-1, keepdims=True))
    p = (e / jnp.sum(e, axis=-1, keepdims=True)).astype(BF16)
    o_ref[0:CTX, :] = unstack_heads(_dot(p, v_s[0:CTX, :])).astype(BF16)

    for g in range(GRID_H // ATT_GROUP):
        g0, g1 = g * ATT_GROUP, (g + 1) * ATT_GROUP
        sc = _dot_nt(qraw_s[g0:g1].reshape(ATT_GROUP * W2, LANES), kc_s[...])
        sc_s[g0:g1] = sc.reshape(ATT_GROUP, W2, CTX)

    win = NA_ROWS * GRID_W

    def lane_fold(op, *xs):
        cols = [x[:, c:c + LANES] for x in xs for c in range(0, x.shape[1], LANES)]
        acc = cols[0]
        for col in cols[1:]:
            acc = op(acc, col)
        return acc

    def body(t, _):
        rows = [t * ATT_UNROLL + j for j in range(ATT_UNROLL)]
        starts = [jnp.clip(r - NA_ROWS // 2, 0, GRID_H - NA_ROWS) for r in rows]
        sw = [_dot_nt(qrot_s[r], krot_s[pl.ds(pl.multiple_of(rs * GRID_W, GRID_W), win), :]) + bias_ref[r - rs]
              for r, rs in zip(rows, starts)]
        sc = [sc_s[r] for r in rows]
        m = [jnp.max(lane_fold(jnp.maximum, a, c), axis=-1, keepdims=True) for a, c in zip(sw, sc)]
        ew = [jnp.exp2(a - mm) for a, mm in zip(sw, m)]
        ec = [jnp.exp2(c - mm) for c, mm in zip(sc, m)]
        inv = [1.0 / jnp.sum(lane_fold(jnp.add, a, c), axis=-1, keepdims=True) for a, c in zip(ew, ec)]
        for r, rs, a, c, iv in zip(rows, starts, ew, ec, inv):
            kx = pl.multiple_of(CTX + rs * GRID_W, GRID_W)
            ow_s[r] = _dot(a.astype(BF16), v_s[pl.ds(kx, win), :]) * iv
            pc_s[r] = (c * iv).astype(BF16)
        return 0

    lax.fori_loop(0, GRID_H // ATT_UNROLL, body, 0)

    for g in range(GRID_H // ATT_GROUP):
        g0, g1 = g * ATT_GROUP, (g + 1) * ATT_GROUP
        oc = _dot(pc_s[g0:g1].reshape(ATT_GROUP * W2, CTX), v_s[0:CTX, :]).reshape(ATT_GROUP, W2, LANES)
        tot = oc + ow_s[g0:g1]
        for j in range(ATT_GROUP):
            row = CTX + (g0 + j) * GRID_W
            o_ref[row:row + GRID_W, :] = unstack_heads(tot[j]).astype(BF16)


def _attn(z, layer, cos_t, sin_t, biasmask):
    nhp = NA_W // LANES
    qb = 2 * LRU_W // LANES
    return pl.pallas_call(
        _attn_kernel,
        grid=(nhp, B),
        in_specs=[
            pl.BlockSpec((None, T, LANES), lambda p, b: (b, 0, qb + p)),
            pl.BlockSpec((None, T, LANES), lambda p, b: (b, 0, qb + nhp + p)),
            pl.BlockSpec((None, T, LANES), lambda p, b: (b, 0, qb + 2 * nhp + p)),
            pl.BlockSpec((SEQ, LANES), lambda p, b: (0, 0)),
            pl.BlockSpec((SEQ, LANES), lambda p, b: (0, 0)),
            pl.BlockSpec((None, None, NA_ROWS, 2 * GRID_W, NA_ROWS * GRID_W), lambda p, b: (layer, p, 0, 0, 0)),
        ],
        out_specs=pl.BlockSpec((None, T, LANES), lambda p, b: (b, 0, p)),
        out_shape=jax.ShapeDtypeStruct((B, T, NA_W), BF16),
        scratch_shapes=[
            pltpu.VMEM((GRID_H, 2 * GRID_W, LANES), BF16),
            pltpu.VMEM((GRID_H, 2 * GRID_W, LANES), BF16),
            pltpu.VMEM((2 * CTX, LANES), BF16),
            pltpu.VMEM((SEQ, LANES), BF16),
            pltpu.VMEM((CTX, LANES), BF16),
            pltpu.VMEM((T, LANES), BF16),
            pltpu.VMEM((GRID_H, 2 * GRID_W, CTX), F32),
            pltpu.VMEM((GRID_H, 2 * GRID_W, CTX), BF16),
            pltpu.VMEM((GRID_H, 2 * GRID_W, LANES), F32),
        ],
        compiler_params=_cparams(("parallel", "parallel")),
        name="attn",
    )(z, z, z, cos_t, sin_t, biasmask)


def _sgu_kernel(u_ref, v_ref, g_ref, b_ref, ws_ref, bs_ref, o_ref):
    lane = lax.broadcasted_iota(jnp.int32, (1, LANES), 1)
    first = lane < HEAD
    for n in range(u_ref.shape[0] // SGU_CHUNK):
        r0 = n * SGU_CHUNK
        r1 = r0 + SGU_CHUNK
        v = jax.nn.gelu(v_ref[r0:r1, :])
        mu = jnp.mean(v, axis=-1, keepdims=True)
        var = jnp.mean(jnp.square(v - mu), axis=-1, keepdims=True)
        vn = ((v - mu) * lax.rsqrt(var + EPS) * g_ref[...] + b_ref[...]).astype(BF16)
        for j in range(SGU_W // LANES):
            c0 = j * LANES
            c1 = c0 + LANES
            vp = vn[:, c0:c1]
            mixed = jnp.where(first, _dot(ws_ref[2 * j], vp), _dot(ws_ref[2 * j + 1], vp)) + bs_ref[:, c0:c1]
            o_ref[r0:r1, c0:c1] = (jax.nn.gelu(u_ref[r0:r1, c0:c1]) * mixed).astype(BF16)


def _sgu(z, layer, ln_g, ln_b, ws_bf, bs_full):
    ub = (2 * LRU_W + 3 * NA_W) // SGU_W
    return pl.pallas_call(
        _sgu_kernel,
        grid=(B, T // TM_SGU),
        in_specs=[
            pl.BlockSpec((None, TM_SGU, SGU_W), lambda b, i: (b, i, ub)),
            pl.BlockSpec((None, TM_SGU, SGU_W), lambda b, i: (b, i, ub + 1)),
            pl.BlockSpec((None, 1, SGU_W), lambda b, i: (layer, 0, 0)),
            pl.BlockSpec((None, 1, SGU_W), lambda b, i: (layer, 0, 0)),
            pl.BlockSpec((None, SGU_G, SGU_CHUNK, SGU_CHUNK), lambda b, i: (layer, 0, 0, 0)),
            pl.BlockSpec((None, SGU_CHUNK, SGU_W), lambda b, i: (layer, 0, 0)),
        ],
        out_specs=pl.BlockSpec((None, TM_SGU, SGU_W), lambda b, i: (b, i, 0)),
        out_shape=jax.ShapeDtypeStruct((B, T, SGU_W), BF16),
        compiler_params=_cparams(("parallel", "parallel")),
        name="sgu",
    )(z, z, ln_g, ln_b, ws_bf, bs_full)


def _outproj_kernel(x_ref, a_ref, b_ref, c_ref, gtb, gtc, w_ref, o_ref):
    i = pl.program_id(1)
    acc = _dot(a_ref[...], w_ref[0:LRU_W, :])
    acc = acc + _dot(b_ref[...], w_ref[LRU_W:LRU_W + NA_W, :])
    acc = acc + _dot(c_ref[...], w_ref[LRU_W + NA_W:D, :])
    is_ctx = (lax.broadcasted_iota(jnp.int32, (TM, 1), 0) + i * TM) < CTX
    o_ref[...] = x_ref[...] + jnp.where(is_ctx, gtc[...], gtb[...]) * acc


def _outproj(xall, oa, ob, oc, mod5, layer, w_all):
    tn = D
    return pl.pallas_call(
        _outproj_kernel,
        grid=(B, NT, D // tn),
        in_specs=[
            pl.BlockSpec((None, TM, tn), lambda b, i, j: (b, i, j)),
            pl.BlockSpec((None, TM, LRU_W), lambda b, i, j: (b, i, 0)),
            pl.BlockSpec((None, TM, NA_W), lambda b, i, j: (b, i, 0)),
            pl.BlockSpec((None, TM, SGU_W), lambda b, i, j: (b, i, 0)),
            _mod_spec(layer, 2, False, tn, 2), _mod_spec(layer, 2, True, tn, 2),
            pl.BlockSpec((None, D, tn), lambda b, i, j: (layer, 0, j), pipeline_mode=pl.Buffered(1)),
        ],
        out_specs=pl.BlockSpec((None, TM, tn), lambda b, i, j: (b, i, j)),
        out_shape=jax.ShapeDtypeStruct((B, T, D), F32),
        compiler_params=_cparams(("parallel", "parallel", "parallel"), VMEM_BIG),
        name="outproj",
    )(xall, oa, ob, oc, mod5, mod5, w_all)


def _ffn_kernel(x_ref, xp_ref, xn_ref, g_ref, shb, scb, gtb, shc, scc, gtc, wa_ref, wg_ref, cw_ref, cb_ref, wd_ref,
                o_ref, hx_s, u_s, act_s):
    i = pl.program_id(1)
    k = pl.program_id(2)
    tm = x_ref.shape[0]
    rb = FFN_RB
    left_blocks = (0, CTX // rb)
    right_blocks = (CTX // rb - 1, tm // rb - 1)

    @pl.when(k == 0)
    def _():
        gain_b = g_ref[...] * (1.0 + scb[...])

        def normed(x):
            return (x * lax.rsqrt(jnp.mean(x * x, axis=-1, keepdims=True) + EPS)) * gain_b + shb[...]
        hx_s[0:HALO, :] = jnp.concatenate([normed(xn_ref[...]), normed(xp_ref[...])], axis=0).astype(BF16)
        _norm_tile(i, x_ref, hx_s, HALO, g_ref, shb, scb, shc, scc)
        o_ref[...] = jnp.zeros_like(o_ref)

    slab = FFN_SLAB
    for h in range(tm // slab):
        q0, q1 = h * slab, min((h + 1) * slab + 2 * HALO, HALO + tm)
        u_s[q0:q1, :TH] = _dot(hx_s[q0:q1, :], wa_ref[...])
        u_s[q0:q1, TH:] = _dot(hx_s[q0:q1, :], wg_ref[...])
    u_s[HALO + tm:HALO + tm + SUBLANES, :] = u_s[0:SUBLANES, :]
    cw = cw_ref[...]
    cb = cb_ref[...]
    for blk in range(tm // rb):
        r0 = HALO + blk * rb
        rows = lax.broadcasted_iota(jnp.int32, (rb, 1), 0) + (i * tm + blk * rb)
        ul = u_s[r0 - 1:r0 - 1 + rb, :]
        ur = u_s[r0 + 1:r0 + 1 + rb, :]
        if blk in left_blocks:
            ul = jnp.where(jnp.logical_and(rows != 0, rows != CTX), ul, 0.0)
        if blk in right_blocks:
            ur = jnp.where(jnp.logical_and(rows != CTX - 1, rows != T - 1), ur, 0.0)
        y = cb + ul * cw[0:1]
        y = y + u_s[r0:r0 + rb, :] * cw[1:2]
        y = y + ur * cw[2:3]
        a = y[:, :TH]
        gg = y[:, TH:]
        act_s[blk * rb:(blk + 1) * rb, :] = (gg * jax.nn.sigmoid(gg) * a).astype(BF16)
        if (blk + 1) * rb % slab == 0:
            s0 = (blk + 1) * rb - slab
            o_ref[s0:s0 + slab, :] += _dot(act_s[s0:s0 + slab, :], wd_ref[...])

    @pl.when(k == NK - 1)
    def _():
        def finish(lo, hi, gate):
            o_ref[lo:hi, :] = x_ref[lo:hi, :] + gate * o_ref[lo:hi, :]
        _per_segment(i, tm, finish, gtb[...], gtc[...])


def _ffn(xall, mod5, layer, g, wa, wg, cw_r, cb_r, wd_p):
    tm = TM_FFN
    hb = tm // SUBLANES
    last = T // SUBLANES - 1
    col_map = lambda b, i, k: (layer, 0, k)
    return pl.pallas_call(
        _ffn_kernel,
        grid=(B, T // tm, NK),
        in_specs=[
            pl.BlockSpec((None, tm, D), lambda b, i, k: (b, i, 0)),
            pl.BlockSpec((None, SUBLANES, D), lambda b, i, k: (b, jnp.maximum(i * hb - 1, 0), 0)),
            pl.BlockSpec((None, SUBLANES, D), lambda b, i, k: (b, jnp.minimum((i + 1) * hb, last), 0)),
            pl.BlockSpec((None, 1, D), lambda b, i, k: (layer, 0, 0)),
            _mod_spec(layer, 3, False), _mod_spec(layer, 4, False), _mod_spec(layer, 5, False),
            _mod_spec(layer, 3, True), _mod_spec(layer, 4, True), _mod_spec(layer, 5, True),
            pl.BlockSpec((None, D, TH), col_map),
            pl.BlockSpec((None, D, TH), col_map),
            pl.BlockSpec((None, 3, 2 * TH), col_map),
            pl.BlockSpec((None, 1, 2 * TH), col_map),
            pl.BlockSpec((None, TH, D), lambda b, i, k: (layer, k, 0)),
        ],
        out_specs=pl.BlockSpec((None, tm, D), lambda b, i, k: (b, i, 0)),
        out_shape=jax.ShapeDtypeStruct((B, T, D), F32),
        scratch_shapes=[
            pltpu.VMEM((HALO + tm, D), BF16),
            pltpu.VMEM((HALO + tm + SUBLANES, 2 * TH), F32),
            pltpu.VMEM((tm, TH), BF16),
        ],
        compiler_params=_cparams(("parallel", "parallel", "arbitrary"), VMEM_BIG),
        name="ffn",
    )(xall, xall, xall, g, mod5, mod5, mod5, mod5, mod5, mod5, wa, wg, cw_r, cb_r, wd_p)


def _final_kernel(x_ref, g_ref, o_ref):
    x = x_ref[...]
    o_ref[...] = x * lax.rsqrt(jnp.mean(x * x, axis=-1, keepdims=True) + EPS) * g_ref[...]


def _final_norm(xall, g):
    tm = CTX
    return pl.pallas_call(
        _final_kernel,
        grid=(B, SEQ // tm),
        in_specs=[
            pl.BlockSpec((None, tm, D), lambda b, i: (b, i + 1, 0)),
            pl.BlockSpec((1, D), lambda b, i: (0, 0)),
        ],
        out_specs=pl.BlockSpec((None, tm, D), lambda b, i: (b, i, 0)),
        out_shape=jax.ShapeDtypeStruct((B, SEQ, D), F32),
        compiler_params=_cparams(("parallel", "parallel")),
        name="final_norm",
    )(xall, g.reshape(1, D))


def _rope_tables():
    t = jnp.arange(SEQ)
    pos = jnp.stack([t // GRID_W, t % GRID_W], axis=-1).astype(F32)
    inv = ROPE_BASE ** (-jnp.arange(ROPE_F, dtype=F32) / ROPE_F)
    ang = pos[:, :, None] * inv
    cos, sin = jnp.cos(ang), jnp.sin(ang)
    cos_h = jnp.concatenate([cos[:, 0], cos[:, 0], cos[:, 1], cos[:, 1]], axis=-1)
    sin_h = jnp.concatenate([-sin[:, 0], sin[:, 0], -sin[:, 1], sin[:, 1]], axis=-1)
    reps = LANES // HEAD
    return jnp.tile(cos_h, (1, reps)), jnp.tile(sin_h, (1, reps))


def _bias_tables(rpb):
    nl = rpb.shape[0]
    nr, nc = 2 * NA_ROWS - 1, 2 * NA_KC - 1
    pairs = rpb.reshape(nl, NA_HEADS // 2, 2, nr, nc) * LOG2E
    pairs = jnp.pad(pairs, ((0, 0), (0, 0), (0, 0), (0, 2 * NA_ROWS - nr), (0, LANES - nc)))
    return pl.pallas_call(
        _bias_kernel,
        grid=(nl, NA_HEADS // 2),
        in_specs=[pl.BlockSpec((None, None, 2, 2 * NA_ROWS, LANES), lambda l, p: (l, p, 0, 0, 0))],
        out_specs=pl.BlockSpec((None, None, NA_ROWS, 2 * GRID_W, NA_ROWS * GRID_W), lambda l, p: (l, p, 0, 0, 0)),
        out_shape=jax.ShapeDtypeStruct((nl, NA_HEADS // 2, NA_ROWS, 2 * GRID_W, NA_ROWS * GRID_W), F32),
        compiler_params=_cparams(("parallel", "parallel")),
        name="bias_table",
    )(pairs)


def _bias_kernel(rp_ref, o_ref):
    c = lax.broadcasted_iota(jnp.int32, (GRID_W, LANES), 0)
    lane = lax.broadcasted_iota(jnp.int32, (GRID_W, LANES), 1)
    kc = lane % GRID_W
    col_start = jnp.clip(c - NA_KC // 2, 0, GRID_W - NA_KC)
    in_win = jnp.logical_and(kc >= col_start, kc < col_start + NA_KC)
    first = lane < GRID_W
    per_blk = LANES // GRID_W
    for cls in range(NA_ROWS):
        for hd in range(2):
            for blk in range(NA_ROWS // per_blk):
                parts = []
                for h in range(per_blk):
                    m = blk * per_blk + h - cls + NA_ROWS - 1
                    row = jnp.broadcast_to(rp_ref[hd, m:m + 1, :], (GRID_W, LANES))
                    shift = (LANES - (NA_KC - 1) + h * GRID_W) % LANES
                    parts.append(pltpu.roll(row, shift, 1, stride=1, stride_axis=0))
                tile = jnp.where(in_win, jnp.where(first, parts[0], parts[1]), NEG)
                o_ref[cls, hd * GRID_W:(hd + 1) * GRID_W, blk * LANES:(blk + 1) * LANES] = tile


def _gate_blockdiag(gate_w):
    nl = gate_w.shape[0]
    nh = LRU_W // LRU_HALF
    gper = LRU_HALF // HEAD
    w = gate_w.reshape(nl, 4, nh, gper, HEAD, HEAD)
    eye = jnp.eye(gper, dtype=gate_w.dtype)
    bd = jnp.einsum('lkhgio,gj->lkhgijo', w, eye).reshape(nl, 4, nh, LRU_HALF, LRU_HALF)
    return jnp.transpose(bd, (0, 2, 1, 3, 4)).astype(BF16)


def _ffn_layout(w_up, conv_w, conv_b, w_down):
    pad = FFN_HP - FFN_H
    nl = w_up.shape[0]

    def inter(m):
        r = m.shape[1]
        a = jnp.pad(m[..., :FFN_H], ((0, 0), (0, 0), (0, pad))).reshape(nl, r, NK, 1, TH)
        g = jnp.pad(m[..., FFN_H:], ((0, 0), (0, 0), (0, pad))).reshape(nl, r, NK, 1, TH)
        return jnp.concatenate([a, g], axis=3).reshape(nl, r, NK * 2 * TH)

    zc = jnp.zeros((nl, D, pad), BF16)
    wa = jnp.concatenate([w_up[..., :FFN_H].astype(BF16), zc], axis=2)
    wg = jnp.concatenate([w_up[..., FFN_H:].astype(BF16), zc], axis=2)
    wd = jnp.concatenate([w_down.astype(BF16), jnp.zeros((nl, pad, D), BF16)], axis=1)
    return wa, wg, inter(conv_w), inter(conv_b[:, None, :]), wd


def kernel(x, c, ctx, c_ctx, w_ada, b_ada, norm_mix_g, norm_ffn_g, w_in, lru_conv_w, lru_conv_b, lru_gate_w, lru_gate_b, lru_lambda, na_rpb, sgu_ln_g, sgu_ln_b, sgu_w, sgu_b, w_out, ffn_up, ffn_conv_w, ffn_conv_b, ffn_down, final_norm_g):
    xall = jnp.concatenate([ctx, x], axis=1)
    cc = jnp.concatenate([c, c_ctx[None], jnp.zeros((MOD_ROWS - B - 1, D), F32)], axis=0)
    mod5 = _adaln(cc, w_ada, b_ada).reshape(DEPTH, 6, MOD_ROWS, 1, D)
    cos_t, sin_t = _rope_tables()

    g_mix = norm_mix_g.reshape(DEPTH, 1, D)
    g_ffn = norm_ffn_g.reshape(DEPTH, 1, D)
    w_in_bf = w_in.astype(BF16)
    w_out_bf = w_out.astype(BF16)
    lru_cb = lru_conv_b.reshape(DEPTH, 1, LRU_W)
    lru_gw = _gate_blockdiag(lru_gate_w)
    lru_gb = lru_gate_b.reshape(DEPTH, 4, LRU_W)
    bias = _bias_tables(na_rpb)
    sgu_g = sgu_ln_g.reshape(DEPTH, 1, SGU_W)
    sgu_bb = sgu_ln_b.reshape(DEPTH, 1, SGU_W)
    sgu_w_bf = sgu_w.astype(BF16)
    sgu_bias = jnp.repeat(jnp.swapaxes(sgu_b, 1, 2), HEAD, axis=2)
    ffn_params = _ffn_layout(ffn_up, ffn_conv_w, ffn_conv_b, ffn_down)

    for l in range(DEPTH):
        z = _inproj(xall, mod5, l, g_mix, w_in_bf)
        oa = _lru(z, l, lru_conv_w, lru_cb, lru_gw, lru_gb, lru_lambda)
        ob = _attn(z, l, cos_t, sin_t, bias)
        oc = _sgu(z, l, sgu_g, sgu_bb, sgu_w_bf, sgu_bias)
        xall = _outproj(xall, oa, ob, oc, mod5, l, w_out_bf)
        xall = _ffn(xall, mod5, l, g_ffn, *ffn_params)
    return _final_norm(xall, final_norm_g)
```

```python
import functools

import jax
import jax.numpy as jnp
from jax import lax
from jax.experimental import pallas as pl
from jax.experimental.pallas import tpu as pltpu

F32 = jnp.float32
BF16 = jnp.bfloat16

D = 2048
B = 8
SEQ = 2048
DEPTH = 4
GRID_W = 64
GRID_H = SEQ // GRID_W
CTX = 256
T = CTX + SEQ
HEAD = 64
LRU_W = 512
LRU_TAPS = 4
LRU_C = 8.0
NA_W = 1024
NA_HEADS = NA_W // HEAD
NA_ROWS = 8
NA_KC = 16
SGU_W = 512
SGU_G = SGU_W // HEAD
SGU_CHUNK = 128
PROJ = 2 * LRU_W + 3 * NA_W + 2 * SGU_W
FFN_H = 5504
EPS = 1e-6
NEG = -1e30
LOG2E = 1.4426950408889634
ROPE_BASE = 10000.0
ROPE_F = HEAD // 4
MOD_ROWS = 16
CTX_ROW = B

LANES = 128
SUBLANES = 8
VMEM_BIG = 56 * 1024 * 1024
VMEM_MID = 40 * 1024 * 1024

TM = 768
NT = T // TM
HALO = 16
TH = 512
FFN_HP = 5632
NK = FFN_HP // TH
LRU_HALF = 256
LRU_RB = 128
ATT_RB = 256
NORM_RB = 16
NORM_UNROLL = 4
FFN_RB = 128
TM_FFN = 768
TM_SGU = 1152
FFN_SLAB = 384
ATT_UNROLL = 4
ATT_GROUP = 16


def _cparams(sem, vmem=VMEM_MID):
    return pltpu.CompilerParams(dimension_semantics=sem, vmem_limit_bytes=vmem)


def _dot(a, b):
    return jnp.dot(a, b, preferred_element_type=F32)


def _dot_nt(a, b):
    return lax.dot_general(a, b, (((1,), (1,)), ((), ())), preferred_element_type=F32)


def _mod_spec(layer, which, ctx, width=D, col=None):
    def imap(*ids):
        b = ids[0]
        row = CTX_ROW if ctx else b
        c = 0 if col is None else ids[col]
        return (layer, which, row, 0, c)
    return pl.BlockSpec((None, None, None, 1, width), imap)


def _per_segment(i, tm, fn, latent_arg, ctx_arg):
    @pl.when(i == 0)
    def _():
        fn(0, CTX, ctx_arg)
        fn(CTX, tm, latent_arg)

    @pl.when(i != 0)
    def _():
        fn(0, tm, latent_arg)


def _norm_rows(x_ref, dst_ref, dst_off, lo, hi, gain, shift):
    def body(j, _):
        r = pl.multiple_of(lo + j * NORM_RB, NORM_RB)
        x = x_ref[pl.ds(r, NORM_RB), :]
        inv = lax.rsqrt(jnp.mean(x * x, axis=-1, keepdims=True) + EPS)
        dst_ref[pl.ds(dst_off + r, NORM_RB), :] = ((x * inv) * gain + shift).astype(BF16)
        return 0
    n = (hi - lo) // NORM_RB
    lax.fori_loop(0, n, body, 0, unroll=min(n, NORM_UNROLL))


def _norm_tile(i, x_ref, dst_ref, dst_off, g_ref, shb, scb, shc, scc):
    g = g_ref[...]

    def run(lo, hi, mod):
        _norm_rows(x_ref, dst_ref, dst_off, lo, hi, g * (1.0 + mod[0]), mod[1])
    _per_segment(i, x_ref.shape[0], run, (scb[...], shb[...]), (scc[...], shc[...]))


def _adaln_kernel(c_ref, w_ref, b_ref, o_ref):
    c = c_ref[...]
    s = (c * jax.nn.sigmoid(c)).astype(BF16)
    o_ref[...] = _dot(s, w_ref[...].astype(BF16)) + b_ref[...]


def _adaln(cc, w_ada, b_ada):
    tn = D
    nj = D // tn
    return pl.pallas_call(
        _adaln_kernel,
        grid=(DEPTH, 6 * nj),
        in_specs=[
            pl.BlockSpec((MOD_ROWS, D), lambda l, j: (0, 0)),
            pl.BlockSpec((None, D, tn), lambda l, j: (l, 0, j)),
            pl.BlockSpec((None, 1, tn), lambda l, j: (l, 0, j)),
        ],
        out_specs=pl.BlockSpec((None, None, MOD_ROWS, tn), lambda l, j: (l, j // nj, 0, j % nj)),
        out_shape=jax.ShapeDtypeStruct((DEPTH, 6, MOD_ROWS, D), F32),
        compiler_params=_cparams(("parallel", "parallel"), VMEM_BIG),
        name="adaln",
    )(cc, w_ada, b_ada.reshape(DEPTH, 1, 6 * D))


def _inproj_kernel(x_ref, g_ref, shb, scb, shc, scc, w_ref, o_ref, hx_s):
    i = pl.program_id(1)

    @pl.when(pl.program_id(2) == 0)
    def _():
        _norm_tile(i, x_ref, hx_s, 0, g_ref, shb, scb, shc, scc)

    o_ref[...] = _dot(hx_s[...], w_ref[...])


def _inproj(xall, mod5, layer, g_all, w_all):
    tn = PROJ // 2
    return pl.pallas_call(
        _inproj_kernel,
        grid=(B, NT, PROJ // tn),
        in_specs=[
            pl.BlockSpec((None, TM, D), lambda b, i, j: (b, i, 0)),
            pl.BlockSpec((None, 1, D), lambda b, i, j: (layer, 0, 0)),
            _mod_spec(layer, 0, False), _mod_spec(layer, 1, False),
            _mod_spec(layer, 0, True), _mod_spec(layer, 1, True),
            pl.BlockSpec((None, D, tn), lambda b, i, j: (layer, 0, j)),
        ],
        out_specs=pl.BlockSpec((None, TM, tn), lambda b, i, j: (b, i, j)),
        out_shape=jax.ShapeDtypeStruct((B, T, PROJ), F32),
        scratch_shapes=[pltpu.VMEM((TM, D), BF16)],
        compiler_params=_cparams(("parallel", "parallel", "arbitrary"), VMEM_BIG),
        name="inproj",
    )(xall, g_all, mod5, mod5, mod5, mod5, w_all)


def _tile_scan(a, b, carry, rev):
    rid = lax.broadcasted_iota(jnp.int32, a.shape, 0)
    for d in (1, 2, 4):
        sh = SUBLANES - d if rev else d
        keep = (rid < SUBLANES - d) if rev else (rid >= d)
        a_s = jnp.where(keep, pltpu.roll(a, sh, 0), 1.0)
        b_s = jnp.where(keep, pltpu.roll(b, sh, 0), 0.0)
        b = a * b_s + b
        a = a * a_s
    h = a * carry + b
    return h, (h[0:1] if rev else h[SUBLANES - 1:SUBLANES])


def _lru_kernel(ax_ref, ay_ref, cw_ref, cb_ref, gw_ref, gb_ref, lam_ref, o_ref, af, bf, ab, bb):
    W = LRU_HALF
    cw = cw_ref[...]
    cb = cb_ref[...]
    gb = gb_ref[...]
    nl = -lam_ref[...]
    softplus = jnp.maximum(nl, 0.0) + jnp.log(1.0 + jnp.exp(-jnp.abs(nl)))
    coef = -LRU_C * softplus
    zeros8 = jnp.zeros((SUBLANES, W), F32)
    n_ext = LRU_RB + 2 * SUBLANES

    for blk in range(T // LRU_RB):
        r0 = blk * LRU_RB
        r1 = r0 + LRU_RB
        prev = zeros8 if r0 in (0, CTX) else ax_ref[r0 - SUBLANES:r0, :]
        nxt = zeros8 if r1 in (CTX, T) else ax_ref[r1:r1 + SUBLANES, :]
        ext = jnp.concatenate([prev, ax_ref[r0:r1, :], nxt], axis=0)
        lo, hi = SUBLANES, SUBLANES + LRU_RB
        xc = cb + pltpu.roll(ext, 2, 0)[lo:hi] * cw[0:1]
        xc = xc + pltpu.roll(ext, 1, 0)[lo:hi] * cw[1:2]
        xc = xc + ext[lo:hi] * cw[2:3]
        xc = xc + pltpu.roll(ext, n_ext - 1, 0)[lo:hi] * cw[3:4]
        xb = xc.astype(BF16)
        for d, (a_s, b_s) in enumerate(((af, bf), (ab, bb))):
            r = jax.nn.sigmoid(_dot(xb, gw_ref[2 * d]) + gb[2 * d:2 * d + 1])
            gi = jax.nn.sigmoid(_dot(xb, gw_ref[2 * d + 1]) + gb[2 * d + 1:2 * d + 2])
            log_a = coef[d:d + 1] * r
            a_s[r0:r1, :] = jnp.exp(log_a)
            th = jnp.tanh(log_a)
            b_s[r0:r1, :] = jnp.sqrt(-2.0 * th / (1.0 - th)) * (gi * xc)

    def make_body(f_base, b_top):
        def body(k, carry):
            cf, cr = carry
            rf = pl.multiple_of((f_base + k) * SUBLANES, SUBLANES)
            hf, cf = _tile_scan(af[pl.ds(rf, SUBLANES), :], bf[pl.ds(rf, SUBLANES), :], cf, False)
            bf[pl.ds(rf, SUBLANES), :] = hf
            rb = pl.multiple_of((b_top - k) * SUBLANES, SUBLANES)
            hb, cr = _tile_scan(ab[pl.ds(rb, SUBLANES), :], bb[pl.ds(rb, SUBLANES), :], cr, True)
            bb[pl.ds(rb, SUBLANES), :] = hb
            return cf, cr
        return body

    z1 = jnp.zeros((1, W), F32)
    n_ctx = CTX // SUBLANES
    n_all = T // SUBLANES
    carry = lax.fori_loop(0, n_ctx, make_body(0, n_ctx - 1), (z1, z1))
    lax.fori_loop(0, n_all - n_ctx, make_body(n_ctx, n_all - 1), carry)

    for blk in range(T // LRU_RB):
        r0 = blk * LRU_RB
        r1 = r0 + LRU_RB
        h = bf[r0:r1, :] + bb[r0:r1, :]
        o_ref[r0:r1, :] = (h * jax.nn.gelu(ay_ref[r0:r1, :])).astype(BF16)


def _lru(z, layer, cw, cb, gw_bd, gb, lam):
    W = LRU_HALF
    nh = LRU_W // W
    return pl.pallas_call(
        _lru_kernel,
        grid=(B, nh),
        in_specs=[
            pl.BlockSpec((None, T, W), lambda b, c: (b, 0, c)),
            pl.BlockSpec((None, T, W), lambda b, c: (b, 0, nh + c)),
            pl.BlockSpec((None, LRU_TAPS, W), lambda b, c: (layer, 0, c)),
            pl.BlockSpec((None, 1, W), lambda b, c: (layer, 0, c)),
            pl.BlockSpec((None, None, 4, W, W), lambda b, c: (layer, c, 0, 0, 0)),
            pl.BlockSpec((None, 4, W), lambda b, c: (layer, 0, c)),
            pl.BlockSpec((None, 2, W), lambda b, c: (layer, 0, c)),
        ],
        out_specs=pl.BlockSpec((None, T, W), lambda b, c: (b, 0, c)),
        out_shape=jax.ShapeDtypeStruct((B, T, LRU_W), BF16),
        scratch_shapes=[pltpu.VMEM((T, W), F32)] * 4,
        compiler_params=_cparams(("parallel", "parallel")),
        name="lru",
    )(z, z, cw, cb, gw_bd, gb, lam)


def _attn_kernel(q_ref, k_ref, v_ref, cos_ref, sin_ref, bias_ref, o_ref,
                 qraw_s, qrot_s, qctx_s, krot_s, kc_s, v_s, sc_s, pc_s, ow_s):
    lane = lax.broadcasted_iota(jnp.int32, (1, LANES), 1)
    head0 = lane < HEAD
    second16 = ((lane // ROPE_F) % 2) == 1
    scale = HEAD ** -0.5 * LOG2E
    W2 = 2 * GRID_W

    def rope(x, c, s):
        partner = jnp.where(second16, pltpu.roll(x, ROPE_F, 1), pltpu.roll(x, LANES - ROPE_F, 1))
        return x * c + partner * s

    def stack_heads(x):
        return jnp.concatenate([jnp.where(head0, x, 0.0), jnp.where(head0, 0.0, x)], axis=0).astype(BF16)

    def unstack_heads(y):
        n = y.shape[0] // 2
        return jnp.where(head0, y[:n], y[n:])

    qctx_s[...] = stack_heads(q_ref[0:CTX, :] * scale)
    kc_s[...] = k_ref[0:CTX, :].astype(BF16)
    v_s[0:CTX, :] = v_ref[0:CTX, :].astype(BF16)
    rows_per_blk = ATT_RB // GRID_W
    for blk in range(SEQ // ATT_RB):
        r0 = blk * ATT_RB
        r1 = r0 + ATT_RB
        c = cos_ref[r0:r1, :]
        s = sin_ref[r0:r1, :]
        q = q_ref[CTX + r0:CTX + r1, :]
        qs = q * scale
        qr = rope(q, c, s) * scale
        for j in range(rows_per_blk):
            qraw_s[blk * rows_per_blk + j] = stack_heads(qs[j * GRID_W:(j + 1) * GRID_W])
            qrot_s[blk * rows_per_blk + j] = stack_heads(qr[j * GRID_W:(j + 1) * GRID_W])
        krot_s[r0:r1, :] = rope(k_ref[CTX + r0:CTX + r1, :], c, s).astype(BF16)
        v_s[CTX + r0:CTX + r1, :] = v_ref[CTX + r0:CTX + r1, :].astype(BF16)

    s = _dot_nt(qctx_s[...], kc_s[...])
    e = jnp.exp2(s - jnp.max(s, axis=-1, keepdims=True))
    p = (e / jnp.sum(e, axis=-1, keepdims=True)).astype(BF16)
    o_ref[0:CTX, :] = unstack_heads(_dot(p, v_s[0:CTX, :])).astype(BF16)

    for g in range(GRID_H // ATT_GROUP):
        g0, g1 = g * ATT_GROUP, (g + 1) * ATT_GROUP
        sc = _dot_nt(qraw_s[g0:g1].reshape(ATT_GROUP * W2, LANES), kc_s[...])
        sc_s[g0:g1] = sc.reshape(ATT_GROUP, W2, CTX)

    win = NA_ROWS * GRID_W

    def lane_fold(op, *xs):
        cols = [x[:, c:c + LANES] for x in xs for c in range(0, x.shape[1], LANES)]
        acc = cols[0]
        for col in cols[1:]:
            acc = op(acc, col)
        return acc

    def body(t, _):
        rows = [t * ATT_UNROLL + j for j in range(ATT_UNROLL)]
        starts = [jnp.clip(r - NA_ROWS // 2, 0, GRID_H - NA_ROWS) for r in rows]
        sw = [_dot_nt(qrot_s[r], krot_s[pl.ds(pl.multiple_of(rs * GRID_W, GRID_W), win), :]) + bias_ref[r - rs]
              for r, rs in zip(rows, starts)]
        sc = [sc_s[r] for r in rows]
        m = [jnp.max(lane_fold(jnp.maximum, a, c), axis=-1, keepdims=True) for a, c in zip(sw, sc)]
        ew = [jnp.exp2(a - mm) for a, mm in zip(sw, m)]
        ec = [jnp.exp2(c - mm) for c, mm in zip(sc, m)]
        inv = [1.0 / jnp.sum(lane_fold(jnp.add, a, c), axis=-1, keepdims=True) for a, c in zip(ew, ec)]
        for r, rs, a, c, iv in zip(rows, starts, ew, ec, inv):
            kx = pl.multiple_of(CTX + rs * GRID_W, GRID_W)
            ow_s[r] = _dot(a.astype(BF16), v_s[pl.ds(kx, win), :]) * iv
            pc_s[r] = (c * iv).astype(BF16)
        return 0

    lax.fori_loop(0, GRID_H // ATT_UNROLL, body, 0)

    for g in range(GRID_H // ATT_GROUP):
        g0, g1 = g * ATT_GROUP, (g + 1) * ATT_GROUP
        oc = _dot(pc_s[g0:g1].reshape(ATT_GROUP * W2, CTX), v_s[0:CTX, :]).reshape(ATT_GROUP, W2, LANES)
        tot = oc + ow_s[g0:g1]
        for j in range(ATT_GROUP):
            row = CTX + (g0 + j) * GRID_W
            o_ref[row:row + GRID_W, :] = unstack_heads(tot[j]).astype(BF16)


def _attn(z, layer, cos_t, sin_t, biasmask):
    nhp = NA_W // LANES
    qb = 2 * LRU_W // LANES
    return pl.pallas_call(
        _attn_kernel,
        grid=(nhp, B),
        in_specs=[
            pl.BlockSpec((None, T, LANES), lambda p, b: (b, 0, qb + p)),
            pl.BlockSpec((None, T, LANES), lambda p, b: (b, 0, qb + nhp + p)),
            pl.BlockSpec((None, T, LANES), lambda p, b: (b, 0, qb + 2 * nhp + p)),
            pl.BlockSpec((SEQ, LANES), lambda p, b: (0, 0)),
            pl.BlockSpec((SEQ, LANES), lambda p, b: (0, 0)),
            pl.BlockSpec((None, None, NA_ROWS, 2 * GRID_W, NA_ROWS * GRID_W), lambda p, b: (layer, p, 0, 0, 0)),
        ],
        out_specs=pl.BlockSpec((None, T, LANES), lambda p, b: (b, 0, p)),
        out_shape=jax.ShapeDtypeStruct((B, T, NA_W), BF16),
        scratch_shapes=[
            pltpu.VMEM((GRID_H, 2 * GRID_W, LANES), BF16),
            pltpu.VMEM((GRID_H, 2 * GRID_W, LANES), BF16),
            pltpu.VMEM((2 * CTX, LANES), BF16),
            pltpu.VMEM((SEQ, LANES), BF16),
            pltpu.VMEM((CTX, LANES), BF16),
            pltpu.VMEM((T, LANES), BF16),
            pltpu.VMEM((GRID_H, 2 * GRID_W, CTX), F32),
            pltpu.VMEM((GRID_H, 2 * GRID_W, CTX), BF16),
            pltpu.VMEM((GRID_H, 2 * GRID_W, LANES), F32),
        ],
        compiler_params=_cparams(("parallel", "parallel")),
        name="attn",
    )(z, z, z, cos_t, sin_t, biasmask)


def _sgu_kernel(u_ref, v_ref, g_ref, b_ref, ws_ref, bs_ref, o_ref):
    lane = lax.broadcasted_iota(jnp.int32, (1, LANES), 1)
    first = lane < HEAD
    for n in range(u_ref.shape[0] // SGU_CHUNK):
        r0 = n * SGU_CHUNK
        r1 = r0 + SGU_CHUNK
        v = jax.nn.gelu(v_ref[r0:r1, :])
        mu = jnp.mean(v, axis=-1, keepdims=True)
        var = jnp.mean(jnp.square(v - mu), axis=-1, keepdims=True)
        vn = ((v - mu) * lax.rsqrt(var + EPS) * g_ref[...] + b_ref[...]).astype(BF16)
        for j in range(SGU_W // LANES):
            c0 = j * LANES
            c1 = c0 + LANES
            vp = vn[:, c0:c1]
            mixed = jnp.where(first, _dot(ws_ref[2 * j], vp), _dot(ws_ref[2 * j + 1], vp)) + bs_ref[:, c0:c1]
            o_ref[r0:r1, c0:c1] = (jax.nn.gelu(u_ref[r0:r1, c0:c1]) * mixed).astype(BF16)


def _sgu(z, layer, ln_g, ln_b, ws_bf, bs_full):
    ub = (2 * LRU_W + 3 * NA_W) // SGU_W
    return pl.pallas_call(
        _sgu_kernel,
        grid=(B, T // TM_SGU),
        in_specs=[
            pl.BlockSpec((None, TM_SGU, SGU_W), lambda b, i: (b, i, ub)),
            pl.BlockSpec((None, TM_SGU, SGU_W), lambda b, i: (b, i, ub + 1)),
            pl.BlockSpec((None, 1, SGU_W), lambda b, i: (layer, 0, 0)),
            pl.BlockSpec((None, 1, SGU_W), lambda b, i: (layer, 0, 0)),
            pl.BlockSpec((None, SGU_G, SGU_CHUNK, SGU_CHUNK), lambda b, i: (layer, 0, 0, 0)),
            pl.BlockSpec((None, SGU_CHUNK, SGU_W), lambda b, i: (layer, 0, 0)),
        ],
        out_specs=pl.BlockSpec((None, TM_SGU, SGU_W), lambda b, i: (b, i, 0)),
        out_shape=jax.ShapeDtypeStruct((B, T, SGU_W), BF16),
        compiler_params=_cparams(("parallel", "parallel")),
        name="sgu",
    )(z, z, ln_g, ln_b, ws_bf, bs_full)


def _outproj_kernel(x_ref, a_ref, b_ref, c_ref, gtb, gtc, w_ref, o_ref):
    i = pl.program_id(1)
    acc = _dot(a_ref[...], w_ref[0:LRU_W, :])
    acc = acc + _dot(b_ref[...], w_ref[LRU_W:LRU_W + NA_W, :])
    acc = acc + _dot(c_ref[...], w_ref[LRU_W + NA_W:D, :])
    is_ctx = (lax.broadcasted_iota(jnp.int32, (TM, 1), 0) + i * TM) < CTX
    o_ref[...] = x_ref[...] + jnp.where(is_ctx, gtc[...], gtb[...]) * acc


def _outproj(xall, oa, ob, oc, mod5, layer, w_all):
    tn = D
    return pl.pallas_call(
        _outproj_kernel,
        grid=(B, NT, D // tn),
        in_specs=[
            pl.BlockSpec((None, TM, tn), lambda b, i, j: (b, i, j)),
            pl.BlockSpec((None, TM, LRU_W), lambda b, i, j: (b, i, 0)),
            pl.BlockSpec((None, TM, NA_W), lambda b, i, j: (b, i, 0)),
            pl.BlockSpec((None, TM, SGU_W), lambda b, i, j: (b, i, 0)),
            _mod_spec(layer, 2, False, tn, 2), _mod_spec(layer, 2, True, tn, 2),
            pl.BlockSpec((None, D, tn), lambda b, i, j: (layer, 0, j), pipeline_mode=pl.Buffered(1)),
        ],
        out_specs=pl.BlockSpec((None, TM, tn), lambda b, i, j: (b, i, j)),
        out_shape=jax.ShapeDtypeStruct((B, T, D), F32),
        compiler_params=_cparams(("parallel", "parallel", "parallel"), VMEM_BIG),
        name="outproj",
    )(xall, oa, ob, oc, mod5, mod5, w_all)


def _ffn_kernel(x_ref, xp_ref, xn_ref, g_ref, shb, scb, gtb, shc, scc, gtc, wa_ref, wg_ref, cw_ref, cb_ref, wd_ref,
                o_ref, hx_s, u_s, act_s):
    i = pl.program_id(1)
    k = pl.program_id(2)
    tm = x_ref.shape[0]
    rb = FFN_RB
    left_blocks = (0, CTX // rb)
    right_blocks = (CTX // rb - 1, tm // rb - 1)

    @pl.when(k == 0)
    def _():
        gain_b = g_ref[...] * (1.0 + scb[...])

        def normed(x):
            return (x * lax.rsqrt(jnp.mean(x * x, axis=-1, keepdims=True) + EPS)) * gain_b + shb[...]
        hx_s[0:HALO, :] = jnp.concatenate([normed(xn_ref[...]), normed(xp_ref[...])], axis=0).astype(BF16)
        _norm_tile(i, x_ref, hx_s, HALO, g_ref, shb, scb, shc, scc)
        o_ref[...] = jnp.zeros_like(o_ref)

    slab = FFN_SLAB
    for h in range(tm // slab):
        q0 = 0 if h == 0 else h * slab + 2 * HALO
        q1 = min((h + 1) * slab + 2 * HALO, HALO + tm)
        u_s[q0:q1, :TH] = _dot(hx_s[q0:q1, :], wa_ref[...])
        u_s[q0:q1, TH:] = _dot(hx_s[q0:q1, :], wg_ref[...])
    u_s[HALO + tm:HALO + tm + SUBLANES, :] = u_s[0:SUBLANES, :]
    cw = cw_ref[...]
    cb = cb_ref[...]
    for blk in range(tm // rb):
        r0 = HALO + blk * rb
        rows = lax.broadcasted_iota(jnp.int32, (rb, 1), 0) + (i * tm + blk * rb)
        ul = u_s[r0 - 1:r0 - 1 + rb, :]
        ur = u_s[r0 + 1:r0 + 1 + rb, :]
        if blk in left_blocks:
            ul = jnp.where(jnp.logical_and(rows != 0, rows != CTX), ul, 0.0)
        if blk in right_blocks:
            ur = jnp.where(jnp.logical_and(rows != CTX - 1, rows != T - 1), ur, 0.0)
        y = cb + ul * cw[0:1]
        y = y + u_s[r0:r0 + rb, :] * cw[1:2]
        y = y + ur * cw[2:3]
        a = y[:, :TH]
        gg = y[:, TH:]
        act_s[blk * rb:(blk + 1) * rb, :] = (gg * jax.nn.sigmoid(gg) * a).astype(BF16)
        if (blk + 1) * rb % slab == 0:
            s0 = (blk + 1) * rb - slab
            o_ref[s0:s0 + slab, :] += _dot(act_s[s0:s0 + slab, :], wd_ref[...])

    @pl.when(k == NK - 1)
    def _():
        def finish(lo, hi, gate):
            o_ref[lo:hi, :] = x_ref[lo:hi, :] + gate * o_ref[lo:hi, :]
        _per_segment(i, tm, finish, gtb[...], gtc[...])


def _ffn(xall, mod5, layer, g, wa, wg, cw_r, cb_r, wd_p):
    tm = TM_FFN
    hb = tm // SUBLANES
    last = T // SUBLANES - 1
    col_map = lambda b, i, k: (layer, 0, k)
    return pl.pallas_call(
        _ffn_kernel,
        grid=(B, T // tm, NK),
        in_specs=[
            pl.BlockSpec((None, tm, D), lambda b, i, k: (b, i, 0)),
            pl.BlockSpec((None, SUBLANES, D), lambda b, i, k: (b, jnp.maximum(i * hb - 1, 0), 0)),
            pl.BlockSpec((None, SUBLANES, D), lambda b, i, k: (b, jnp.minimum((i + 1) * hb, last), 0)),
            pl.BlockSpec((None, 1, D), lambda b, i, k: (layer, 0, 0)),
            _mod_spec(layer, 3, False), _mod_spec(layer, 4, False), _mod_spec(layer, 5, False),
            _mod_spec(layer, 3, True), _mod_spec(layer, 4, True), _mod_spec(layer, 5, True),
            pl.BlockSpec((None, D, TH), col_map),
            pl.BlockSpec((None, D, TH), col_map),
            pl.BlockSpec((None, 3, 2 * TH), col_map),
            pl.BlockSpec((None, 1, 2 * TH), col_map),
            pl.BlockSpec((None, TH, D), lambda b, i, k: (layer, k, 0)),
        ],
        out_specs=pl.BlockSpec((None, tm, D), lambda b, i, k: (b, i, 0)),
        out_shape=jax.ShapeDtypeStruct((B, T, D), F32),
        scratch_shapes=[
            pltpu.VMEM((HALO + tm, D), BF16),
            pltpu.VMEM((HALO + tm + SUBLANES, 2 * TH), F32),
            pltpu.VMEM((tm, TH), BF16),
        ],
        compiler_params=_cparams(("parallel", "parallel", "arbitrary"), VMEM_BIG),
        name="ffn",
    )(xall, xall, xall, g, mod5, mod5, mod5, mod5, mod5, mod5, wa, wg, cw_r, cb_r, wd_p)


def _final_kernel(x_ref, g_ref, o_ref):
    x = x_ref[...]
    o_ref[...] = x * lax.rsqrt(jnp.mean(x * x, axis=-1, keepdims=True) + EPS) * g_ref[...]


def _final_norm(xall, g):
    tm = CTX
    return pl.pallas_call(
        _final_kernel,
        grid=(B, SEQ // tm),
        in_specs=[
            pl.BlockSpec((None, tm, D), lambda b, i: (b, i + 1, 0)),
            pl.BlockSpec((1, D), lambda b, i: (0, 0)),
        ],
        out_specs=pl.BlockSpec((None, tm, D), lambda b, i: (b, i, 0)),
        out_shape=jax.ShapeDtypeStruct((B, SEQ, D), F32),
        compiler_params=_cparams(("parallel", "parallel")),
        name="final_norm",
    )(xall, g.reshape(1, D))


def _rope_tables():
    t = jnp.arange(SEQ)
    pos = jnp.stack([t // GRID_W, t % GRID_W], axis=-1).astype(F32)
    inv = ROPE_BASE ** (-jnp.arange(ROPE_F, dtype=F32) / ROPE_F)
    ang = pos[:, :, None] * inv
    cos, sin = jnp.cos(ang), jnp.sin(ang)
    cos_h = jnp.concatenate([cos[:, 0], cos[:, 0], cos[:, 1], cos[:, 1]], axis=-1)
    sin_h = jnp.concatenate([-sin[:, 0], sin[:, 0], -sin[:, 1], sin[:, 1]], axis=-1)
    reps = LANES // HEAD
    return jnp.tile(cos_h, (1, reps)), jnp.tile(sin_h, (1, reps))


def _bias_tables(rpb):
    nl = rpb.shape[0]
    nr, nc = 2 * NA_ROWS - 1, 2 * NA_KC - 1
    pairs = rpb.reshape(nl, NA_HEADS // 2, 2, nr, nc) * LOG2E
    pairs = jnp.pad(pairs, ((0, 0), (0, 0), (0, 0), (0, 2 * NA_ROWS - nr), (0, LANES - nc)))
    return pl.pallas_call(
        _bias_kernel,
        grid=(nl, NA_HEADS // 2),
        in_specs=[pl.BlockSpec((None, None, 2, 2 * NA_ROWS, LANES), lambda l, p: (l, p, 0, 0, 0))],
        out_specs=pl.BlockSpec((None, None, NA_ROWS, 2 * GRID_W, NA_ROWS * GRID_W), lambda l, p: (l, p, 0, 0, 0)),
        out_shape=jax.ShapeDtypeStruct((nl, NA_HEADS // 2, NA_ROWS, 2 * GRID_W, NA_ROWS * GRID_W), F32),
        compiler_params=_cparams(("parallel", "parallel")),
        name="bias_table",
    )(pairs)


def _bias_kernel(rp_ref, o_ref):
    c = lax.broadcasted_iota(jnp.int32, (GRID_W, LANES), 0)
    lane = lax.broadcasted_iota(jnp.int32, (GRID_W, LANES), 1)
    kc = lane % GRID_W
    col_start = jnp.clip(c - NA_KC // 2, 0, GRID_W - NA_KC)
    in_win = jnp.logical_and(kc >= col_start, kc < col_start + NA_KC)
    first = lane < GRID_W
    per_blk = LANES // GRID_W
    for cls in range(NA_ROWS):
        for hd in range(2):
            for blk in range(NA_ROWS // per_blk):
                parts = []
                for h in range(per_blk):
                    m = blk * per_blk + h - cls + NA_ROWS - 1
                    row = jnp.broadcast_to(rp_ref[hd, m:m + 1, :], (GRID_W, LANES))
                    shift = (LANES - (NA_KC - 1) + h * GRID_W) % LANES
                    parts.append(pltpu.roll(row, shift, 1, stride=1, stride_axis=0))
                tile = jnp.where(in_win, jnp.where(first, parts[0], parts[1]), NEG)
                o_ref[cls, hd * GRID_W:(hd + 1) * GRID_W, blk * LANES:(blk + 1) * LANES] = tile


def _gate_blockdiag(gate_w):
    nl = gate_w.shape[0]
    nh = LRU_W // LRU_HALF
    gper = LRU_HALF // HEAD
    w = gate_w.reshape(nl, 4, nh, gper, HEAD, HEAD)
    eye = jnp.eye(gper, dtype=gate_w.dtype)
    bd = jnp.einsum('lkhgio,gj->lkhgijo', w, eye).reshape(nl, 4, nh, LRU_HALF, LRU_HALF)
    return jnp.transpose(bd, (0, 2, 1, 3, 4)).astype(BF16)


def _ffn_layout(w_up, conv_w, conv_b, w_down):
    pad = FFN_HP - FFN_H
    nl = w_up.shape[0]

    def inter(m):
        r = m.shape[1]
        a = jnp.pad(m[..., :FFN_H], ((0, 0), (0, 0), (0, pad))).reshape(nl, r, NK, 1, TH)
        g = jnp.pad(m[..., FFN_H:], ((0, 0), (0, 0), (0, pad))).reshape(nl, r, NK, 1, TH)
        return jnp.concatenate([a, g], axis=3).reshape(nl, r, NK * 2 * TH)

    zc = jnp.zeros((nl, D, pad), BF16)
    wa = jnp.concatenate([w_up[..., :FFN_H].astype(BF16), zc], axis=2)
    wg = jnp.concatenate([w_up[..., FFN_H:].astype(BF16), zc], axis=2)
    wd = jnp.concatenate([w_down.astype(BF16), jnp.zeros((nl, pad, D), BF16)], axis=1)
    return wa, wg, inter(conv_w), inter(conv_b[:, None, :]), wd


def kernel(x, c, ctx, c_ctx, w_ada, b_ada, norm_mix_g, norm_ffn_g, w_in, lru_conv_w, lru_conv_b, lru_gate_w, lru_gate_b, lru_lambda, na_rpb, sgu_ln_g, sgu_ln_b, sgu_w, sgu_b, w_out, ffn_up, ffn_conv_w, ffn_conv_b, ffn_down, final_norm_g):
    xall = jnp.concatenate([ctx, x], axis=1)
    cc = jnp.concatenate([c, c_ctx[None], jnp.zeros((MOD_ROWS - B - 1, D), F32)], axis=0)
    mod5 = _adaln(cc, w_ada, b_ada).reshape(DEPTH, 6, MOD_ROWS, 1, D)
    cos_t, sin_t = _rope_tables()

    g_mix = norm_mix_g.reshape(DEPTH, 1, D)
    g_ffn = norm_ffn_g.reshape(DEPTH, 1, D)
    w_in_bf = w_in.astype(BF16)
    w_out_bf = w_out.astype(BF16)
    lru_cb = lru_conv_b.reshape(DEPTH, 1, LRU_W)
    lru_gw = _gate_blockdiag(lru_gate_w)
    lru_gb = lru_gate_b.reshape(DEPTH, 4, LRU_W)
    bias = _bias_tables(na_rpb)
    sgu_g = sgu_ln_g.reshape(DEPTH, 1, SGU_W)
    sgu_bb = sgu_ln_b.reshape(DEPTH, 1, SGU_W)
    sgu_w_bf = sgu_w.astype(BF16)
    sgu_bias = jnp.repeat(jnp.swapaxes(sgu_b, 1, 2), HEAD, axis=2)
    ffn_params = _ffn_layout(ffn_up, ffn_conv_w, ffn_conv_b, ffn_down)

    for l in range(DEPTH):
        z = _inproj(xall, mod5, l, g_mix, w_in_bf)
        oa = _lru(z, l, lru_conv_w, lru_cb, lru_gw, lru_gb, lru_lambda)
        ob = _attn(z, l, cos_t, sin_t, bias)
        oc = _sgu(z, l, sgu_g, sgu_bb, sgu_w_bf, sgu_bias)
        xall = _outproj(xall, oa, ob, oc, mod5, l, w_out_bf)
        xall = _ffn(xall, mod5, l, g_ffn, *ffn_params)
    return _final_norm(xall, final_norm_g)
```

```python
import functools

import jax
import jax.numpy as jnp
from jax import lax
from jax.experimental import pallas as pl
from jax.experimental.pallas import tpu as pltpu

F32 = jnp.float32
BF16 = jnp.bfloat16

D = 2048
B = 8
SEQ = 2048
DEPTH = 4
GRID_W = 64
GRID_H = SEQ // GRID_W
CTX = 256
T = CTX + SEQ
HEAD = 64
LRU_W = 512
LRU_TAPS = 4
LRU_C = 8.0
NA_W = 1024
NA_HEADS = NA_W // HEAD
NA_ROWS = 8
NA_KC = 16
SGU_W = 512
SGU_G = SGU_W // HEAD
SGU_CHUNK = 128
PROJ = 2 * LRU_W + 3 * NA_W + 2 * SGU_W
FFN_H = 5504
EPS = 1e-6
NEG = -1e30
LOG2E = 1.4426950408889634
ROPE_BASE = 10000.0
ROPE_F = HEAD // 4
MOD_ROWS = 16
CTX_ROW = B

LANES = 128
SUBLANES = 8
VMEM_BIG = 56 * 1024 * 1024
VMEM_MID = 40 * 1024 * 1024

TM = 768
NT = T // TM
HALO = 16
TH = 512
FFN_HP = 5632
NK = FFN_HP // TH
LRU_HALF = 256
LRU_RB = 128
ATT_RB = 256
NORM_RB = 16
NORM_UNROLL = 4
FFN_RB = 128
TM_FFN = 768
TM_SGU = 1152
FFN_SLAB = 384
ATT_UNROLL = 4
ATT_GROUP = 16


def _cparams(sem, vmem=VMEM_MID):
    return pltpu.CompilerParams(dimension_semantics=sem, vmem_limit_bytes=vmem)


def _dot(a, b):
    return jnp.dot(a, b, preferred_element_type=F32)


def _dot_nt(a, b):
    return lax.dot_general(a, b, (((1,), (1,)), ((), ())), preferred_element_type=F32)


def _mod_spec(layer, which, ctx, width=D, col=None):
    def imap(*ids):
        b = ids[0]
        row = CTX_ROW if ctx else b
        c = 0 if col is None else ids[col]
        return (layer, which, row, 0, c)
    return pl.BlockSpec((None, None, None, 1, width), imap)


def _per_segment(i, tm, fn, latent_arg, ctx_arg):
    @pl.when(i == 0)
    def _():
        fn(0, CTX, ctx_arg)
        fn(CTX, tm, latent_arg)

    @pl.when(i != 0)
    def _():
        fn(0, tm, latent_arg)


def _norm_rows(x_ref, dst_ref, dst_off, lo, hi, gain, shift):
    def body(j, _):
        r = pl.multiple_of(lo + j * NORM_RB, NORM_RB)
        x = x_ref[pl.ds(r, NORM_RB), :]
        inv = lax.rsqrt(jnp.mean(x * x, axis=-1, keepdims=True) + EPS)
        dst_ref[pl.ds(dst_off + r, NORM_RB), :] = ((x * inv) * gain + shift).astype(BF16)
        return 0
    n = (hi - lo) // NORM_RB
    lax.fori_loop(0, n, body, 0, unroll=min(n, NORM_UNROLL))


def _norm_tile(i, x_ref, dst_ref, dst_off, g_ref, shb, scb, shc, scc):
    g = g_ref[...]

    def run(lo, hi, mod):
        _norm_rows(x_ref, dst_ref, dst_off, lo, hi, g * (1.0 + mod[0]), mod[1])
    _per_segment(i, x_ref.shape[0], run, (scb[...], shb[...]), (scc[...], shc[...]))


def _adaln_kernel(c_ref, w_ref, b_ref, o_ref):
    c = c_ref[...]
    s = (c * jax.nn.sigmoid(c)).astype(BF16)
    o_ref[...] = _dot(s, w_ref[...].astype(BF16)) + b_ref[...]


def _adaln(cc, w_ada, b_ada):
    tn = D // 2
    nj = D // tn
    return pl.pallas_call(
        _adaln_kernel,
        grid=(DEPTH, 6 * nj),
        in_specs=[
            pl.BlockSpec((MOD_ROWS, D), lambda l, j: (0, 0)),
            pl.BlockSpec((None, D, tn), lambda l, j: (l, 0, j)),
            pl.BlockSpec((None, 1, tn), lambda l, j: (l, 0, j)),
        ],
        out_specs=pl.BlockSpec((None, None, MOD_ROWS, tn), lambda l, j: (l, j // nj, 0, j % nj)),
        out_shape=jax.ShapeDtypeStruct((DEPTH, 6, MOD_ROWS, D), F32),
        compiler_params=_cparams(("parallel", "parallel")),
        name="adaln",
    )(cc, w_ada, b_ada.reshape(DEPTH, 1, 6 * D))


def _inproj_kernel(x_ref, g_ref, shb, scb, shc, scc, w_ref, o_ref, hx_s):
    i = pl.program_id(1)

    @pl.when(pl.program_id(2) == 0)
    def _():
        _norm_tile(i, x_ref, hx_s, 0, g_ref, shb, scb, shc, scc)

    o_ref[...] = _dot(hx_s[...], w_ref[...])


def _inproj(xall, mod5, layer, g_all, w_all):
    tn = PROJ // 2
    return pl.pallas_call(
        _inproj_kernel,
        grid=(B, NT, PROJ // tn),
        in_specs=[
            pl.BlockSpec((None, TM, D), lambda b, i, j: (b, i, 0)),
            pl.BlockSpec((None, 1, D), lambda b, i, j: (layer, 0, 0)),
            _mod_spec(layer, 0, False), _mod_spec(layer, 1, False),
            _mod_spec(layer, 0, True), _mod_spec(layer, 1, True),
            pl.BlockSpec((None, D, tn), lambda b, i, j: (layer, 0, j)),
        ],
        out_specs=pl.BlockSpec((None, TM, tn), lambda b, i, j: (b, i, j)),
        out_shape=jax.ShapeDtypeStruct((B, T, PROJ), F32),
        scratch_shapes=[pltpu.VMEM((TM, D), BF16)],
        compiler_params=_cparams(("parallel", "parallel", "arbitrary"), VMEM_BIG),
        name="inproj",
    )(xall, g_all, mod5, mod5, mod5, mod5, w_all)


def _tile_scan(a, b, carry, rev):
    rid = lax.broadcasted_iota(jnp.int32, a.shape, 0)
    for d in (1, 2, 4):
        sh = SUBLANES - d if rev else d
        keep = (rid < SUBLANES - d) if rev else (rid >= d)
        a_s = jnp.where(keep, pltpu.roll(a, sh, 0), 1.0)
        b_s = jnp.where(keep, pltpu.roll(b, sh, 0), 0.0)
        b = a * b_s + b
        a = a * a_s
    h = a * carry + b
    return h, (h[0:1] if rev else h[SUBLANES - 1:SUBLANES])


def _lru_kernel(ax_ref, ay_ref, cw_ref, cb_ref, gw_ref, gb_ref, lam_ref, o_ref, af, bf, ab, bb):
    W = LRU_HALF
    cw = cw_ref[...]
    cb = cb_ref[...]
    gb = gb_ref[...]
    nl = -lam_ref[...]
    softplus = jnp.maximum(nl, 0.0) + jnp.log(1.0 + jnp.exp(-jnp.abs(nl)))
    coef = -LRU_C * softplus
    zeros8 = jnp.zeros((SUBLANES, W), F32)
    n_ext = LRU_RB + 2 * SUBLANES

    for blk in range(T // LRU_RB):
        r0 = blk * LRU_RB
        r1 = r0 + LRU_RB
        prev = zeros8 if r0 in (0, CTX) else ax_ref[r0 - SUBLANES:r0, :]
        nxt = zeros8 if r1 in (CTX, T) else ax_ref[r1:r1 + SUBLANES, :]
        ext = jnp.concatenate([prev, ax_ref[r0:r1, :], nxt], axis=0)
        lo, hi = SUBLANES, SUBLANES + LRU_RB
        xc = cb + pltpu.roll(ext, 2, 0)[lo:hi] * cw[0:1]
        xc = xc + pltpu.roll(ext, 1, 0)[lo:hi] * cw[1:2]
        xc = xc + ext[lo:hi] * cw[2:3]
        xc = xc + pltpu.roll(ext, n_ext - 1, 0)[lo:hi] * cw[3:4]
        xb = xc.astype(BF16)
        for d, (a_s, b_s) in enumerate(((af, bf), (ab, bb))):
            r = jax.nn.sigmoid(_dot(xb, gw_ref[2 * d]) + gb[2 * d:2 * d + 1])
            gi = jax.nn.sigmoid(_dot(xb, gw_ref[2 * d + 1]) + gb[2 * d + 1:2 * d + 2])
            log_a = coef[d:d + 1] * r
            a_s[r0:r1, :] = jnp.exp(log_a)
            th = jnp.tanh(log_a)
            b_s[r0:r1, :] = jnp.sqrt(-2.0 * th / (1.0 - th)) * (gi * xc)

    def make_body(f_base, b_top):
        def body(k, carry):
            cf, cr = carry
            rf = pl.multiple_of((f_base + k) * SUBLANES, SUBLANES)
            hf, cf = _tile_scan(af[pl.ds(rf, SUBLANES), :], bf[pl.ds(rf, SUBLANES), :], cf, False)
            bf[pl.ds(rf, SUBLANES), :] = hf
            rb = pl.multiple_of((b_top - k) * SUBLANES, SUBLANES)
            hb, cr = _tile_scan(ab[pl.ds(rb, SUBLANES), :], bb[pl.ds(rb, SUBLANES), :], cr, True)
            bb[pl.ds(rb, SUBLANES), :] = hb
            return cf, cr
        return body

    z1 = jnp.zeros((1, W), F32)
    n_ctx = CTX // SUBLANES
    n_all = T // SUBLANES
    carry = lax.fori_loop(0, n_ctx, make_body(0, n_ctx - 1), (z1, z1))
    lax.fori_loop(0, n_all - n_ctx, make_body(n_ctx, n_all - 1), carry)

    for blk in range(T // LRU_RB):
        r0 = blk * LRU_RB
        r1 = r0 + LRU_RB
        h = bf[r0:r1, :] + bb[r0:r1, :]
        o_ref[r0:r1, :] = (h * jax.nn.gelu(ay_ref[r0:r1, :])).astype(BF16)


def _lru(z, layer, cw, cb, gw_bd, gb, lam):
    W = LRU_HALF
    nh = LRU_W // W
    return pl.pallas_call(
        _lru_kernel,
        grid=(B, nh),
        in_specs=[
            pl.BlockSpec((None, T, W), lambda b, c: (b, 0, c)),
            pl.BlockSpec((None, T, W), lambda b, c: (b, 0, nh + c)),
            pl.BlockSpec((None, LRU_TAPS, W), lambda b, c: (layer, 0, c)),
            pl.BlockSpec((None, 1, W), lambda b, c: (layer, 0, c)),
            pl.BlockSpec((None, None, 4, W, W), lambda b, c: (layer, c, 0, 0, 0)),
            pl.BlockSpec((None, 4, W), lambda b, c: (layer, 0, c)),
            pl.BlockSpec((None, 2, W), lambda b, c: (layer, 0, c)),
        ],
        out_specs=pl.BlockSpec((None, T, W), lambda b, c: (b, 0, c)),
        out_shape=jax.ShapeDtypeStruct((B, T, LRU_W), BF16),
        scratch_shapes=[pltpu.VMEM((T, W), F32)] * 4,
        compiler_params=_cparams(("parallel", "parallel")),
        name="lru",
    )(z, z, cw, cb, gw_bd, gb, lam)


def _attn_kernel(q_ref, k_ref, v_ref, cos_ref, sin_ref, bias_ref, o_ref,
                 qraw_s, qrot_s, qctx_s, krot_s, kc_s, v_s, sc_s, pc_s, ow_s):
    lane = lax.broadcasted_iota(jnp.int32, (1, LANES), 1)
    head0 = lane < HEAD
    second16 = ((lane // ROPE_F) % 2) == 1
    scale = HEAD ** -0.5 * LOG2E
    W2 = 2 * GRID_W

    def rope(x, c, s):
        partner = jnp.where(second16, pltpu.roll(x, ROPE_F, 1), pltpu.roll(x, LANES - ROPE_F, 1))
        return x * c + partner * s

    def stack_heads(x):
        return jnp.concatenate([jnp.where(head0, x, 0.0), jnp.where(head0, 0.0, x)], axis=0).astype(BF16)

    def unstack_heads(y):
        n = y.shape[0] // 2
        return jnp.where(head0, y[:n], y[n:])

    qctx_s[...] = stack_heads(q_ref[0:CTX, :] * scale)
    kc_s[...] = k_ref[0:CTX, :].astype(BF16)
    v_s[0:CTX, :] = v_ref[0:CTX, :].astype(BF16)
    rows_per_blk = ATT_RB // GRID_W
    for blk in range(SEQ // ATT_RB):
        r0 = blk * ATT_RB
        r1 = r0 + ATT_RB
        c = cos_ref[r0:r1, :]
        s = sin_ref[r0:r1, :]
        q = q_ref[CTX + r0:CTX + r1, :]
        qs = q * scale
        qr = rope(q, c, s) * scale
        for j in range(rows_per_blk):
            qraw_s[blk * rows_per_blk + j] = stack_heads(qs[j * GRID_W:(j + 1) * GRID_W])
            qrot_s[blk * rows_per_blk + j] = stack_heads(qr[j * GRID_W:(j + 1) * GRID_W])
        krot_s[r0:r1, :] = rope(k_ref[CTX + r0:CTX + r1, :], c, s).astype(BF16)
        v_s[CTX + r0:CTX + r1, :] = v_ref[CTX + r0:CTX + r1, :].astype(BF16)

    s = _dot_nt(qctx_s[...], kc_s[...])
    e = jnp.exp2(s - jnp.max(s, axis=-1, keepdims=True))
    p = (e / jnp.sum(e, axis=-1, keepdims=True)).astype(BF16)
    o_ref[0:CTX, :] = unstack_heads(_dot(p, v_s[0:CTX, :])).astype(BF16)

    for g in range(GRID_H // ATT_GROUP):
        g0, g1 = g * ATT_GROUP, (g + 1) * ATT_GROUP
        sc = _dot_nt(qraw_s[g0:g1].reshape(ATT_GROUP * W2, LANES), kc_s[...])
        sc_s[g0:g1] = sc.reshape(ATT_GROUP, W2, CTX)

    win = NA_ROWS * GRID_W

    def lane_fold(op, *xs):
        cols = [x[:, c:c + LANES] for x in xs for c in range(0, x.shape[1], LANES)]
        acc = cols[0]
        for col in cols[1:]:
            acc = op(acc, col)
        return acc

    def body(t, _):
        rows = [t * ATT_UNROLL + j for j in range(ATT_UNROLL)]
        starts = [jnp.clip(r - NA_ROWS // 2, 0, GRID_H - NA_ROWS) for r in rows]
        sw = [_dot_nt(qrot_s[r], krot_s[pl.ds(pl.multiple_of(rs * GRID_W, GRID_W), win), :]) + bias_ref[r - rs]
              for r, rs in zip(rows, starts)]
        sc = [sc_s[r] for r in rows]
        m = [jnp.max(lane_fold(jnp.maximum, a, c), axis=-1, keepdims=True) for a, c in zip(sw, sc)]
        ew = [jnp.exp2(a - mm) for a, mm in zip(sw, m)]
        ec = [jnp.exp2(c - mm) for c, mm in zip(sc, m)]
        inv = [1.0 / jnp.sum(lane_fold(jnp.add, a, c), axis=-1, keepdims=True) for a, c in zip(ew, ec)]
        for r, rs, a, c, iv in zip(rows, starts, ew, ec, inv):
            kx = pl.multiple_of(CTX + rs * GRID_W, GRID_W)
            ow_s[r] = _dot(a.astype(BF16), v_s[pl.ds(kx, win), :]) * iv
            pc_s[r] = (c * iv).astype(BF16)
        return 0

    lax.fori_loop(0, GRID_H // ATT_UNROLL, body, 0)

    for g in range(GRID_H // ATT_GROUP):
        g0, g1 = g * ATT_GROUP, (g + 1) * ATT_GROUP
        oc = _dot(pc_s[g0:g1].reshape(ATT_GROUP * W2, CTX), v_s[0:CTX, :]).reshape(ATT_GROUP, W2, LANES)
        tot = oc + ow_s[g0:g1]
        for j in range(ATT_GROUP):
            row = CTX + (g0 + j) * GRID_W
            o_ref[row:row + GRID_W, :] = unstack_heads(tot[j]).astype(BF16)


def _attn(z, layer, cos_t, sin_t, biasmask):
    nhp = NA_W // LANES
    qb = 2 * LRU_W // LANES
    return pl.pallas_call(
        _attn_kernel,
        grid=(nhp, B),
        in_specs=[
            pl.BlockSpec((None, T, LANES), lambda p, b: (b, 0, qb + p)),
            pl.BlockSpec((None, T, LANES), lambda p, b: (b, 0, qb + nhp + p)),
            pl.BlockSpec((None, T, LANES), lambda p, b: (b, 0, qb + 2 * nhp + p)),
            pl.BlockSpec((SEQ, LANES), lambda p, b: (0, 0)),
            pl.BlockSpec((SEQ, LANES), lambda p, b: (0, 0)),
            pl.BlockSpec((None, None, NA_ROWS, 2 * GRID_W, NA_ROWS * GRID_W), lambda p, b: (layer, p, 0, 0, 0)),
        ],
        out_specs=pl.BlockSpec((None, T, LANES), lambda p, b: (b, 0, p)),
        out_shape=jax.ShapeDtypeStruct((B, T, NA_W), BF16),
        scratch_shapes=[
            pltpu.VMEM((GRID_H, 2 * GRID_W, LANES), BF16),
            pltpu.VMEM((GRID_H, 2 * GRID_W, LANES), BF16),
            pltpu.VMEM((2 * CTX, LANES), BF16),
            pltpu.VMEM((SEQ, LANES), BF16),
            pltpu.VMEM((CTX, LANES), BF16),
            pltpu.VMEM((T, LANES), BF16),
            pltpu.VMEM((GRID_H, 2 * GRID_W, CTX), F32),
            pltpu.VMEM((GRID_H, 2 * GRID_W, CTX), BF16),
            pltpu.VMEM((GRID_H, 2 * GRID_W, LANES), F32),
        ],
        compiler_params=_cparams(("parallel", "parallel")),
        name="attn",
    )(z, z, z, cos_t, sin_t, biasmask)


def _sgu_kernel(u_ref, v_ref, g_ref, b_ref, ws_ref, bs_ref, o_ref):
    lane = lax.broadcasted_iota(jnp.int32, (1, LANES), 1)
    first = lane < HEAD
    for n in range(u_ref.shape[0] // SGU_CHUNK):
        r0 = n * SGU_CHUNK
        r1 = r0 + SGU_CHUNK
        v = jax.nn.gelu(v_ref[r0:r1, :])
        mu = jnp.mean(v, axis=-1, keepdims=True)
        var = jnp.mean(jnp.square(v - mu), axis=-1, keepdims=True)
        vn = ((v - mu) * lax.rsqrt(var + EPS) * g_ref[...] + b_ref[...]).astype(BF16)
        for j in range(SGU_W // LANES):
            c0 = j * LANES
            c1 = c0 + LANES
            vp = vn[:, c0:c1]
            mixed = jnp.where(first, _dot(ws_ref[2 * j], vp), _dot(ws_ref[2 * j + 1], vp)) + bs_ref[:, c0:c1]
            o_ref[r0:r1, c0:c1] = (jax.nn.gelu(u_ref[r0:r1, c0:c1]) * mixed).astype(BF16)


def _sgu(z, layer, ln_g, ln_b, ws_bf, bs_full):
    ub = (2 * LRU_W + 3 * NA_W) // SGU_W
    return pl.pallas_call(
        _sgu_kernel,
        grid=(B, T // TM_SGU),
        in_specs=[
            pl.BlockSpec((None, TM_SGU, SGU_W), lambda b, i: (b, i, ub)),
            pl.BlockSpec((None, TM_SGU, SGU_W), lambda b, i: (b, i, ub + 1)),
            pl.BlockSpec((None, 1, SGU_W), lambda b, i: (layer, 0, 0)),
            pl.BlockSpec((None, 1, SGU_W), lambda b, i: (layer, 0, 0)),
            pl.BlockSpec((None, SGU_G, SGU_CHUNK, SGU_CHUNK), lambda b, i: (layer, 0, 0, 0)),
            pl.BlockSpec((None, SGU_CHUNK, SGU_W), lambda b, i: (layer, 0, 0)),
        ],
        out_specs=pl.BlockSpec((None, TM_SGU, SGU_W), lambda b, i: (b, i, 0)),
        out_shape=jax.ShapeDtypeStruct((B, T, SGU_W), BF16),
        compiler_params=_cparams(("parallel", "parallel")),
        name="sgu",
    )(z, z, ln_g, ln_b, ws_bf, bs_full)


def _outproj_kernel(x_ref, a_ref, b_ref, c_ref, gtb, gtc, w_ref, o_ref):
    i = pl.program_id(1)
    acc = _dot(a_ref[...], w_ref[0:LRU_W, :])
    acc = acc + _dot(b_ref[...], w_ref[LRU_W:LRU_W + NA_W, :])
    acc = acc + _dot(c_ref[...], w_ref[LRU_W + NA_W:D, :])
    is_ctx = (lax.broadcasted_iota(jnp.int32, (TM, 1), 0) + i * TM) < CTX
    o_ref[...] = x_ref[...] + jnp.where(is_ctx, gtc[...], gtb[...]) * acc


def _outproj(xall, oa, ob, oc, mod5, layer, w_all):
    tn = D
    return pl.pallas_call(
        _outproj_kernel,
        grid=(B, NT, D // tn),
        in_specs=[
            pl.BlockSpec((None, TM, tn), lambda b, i, j: (b, i, j)),
            pl.BlockSpec((None, TM, LRU_W), lambda b, i, j: (b, i, 0)),
            pl.BlockSpec((None, TM, NA_W), lambda b, i, j: (b, i, 0)),
            pl.BlockSpec((None, TM, SGU_W), lambda b, i, j: (b, i, 0)),
            _mod_spec(layer, 2, False, tn, 2), _mod_spec(layer, 2, True, tn, 2),
            pl.BlockSpec((None, D, tn), lambda b, i, j: (layer, 0, j), pipeline_mode=pl.Buffered(1)),
        ],
        out_specs=pl.BlockSpec((None, TM, tn), lambda b, i, j: (b, i, j)),
        out_shape=jax.ShapeDtypeStruct((B, T, D), F32),
        compiler_params=_cparams(("parallel", "parallel", "parallel"), VMEM_BIG),
        name="outproj",
    )(xall, oa, ob, oc, mod5, mod5, w_all)


def _ffn_kernel(x_ref, xp_ref, xn_ref, g_ref, shb, scb, gtb, shc, scc, gtc, wa_ref, wg_ref, cw_ref, cb_ref, wd_ref,
                o_ref, hx_s, u_s, act_s):
    i = pl.program_id(1)
    k = pl.program_id(2)
    tm = x_ref.shape[0]
    rb = FFN_RB
    left_blocks = (0, CTX // rb)
    right_blocks = (CTX // rb - 1, tm // rb - 1)

    @pl.when(k == 0)
    def _():
        gain_b = g_ref[...] * (1.0 + scb[...])

        def normed(x):
            return (x * lax.rsqrt(jnp.mean(x * x, axis=-1, keepdims=True) + EPS)) * gain_b + shb[...]
        hx_s[0:HALO, :] = jnp.concatenate([normed(xn_ref[...]), normed(xp_ref[...])], axis=0).astype(BF16)
        _norm_tile(i, x_ref, hx_s, HALO, g_ref, shb, scb, shc, scc)
        o_ref[...] = jnp.zeros_like(o_ref)

    slab = FFN_SLAB
    for h in range(tm // slab):
        q0 = 0 if h == 0 else h * slab + 2 * HALO
        q1 = min((h + 1) * slab + 2 * HALO, HALO + tm)
        u_s[q0:q1, :TH] = _dot(hx_s[q0:q1, :], wa_ref[...])
        u_s[q0:q1, TH:] = _dot(hx_s[q0:q1, :], wg_ref[...])
    u_s[HALO + tm:HALO + tm + SUBLANES, :] = u_s[0:SUBLANES, :]
    cw = cw_ref[...]
    cb = cb_ref[...]
    for blk in range(tm // rb):
        r0 = HALO + blk * rb
        rows = lax.broadcasted_iota(jnp.int32, (rb, 1), 0) + (i * tm + blk * rb)
        ul = u_s[r0 - 1:r0 - 1 + rb, :]
        ur = u_s[r0 + 1:r0 + 1 + rb, :]
        if blk in left_blocks:
            ul = jnp.where(jnp.logical_and(rows != 0, rows != CTX), ul, 0.0)
        if blk in right_blocks:
            ur = jnp.where(jnp.logical_and(rows != CTX - 1, rows != T - 1), ur, 0.0)
        y = cb + ul * cw[0:1]
        y = y + u_s[r0:r0 + rb, :] * cw[1:2]
        y = y + ur * cw[2:3]
        a = y[:, :TH]
        gg = y[:, TH:]
        act_s[blk * rb:(blk + 1) * rb, :] = (gg * jax.nn.sigmoid(gg) * a).astype(BF16)
        if (blk + 1) * rb % slab == 0:
            s0 = (blk + 1) * rb - slab
            o_ref[s0:s0 + slab, :] += _dot(act_s[s0:s0 + slab, :], wd_ref[...])

    @pl.when(k == NK - 1)
    def _():
        def finish(lo, hi, gate):
            o_ref[lo:hi, :] = x_ref[lo:hi, :] + gate * o_ref[lo:hi, :]
        _per_segment(i, tm, finish, gtb[...], gtc[...])


def _ffn(xall, mod5, layer, g, wa, wg, cw_r, cb_r, wd_p):
    tm = TM_FFN
    hb = tm // SUBLANES
    last = T // SUBLANES - 1
    col_map = lambda b, i, k: (layer, 0, k)
    return pl.pallas_call(
        _ffn_kernel,
        grid=(B, T // tm, NK),
        in_specs=[
            pl.BlockSpec((None, tm, D), lambda b, i, k: (b, i, 0)),
            pl.BlockSpec((None, SUBLANES, D), lambda b, i, k: (b, jnp.maximum(i * hb - 1, 0), 0)),
            pl.BlockSpec((None, SUBLANES, D), lambda b, i, k: (b, jnp.minimum((i + 1) * hb, last), 0)),
            pl.BlockSpec((None, 1, D), lambda b, i, k: (layer, 0, 0)),
            _mod_spec(layer, 3, False), _mod_spec(layer, 4, False), _mod_spec(layer, 5, False),
            _mod_spec(layer, 3, True), _mod_spec(layer, 4, True), _mod_spec(layer, 5, True),
            pl.BlockSpec((None, D, TH), col_map),
            pl.BlockSpec((None, D, TH), col_map),
            pl.BlockSpec((None, 3, 2 * TH), col_map),
            pl.BlockSpec((None, 1, 2 * TH), col_map),
            pl.BlockSpec((None, TH, D), lambda b, i, k: (layer, k, 0)),
        ],
        out_specs=pl.BlockSpec((None, tm, D), lambda b, i, k: (b, i, 0)),
        out_shape=jax.ShapeDtypeStruct((B, T, D), F32),
        scratch_shapes=[
            pltpu.VMEM((HALO + tm, D), BF16),
            pltpu.VMEM((HALO + tm + SUBLANES, 2 * TH), F32),
            pltpu.VMEM((tm, TH), BF16),
        ],
        compiler_params=_cparams(("parallel", "parallel", "arbitrary"), VMEM_BIG),
        name="ffn",
    )(xall, xall, xall, g, mod5, mod5, mod5, mod5, mod5, mod5, wa, wg, cw_r, cb_r, wd_p)


def _final_kernel(x_ref, g_ref, o_ref):
    x = x_ref[...]
    o_ref[...] = x * lax.rsqrt(jnp.mean(x * x, axis=-1, keepdims=True) + EPS) * g_ref[...]


def _final_norm(xall, g):
    tm = CTX
    return pl.pallas_call(
        _final_kernel,
        grid=(B, SEQ // tm),
        in_specs=[
            pl.BlockSpec((None, tm, D), lambda b, i: (b, i + 1, 0)),
            pl.BlockSpec((1, D), lambda b, i: (0, 0)),
        ],
        out_specs=pl.BlockSpec((None, tm, D), lambda b, i: (b, i, 0)),
        out_shape=jax.ShapeDtypeStruct((B, SEQ, D), F32),
        compiler_params=_cparams(("parallel", "parallel")),
        name="final_norm",
    )(xall, g.reshape(1, D))


def _rope_tables():
    t = jnp.arange(SEQ)
    pos = jnp.stack([t // GRID_W, t % GRID_W], axis=-1).astype(F32)
    inv = ROPE_BASE ** (-jnp.arange(ROPE_F, dtype=F32) / ROPE_F)
    ang = pos[:, :, None] * inv
    cos, sin = jnp.cos(ang), jnp.sin(ang)
    cos_h = jnp.concatenate([cos[:, 0], cos[:, 0], cos[:, 1], cos[:, 1]], axis=-1)
    sin_h = jnp.concatenate([-sin[:, 0], sin[:, 0], -sin[:, 1], sin[:, 1]], axis=-1)
    reps = LANES // HEAD
    return jnp.tile(cos_h, (1, reps)), jnp.tile(sin_h, (1, reps))


def _bias_tables(rpb):
    nl = rpb.shape[0]
    nr, nc = 2 * NA_ROWS - 1, 2 * NA_KC - 1
    pairs = rpb.reshape(nl, NA_HEADS // 2, 2, nr, nc) * LOG2E
    pairs = jnp.pad(pairs, ((0, 0), (0, 0), (0, 0), (0, 2 * NA_ROWS - nr), (0, LANES - nc)))
    return pl.pallas_call(
        _bias_kernel,
        grid=(nl, NA_HEADS // 2),
        in_specs=[pl.BlockSpec((None, None, 2, 2 * NA_ROWS, LANES), lambda l, p: (l, p, 0, 0, 0))],
        out_specs=pl.BlockSpec((None, None, NA_ROWS, 2 * GRID_W, NA_ROWS * GRID_W), lambda l, p: (l, p, 0, 0, 0)),
        out_shape=jax.ShapeDtypeStruct((nl, NA_HEADS // 2, NA_ROWS, 2 * GRID_W, NA_ROWS * GRID_W), F32),
        compiler_params=_cparams(("parallel", "parallel")),
        name="bias_table",
    )(pairs)


def _bias_kernel(rp_ref, o_ref):
    c = lax.broadcasted_iota(jnp.int32, (GRID_W, LANES), 0)
    lane = lax.broadcasted_iota(jnp.int32, (GRID_W, LANES), 1)
    kc = lane % GRID_W
    col_start = jnp.clip(c - NA_KC // 2, 0, GRID_W - NA_KC)
    in_win = jnp.logical_and(kc >= col_start, kc < col_start + NA_KC)
    first = lane < GRID_W
    per_blk = LANES // GRID_W
    for cls in range(NA_ROWS):
        for hd in range(2):
            for blk in range(NA_ROWS // per_blk):
                parts = []
                for h in range(per_blk):
                    m = blk * per_blk + h - cls + NA_ROWS - 1
                    row = jnp.broadcast_to(rp_ref[hd, m:m + 1, :], (GRID_W, LANES))
                    shift = (LANES - (NA_KC - 1) + h * GRID_W) % LANES
                    parts.append(pltpu.roll(row, shift, 1, stride=1, stride_axis=0))
                tile = jnp.where(in_win, jnp.where(first, parts[0], parts[1]), NEG)
                o_ref[cls, hd * GRID_W:(hd + 1) * GRID_W, blk * LANES:(blk + 1) * LANES] = tile


def _gate_blockdiag(gate_w):
    nl = gate_w.shape[0]
    nh = LRU_W // LRU_HALF
    gper = LRU_HALF // HEAD
    w = gate_w.reshape(nl, 4, nh, gper, HEAD, HEAD)
    eye = jnp.eye(gper, dtype=gate_w.dtype)
    bd = jnp.einsum('lkhgio,gj->lkhgijo', w, eye).reshape(nl, 4, nh, LRU_HALF, LRU_HALF)
    return jnp.transpose(bd, (0, 2, 1, 3, 4)).astype(BF16)


def _ffn_layout(w_up, conv_w, conv_b, w_down):
    pad = FFN_HP - FFN_H
    nl = w_up.shape[0]

    def inter(m):
        r = m.shape[1]
        a = jnp.pad(m[..., :FFN_H], ((0, 0), (0, 0), (0, pad))).reshape(nl, r, NK, 1, TH)
        g = jnp.pad(m[..., FFN_H:], ((0, 0), (0, 0), (0, pad))).reshape(nl, r, NK, 1, TH)
        return jnp.concatenate([a, g], axis=3).reshape(nl, r, NK * 2 * TH)

    zc = jnp.zeros((nl, D, pad), BF16)
    wa = jnp.concatenate([w_up[..., :FFN_H].astype(BF16), zc], axis=2)
    wg = jnp.concatenate([w_up[..., FFN_H:].astype(BF16), zc], axis=2)
    wd = jnp.concatenate([w_down.astype(BF16), jnp.zeros((nl, pad, D), BF16)], axis=1)
    return wa, wg, inter(conv_w), inter(conv_b[:, None, :]), wd


def kernel(x, c, ctx, c_ctx, w_ada, b_ada, norm_mix_g, norm_ffn_g, w_in, lru_conv_w, lru_conv_b, lru_gate_w, lru_gate_b, lru_lambda, na_rpb, sgu_ln_g, sgu_ln_b, sgu_w, sgu_b, w_out, ffn_up, ffn_conv_w, ffn_conv_b, ffn_down, final_norm_g):
    xall = jnp.concatenate([ctx, x], axis=1)
    cc = jnp.concatenate([c, c_ctx[None], jnp.zeros((MOD_ROWS - B - 1, D), F32)], axis=0)
    mod5 = _adaln(cc, w_ada, b_ada).reshape(DEPTH, 6, MOD_ROWS, 1, D)
    cos_t, sin_t = _rope_tables()

    g_mix = norm_mix_g.reshape(DEPTH, 1, D)
    g_ffn = norm_ffn_g.reshape(DEPTH, 1, D)
    w_in_bf = w_in.astype(BF16)
    w_out_bf = w_out.astype(BF16)
    lru_cb = lru_conv_b.reshape(DEPTH, 1, LRU_W)
    lru_gw = _gate_blockdiag(lru_gate_w)
    lru_gb = lru_gate_b.reshape(DEPTH, 4, LRU_W)
    bias = _bias_tables(na_rpb)
    sgu_g = sgu_ln_g.reshape(DEPTH, 1, SGU_W)
    sgu_bb = sgu_ln_b.reshape(DEPTH, 1, SGU_W)
    sgu_w_bf = sgu_w.astype(BF16)
    sgu_bias = jnp.repeat(jnp.swapaxes(sgu_b, 1, 2), HEAD, axis=2)
    ffn_params = _ffn_layout(ffn_up, ffn_conv_w, ffn_conv_b, ffn_down)

    for l in range(DEPTH):
        z = _inproj(xall, mod5, l, g_mix, w_in_bf)
        oa = _lru(z, l, lru_conv_w, lru_cb, lru_gw, lru_gb, lru_lambda)
        ob = _attn(z, l, cos_t, sin_t, bias)
        oc = _sgu(z, l, sgu_g, sgu_bb, sgu_w_bf, sgu_bias)
        xall = _outproj(xall, oa, ob, oc, mod5, l, w_out_bf)
        xall = _ffn(xall, mod5, l, g_ffn, *ffn_params)
    return _final_norm(xall, final_norm_g)
```

```python
import functools

import jax
import jax.numpy as jnp
from jax import lax
from jax.experimental import pallas as pl
from jax.experimental.pallas import tpu as pltpu

F32 = jnp.float32
BF16 = jnp.bfloat16

D = 2048
B = 8
SEQ = 2048
DEPTH = 4
GRID_W = 64
GRID_H = SEQ // GRID_W
CTX = 256
T = CTX + SEQ
HEAD = 64
LRU_W = 512
LRU_TAPS = 4
LRU_C = 8.0
NA_W = 1024
NA_HEADS = NA_W // HEAD
NA_ROWS = 8
NA_KC = 16
SGU_W = 512
SGU_G = SGU_W // HEAD
SGU_CHUNK = 128
PROJ = 2 * LRU_W + 3 * NA_W + 2 * SGU_W
FFN_H = 5504
EPS = 1e-6
NEG = -1e30
LOG2E = 1.4426950408889634
ROPE_BASE = 10000.0
ROPE_F = HEAD // 4
MOD_ROWS = 16
CTX_ROW = B

LANES = 128
SUBLANES = 8
VMEM_BIG = 56 * 1024 * 1024
VMEM_MID = 40 * 1024 * 1024

TM = 768
NT = T // TM
HALO = 16
TH = 512
FFN_HP = 5632
NK = FFN_HP // TH
LRU_HALF = 256
LRU_RB = 128
ATT_RB = 256
NORM_RB = 16
NORM_UNROLL = 4
FFN_RB = 128
TM_FFN = 768
TM_SGU = 1152
FFN_SLAB = 256
ATT_UNROLL = 4
ATT_GROUP = 16


def _cparams(sem, vmem=VMEM_MID):
    return pltpu.CompilerParams(dimension_semantics=sem, vmem_limit_bytes=vmem)


def _dot(a, b):
    return jnp.dot(a, b, preferred_element_type=F32)


def _dot_nt(a, b):
    return lax.dot_general(a, b, (((1,), (1,)), ((), ())), preferred_element_type=F32)


def _mod_spec(layer, which, ctx, width=D, col=None):
    def imap(*ids):
        b = ids[0]
        row = CTX_ROW if ctx else b
        c = 0 if col is None else ids[col]
        return (layer, which, row, 0, c)
    return pl.BlockSpec((None, None, None, 1, width), imap)


def _per_segment(i, tm, fn, latent_arg, ctx_arg):
    @pl.when(i == 0)
    def _():
        fn(0, CTX, ctx_arg)
        fn(CTX, tm, latent_arg)

    @pl.when(i != 0)
    def _():
        fn(0, tm, latent_arg)


def _norm_rows(x_ref, dst_ref, dst_off, lo, hi, gain, shift):
    def body(j, _):
        r = pl.multiple_of(lo + j * NORM_RB, NORM_RB)
        x = x_ref[pl.ds(r, NORM_RB), :]
        inv = lax.rsqrt(jnp.mean(x * x, axis=-1, keepdims=True) + EPS)
        dst_ref[pl.ds(dst_off + r, NORM_RB), :] = ((x * inv) * gain + shift).astype(BF16)
        return 0
    n = (hi - lo) // NORM_RB
    lax.fori_loop(0, n, body, 0, unroll=min(n, NORM_UNROLL))


def _norm_tile(i, x_ref, dst_ref, dst_off, g_ref, shb, scb, shc, scc):
    g = g_ref[...]

    def run(lo, hi, mod):
        _norm_rows(x_ref, dst_ref, dst_off, lo, hi, g * (1.0 + mod[0]), mod[1])
    _per_segment(i, x_ref.shape[0], run, (scb[...], shb[...]), (scc[...], shc[...]))


def _adaln_kernel(c_ref, w_ref, b_ref, o_ref):
    c = c_ref[...]
    s = (c * jax.nn.sigmoid(c)).astype(BF16)
    o_ref[...] = _dot(s, w_ref[...].astype(BF16)) + b_ref[...]


def _adaln(cc, w_ada, b_ada):
    tn = D // 2
    nj = D // tn
    return pl.pallas_call(
        _adaln_kernel,
        grid=(DEPTH, 6 * nj),
        in_specs=[
            pl.BlockSpec((MOD_ROWS, D), lambda l, j: (0, 0)),
            pl.BlockSpec((None, D, tn), lambda l, j: (l, 0, j)),
            pl.BlockSpec((None, 1, tn), lambda l, j: (l, 0, j)),
        ],
        out_specs=pl.BlockSpec((None, None, MOD_ROWS, tn), lambda l, j: (l, j // nj, 0, j % nj)),
        out_shape=jax.ShapeDtypeStruct((DEPTH, 6, MOD_ROWS, D), F32),
        compiler_params=_cparams(("parallel", "parallel")),
        name="adaln",
    )(cc, w_ada, b_ada.reshape(DEPTH, 1, 6 * D))


def _inproj_kernel(x_ref, g_ref, shb, scb, shc, scc, w_ref, o_ref, hx_s):
    i = pl.program_id(1)

    @pl.when(pl.program_id(2) == 0)
    def _():
        _norm_tile(i, x_ref, hx_s, 0, g_ref, shb, scb, shc, scc)

    o_ref[...] = _dot(hx_s[...], w_ref[...])


def _inproj(xall, mod5, layer, g_all, w_all):
    tn = PROJ // 2
    return pl.pallas_call(
        _inproj_kernel,
        grid=(B, NT, PROJ // tn),
        in_specs=[
            pl.BlockSpec((None, TM, D), lambda b, i, j: (b, i, 0)),
            pl.BlockSpec((None, 1, D), lambda b, i, j: (layer, 0, 0)),
            _mod_spec(layer, 0, False), _mod_spec(layer, 1, False),
            _mod_spec(layer, 0, True), _mod_spec(layer, 1, True),
            pl.BlockSpec((None, D, tn), lambda b, i, j: (layer, 0, j)),
        ],
        out_specs=pl.BlockSpec((None, TM, tn), lambda b, i, j: (b, i, j)),
        out_shape=jax.ShapeDtypeStruct((B, T, PROJ), F32),
        scratch_shapes=[pltpu.VMEM((TM, D), BF16)],
        compiler_params=_cparams(("parallel", "parallel", "arbitrary"), VMEM_BIG),
        name="inproj",
    )(xall, g_all, mod5, mod5, mod5, mod5, w_all)


def _tile_scan(a, b, carry, rev):
    rid = lax.broadcasted_iota(jnp.int32, a.shape, 0)
    for d in (1, 2, 4):
        sh = SUBLANES - d if rev else d
        keep = (rid < SUBLANES - d) if rev else (rid >= d)
        a_s = jnp.where(keep, pltpu.roll(a, sh, 0), 1.0)
        b_s = jnp.where(keep, pltpu.roll(b, sh, 0), 0.0)
        b = a * b_s + b
        a = a * a_s
    h = a * carry + b
    return h, (h[0:1] if rev else h[SUBLANES - 1:SUBLANES])


def _lru_kernel(ax_ref, ay_ref, cw_ref, cb_ref, gw_ref, gb_ref, lam_ref, o_ref, af, bf, ab, bb):
    W = LRU_HALF
    cw = cw_ref[...]
    cb = cb_ref[...]
    gb = gb_ref[...]
    nl = -lam_ref[...]
    softplus = jnp.maximum(nl, 0.0) + jnp.log(1.0 + jnp.exp(-jnp.abs(nl)))
    coef = -LRU_C * softplus
    zeros8 = jnp.zeros((SUBLANES, W), F32)
    n_ext = LRU_RB + 2 * SUBLANES

    for blk in range(T // LRU_RB):
        r0 = blk * LRU_RB
        r1 = r0 + LRU_RB
        prev = zeros8 if r0 in (0, CTX) else ax_ref[r0 - SUBLANES:r0, :]
        nxt = zeros8 if r1 in (CTX, T) else ax_ref[r1:r1 + SUBLANES, :]
        ext = jnp.concatenate([prev, ax_ref[r0:r1, :], nxt], axis=0)
        lo, hi = SUBLANES, SUBLANES + LRU_RB
        xc = cb + pltpu.roll(ext, 2, 0)[lo:hi] * cw[0:1]
        xc = xc + pltpu.roll(ext, 1, 0)[lo:hi] * cw[1:2]
        xc = xc + ext[lo:hi] * cw[2:3]
        xc = xc + pltpu.roll(ext, n_ext - 1, 0)[lo:hi] * cw[3:4]
        xb = xc.astype(BF16)
        for d, (a_s, b_s) in enumerate(((af, bf), (ab, bb))):
            r = jax.nn.sigmoid(_dot(xb, gw_ref[2 * d]) + gb[2 * d:2 * d + 1])
            gi = jax.nn.sigmoid(_dot(xb, gw_ref[2 * d + 1]) + gb[2 * d + 1:2 * d + 2])
            log_a = coef[d:d + 1] * r
            a_s[r0:r1, :] = jnp.exp(log_a)
            th = jnp.tanh(log_a)
            b_s[r0:r1, :] = jnp.sqrt(-2.0 * th / (1.0 - th)) * (gi * xc)

    def make_body(f_base, b_top):
        def body(k, carry):
            cf, cr = carry
            rf = pl.multiple_of((f_base + k) * SUBLANES, SUBLANES)
            hf, cf = _tile_scan(af[pl.ds(rf, SUBLANES), :], bf[pl.ds(rf, SUBLANES), :], cf, False)
            bf[pl.ds(rf, SUBLANES), :] = hf
            rb = pl.multiple_of((b_top - k) * SUBLANES, SUBLANES)
            hb, cr = _tile_scan(ab[pl.ds(rb, SUBLANES), :], bb[pl.ds(rb, SUBLANES), :], cr, True)
            bb[pl.ds(rb, SUBLANES), :] = hb
            return cf, cr
        return body

    z1 = jnp.zeros((1, W), F32)
    n_ctx = CTX // SUBLANES
    n_all = T // SUBLANES
    carry = lax.fori_loop(0, n_ctx, make_body(0, n_ctx - 1), (z1, z1))
    lax.fori_loop(0, n_all - n_ctx, make_body(n_ctx, n_all - 1), carry)

    for blk in range(T // LRU_RB):
        r0 = blk * LRU_RB
        r1 = r0 + LRU_RB
        h = bf[r0:r1, :] + bb[r0:r1, :]
        o_ref[r0:r1, :] = (h * jax.nn.gelu(ay_ref[r0:r1, :])).astype(BF16)


def _lru(z, layer, cw, cb, gw_bd, gb, lam):
    W = LRU_HALF
    nh = LRU_W // W
    return pl.pallas_call(
        _lru_kernel,
        grid=(B, nh),
        in_specs=[
            pl.BlockSpec((None, T, W), lambda b, c: (b, 0, c)),
            pl.BlockSpec((None, T, W), lambda b, c: (b, 0, nh + c)),
            pl.BlockSpec((None, LRU_TAPS, W), lambda b, c: (layer, 0, c)),
            pl.BlockSpec((None, 1, W), lambda b, c: (layer, 0, c)),
            pl.BlockSpec((None, None, 4, W, W), lambda b, c: (layer, c, 0, 0, 0)),
            pl.BlockSpec((None, 4, W), lambda b, c: (layer, 0, c)),
            pl.BlockSpec((None, 2, W), lambda b, c: (layer, 0, c)),
        ],
        out_specs=pl.BlockSpec((None, T, W), lambda b, c: (b, 0, c)),
        out_shape=jax.ShapeDtypeStruct((B, T, LRU_W), BF16),
        scratch_shapes=[pltpu.VMEM((T, W), F32)] * 4,
        compiler_params=_cparams(("parallel", "parallel")),
        name="lru",
    )(z, z, cw, cb, gw_bd, gb, lam)


def _attn_kernel(q_ref, k_ref, v_ref, cos_ref, sin_ref, bias_ref, o_ref,
                 qraw_s, qrot_s, qctx_s, krot_s, kc_s, v_s, sc_s, pc_s, ow_s):
    lane = lax.broadcasted_iota(jnp.int32, (1, LANES), 1)
    head0 = lane < HEAD
    second16 = ((lane // ROPE_F) % 2) == 1
    scale = HEAD ** -0.5 * LOG2E
    W2 = 2 * GRID_W

    def rope(x, c, s):
        partner = jnp.where(second16, pltpu.roll(x, ROPE_F, 1), pltpu.roll(x, LANES - ROPE_F, 1))
        return x * c + partner * s

    def stack_heads(x):
        return jnp.concatenate([jnp.where(head0, x, 0.0), jnp.where(head0, 0.0, x)], axis=0).astype(BF16)

    def unstack_heads(y):
        n = y.shape[0] // 2
        return jnp.where(head0, y[:n], y[n:])

    qctx_s[...] = stack_heads(q_ref[0:CTX, :] * scale)
    kc_s[...] = k_ref[0:CTX, :].astype(BF16)
    v_s[0:CTX, :] = v_ref[0:CTX, :].astype(BF16)
    rows_per_blk = ATT_RB // GRID_W
    for blk in range(SEQ // ATT_RB):
        r0 = blk * ATT_RB
        r1 = r0 + ATT_RB
        c = cos_ref[r0:r1, :]
        s = sin_ref[r0:r1, :]
        q = q_ref[CTX + r0:CTX + r1, :]
        qs = q * scale
        qr = rope(q, c, s) * scale
        for j in range(rows_per_blk):
            qraw_s[blk * rows_per_blk + j] = stack_heads(qs[j * GRID_W:(j + 1) * GRID_W])
            qrot_s[blk * rows_per_blk + j] = stack_heads(qr[j * GRID_W:(j + 1) * GRID_W])
        krot_s[r0:r1, :] = rope(k_ref[CTX + r0:CTX + r1, :], c, s).astype(BF16)
        v_s[CTX + r0:CTX + r1, :] = v_ref[CTX + r0:CTX + r1, :].astype(BF16)

    s = _dot_nt(qctx_s[...], kc_s[...])
    e = jnp.exp2(s - jnp.max(s, axis=-1, keepdims=True))
    p = (e / jnp.sum(e, axis=-1, keepdims=True)).astype(BF16)
    o_ref[0:CTX, :] = unstack_heads(_dot(p, v_s[0:CTX, :])).astype(BF16)

    for g in range(GRID_H // ATT_GROUP):
        g0, g1 = g * ATT_GROUP, (g + 1) * ATT_GROUP
        sc = _dot_nt(qraw_s[g0:g1].reshape(ATT_GROUP * W2, LANES), kc_s[...])
        sc_s[g0:g1] = sc.reshape(ATT_GROUP, W2, CTX)

    win = NA_ROWS * GRID_W

    def lane_fold(op, *xs):
        cols = [x[:, c:c + LANES] for x in xs for c in range(0, x.shape[1], LANES)]
        acc = cols[0]
        for col in cols[1:]:
            acc = op(acc, col)
        return acc

    def body(t, _):
        rows = [t * ATT_UNROLL + j for j in range(ATT_UNROLL)]
        starts = [jnp.clip(r - NA_ROWS // 2, 0, GRID_H - NA_ROWS) for r in rows]
        sw = [_dot_nt(qrot_s[r], krot_s[pl.ds(pl.multiple_of(rs * GRID_W, GRID_W), win), :]) + bias_ref[r - rs]
              for r, rs in zip(rows, starts)]
        sc = [sc_s[r] for r in rows]
        m = [jnp.max(lane_fold(jnp.maximum, a, c), axis=-1, keepdims=True) for a, c in zip(sw, sc)]
        ew = [jnp.exp2(a - mm) for a, mm in zip(sw, m)]
        ec = [jnp.exp2(c - mm) for c, mm in zip(sc, m)]
        inv = [1.0 / jnp.sum(lane_fold(jnp.add, a, c), axis=-1, keepdims=True) for a, c in zip(ew, ec)]
        for r, rs, a, c, iv in zip(rows, starts, ew, ec, inv):
            kx = pl.multiple_of(CTX + rs * GRID_W, GRID_W)
            ow_s[r] = _dot(a.astype(BF16), v_s[pl.ds(kx, win), :]) * iv
            pc_s[r] = (c * iv).astype(BF16)
        return 0

    lax.fori_loop(0, GRID_H // ATT_UNROLL, body, 0)

    for g in range(GRID_H // ATT_GROUP):
        g0, g1 = g * ATT_GROUP, (g + 1) * ATT_GROUP
        oc = _dot(pc_s[g0:g1].reshape(ATT_GROUP * W2, CTX), v_s[0:CTX, :]).reshape(ATT_GROUP, W2, LANES)
        tot = oc + ow_s[g0:g1]
        for j in range(ATT_GROUP):
            row = CTX + (g0 + j) * GRID_W
            o_ref[row:row + GRID_W, :] = unstack_heads(tot[j]).astype(BF16)


def _attn(z, layer, cos_t, sin_t, biasmask):
    nhp = NA_W // LANES
    qb = 2 * LRU_W // LANES
    return pl.pallas_call(
        _attn_kernel,
        grid=(nhp, B),
        in_specs=[
            pl.BlockSpec((None, T, LANES), lambda p, b: (b, 0, qb + p)),
            pl.BlockSpec((None, T, LANES), lambda p, b: (b, 0, qb + nhp + p)),
            pl.BlockSpec((None, T, LANES), lambda p, b: (b, 0, qb + 2 * nhp + p)),
            pl.BlockSpec((SEQ, LANES), lambda p, b: (0, 0)),
            pl.BlockSpec((SEQ, LANES), lambda p, b: (0, 0)),
            pl.BlockSpec((None, None, NA_ROWS, 2 * GRID_W, NA_ROWS * GRID_W), lambda p, b: (layer, p, 0, 0, 0)),
        ],
        out_specs=pl.BlockSpec((None, T, LANES), lambda p, b: (b, 0, p)),
        out_shape=jax.ShapeDtypeStruct((B, T, NA_W), BF16),
        scratch_shapes=[
            pltpu.VMEM((GRID_H, 2 * GRID_W, LANES), BF16),
            pltpu.VMEM((GRID_H, 2 * GRID_W, LANES), BF16),
            pltpu.VMEM((2 * CTX, LANES), BF16),
            pltpu.VMEM((SEQ, LANES), BF16),
            pltpu.VMEM((CTX, LANES), BF16),
            pltpu.VMEM((T, LANES), BF16),
            pltpu.VMEM((GRID_H, 2 * GRID_W, CTX), F32),
            pltpu.VMEM((GRID_H, 2 * GRID_W, CTX), BF16),
            pltpu.VMEM((GRID_H, 2 * GRID_W, LANES), F32),
        ],
        compiler_params=_cparams(("parallel", "parallel")),
        name="attn",
    )(z, z, z, cos_t, sin_t, biasmask)


def _sgu_kernel(u_ref, v_ref, g_ref, b_ref, ws_ref, bs_ref, o_ref):
    lane = lax.broadcasted_iota(jnp.int32, (1, LANES), 1)
    first = lane < HEAD
    for n in range(u_ref.shape[0] // SGU_CHUNK):
        r0 = n * SGU_CHUNK
        r1 = r0 + SGU_CHUNK
        v = jax.nn.gelu(v_ref[r0:r1, :])
        mu = jnp.mean(v, axis=-1, keepdims=True)
        var = jnp.mean(jnp.square(v - mu), axis=-1, keepdims=True)
        vn = ((v - mu) * lax.rsqrt(var + EPS) * g_ref[...] + b_ref[...]).astype(BF16)
        for j in range(SGU_W // LANES):
            c0 = j * LANES
            c1 = c0 + LANES
            vp = vn[:, c0:c1]
            mixed = jnp.where(first, _dot(ws_ref[2 * j], vp), _dot(ws_ref[2 * j + 1], vp)) + bs_ref[:, c0:c1]
            o_ref[r0:r1, c0:c1] = (jax.nn.gelu(u_ref[r0:r1, c0:c1]) * mixed).astype(BF16)


def _sgu(z, layer, ln_g, ln_b, ws_bf, bs_full):
    ub = (2 * LRU_W + 3 * NA_W) // SGU_W
    return pl.pallas_call(
        _sgu_kernel,
        grid=(B, T // TM_SGU),
        in_specs=[
            pl.BlockSpec((None, TM_SGU, SGU_W), lambda b, i: (b, i, ub)),
            pl.BlockSpec((None, TM_SGU, SGU_W), lambda b, i: (b, i, ub + 1)),
            pl.BlockSpec((None, 1, SGU_W), lambda b, i: (layer, 0, 0)),
            pl.BlockSpec((None, 1, SGU_W), lambda b, i: (layer, 0, 0)),
            pl.BlockSpec((None, SGU_G, SGU_CHUNK, SGU_CHUNK), lambda b, i: (layer, 0, 0, 0)),
            pl.BlockSpec((None, SGU_CHUNK, SGU_W), lambda b, i: (layer, 0, 0)),
        ],
        out_specs=pl.BlockSpec((None, TM_SGU, SGU_W), lambda b, i: (b, i, 0)),
        out_shape=jax.ShapeDtypeStruct((B, T, SGU_W), BF16),
        compiler_params=_cparams(("parallel", "parallel")),
        name="sgu",
    )(z, z, ln_g, ln_b, ws_bf, bs_full)


def _outproj_kernel(x_ref, a_ref, b_ref, c_ref, gtb, gtc, w_ref, o_ref):
    i = pl.program_id(1)
    acc = _dot(a_ref[...], w_ref[0:LRU_W, :])
    acc = acc + _dot(b_ref[...], w_ref[LRU_W:LRU_W + NA_W, :])
    acc = acc + _dot(c_ref[...], w_ref[LRU_W + NA_W:D, :])
    is_ctx = (lax.broadcasted_iota(jnp.int32, (TM, 1), 0) + i * TM) < CTX
    o_ref[...] = x_ref[...] + jnp.where(is_ctx, gtc[...], gtb[...]) * acc


def _outproj(xall, oa, ob, oc, mod5, layer, w_all):
    tn = D
    return pl.pallas_call(
        _outproj_kernel,
        grid=(B, NT, D // tn),
        in_specs=[
            pl.BlockSpec((None, TM, tn), lambda b, i, j: (b, i, j)),
            pl.BlockSpec((None, TM, LRU_W), lambda b, i, j: (b, i, 0)),
            pl.BlockSpec((None, TM, NA_W), lambda b, i, j: (b, i, 0)),
            pl.BlockSpec((None, TM, SGU_W), lambda b, i, j: (b, i, 0)),
            _mod_spec(layer, 2, False, tn, 2), _mod_spec(layer, 2, True, tn, 2),
            pl.BlockSpec((None, D, tn), lambda b, i, j: (layer, 0, j), pipeline_mode=pl.Buffered(1)),
        ],
        out_specs=pl.BlockSpec((None, TM, tn), lambda b, i, j: (b, i, j)),
        out_shape=jax.ShapeDtypeStruct((B, T, D), F32),
        compiler_params=_cparams(("parallel", "parallel", "parallel"), VMEM_BIG),
        name="outproj",
    )(xall, oa, ob, oc, mod5, mod5, w_all)


def _ffn_kernel(x_ref, xp_ref, xn_ref, g_ref, shb, scb, gtb, shc, scc, gtc, wa_ref, wg_ref, cw_ref, cb_ref, wd_ref,
                o_ref, hx_s, u_s, act_s):
    i = pl.program_id(1)
    k = pl.program_id(2)
    tm = x_ref.shape[0]
    rb = FFN_RB
    left_blocks = (0, CTX // rb)
    right_blocks = (CTX // rb - 1, tm // rb - 1)

    @pl.when(k == 0)
    def _():
        gain_b = g_ref[...] * (1.0 + scb[...])

        def normed(x):
            return (x * lax.rsqrt(jnp.mean(x * x, axis=-1, keepdims=True) + EPS)) * gain_b + shb[...]
        hx_s[0:HALO, :] = jnp.concatenate([normed(xn_ref[...]), normed(xp_ref[...])], axis=0).astype(BF16)
        _norm_tile(i, x_ref, hx_s, HALO, g_ref, shb, scb, shc, scc)
        o_ref[...] = jnp.zeros_like(o_ref)

    slab = FFN_SLAB
    for h in range(tm // slab):
        q0 = 0 if h == 0 else h * slab + 2 * HALO
        q1 = min((h + 1) * slab + 2 * HALO, HALO + tm)
        u_s[q0:q1, :TH] = _dot(hx_s[q0:q1, :], wa_ref[...])
        u_s[q0:q1, TH:] = _dot(hx_s[q0:q1, :], wg_ref[...])
    u_s[HALO + tm:HALO + tm + SUBLANES, :] = u_s[0:SUBLANES, :]
    cw = cw_ref[...]
    cb = cb_ref[...]
    for blk in range(tm // rb):
        r0 = HALO + blk * rb
        rows = lax.broadcasted_iota(jnp.int32, (rb, 1), 0) + (i * tm + blk * rb)
        ul = u_s[r0 - 1:r0 - 1 + rb, :]
        ur = u_s[r0 + 1:r0 + 1 + rb, :]
        if blk in left_blocks:
            ul = jnp.where(jnp.logical_and(rows != 0, rows != CTX), ul, 0.0)
        if blk in right_blocks:
            ur = jnp.where(jnp.logical_and(rows != CTX - 1, rows != T - 1), ur, 0.0)
        y = cb + ul * cw[0:1]
        y = y + u_s[r0:r0 + rb, :] * cw[1:2]
        y = y + ur * cw[2:3]
        a = y[:, :TH]
        gg = y[:, TH:]
        act_s[blk * rb:(blk + 1) * rb, :] = (gg * jax.nn.sigmoid(gg) * a).astype(BF16)
        if (blk + 1) * rb % slab == 0:
            s0 = (blk + 1) * rb - slab
            o_ref[s0:s0 + slab, :] += _dot(act_s[s0:s0 + slab, :], wd_ref[...])

    @pl.when(k == NK - 1)
    def _():
        def finish(lo, hi, gate):
            o_ref[lo:hi, :] = x_ref[lo:hi, :] + gate * o_ref[lo:hi, :]
        _per_segment(i, tm, finish, gtb[...], gtc[...])


def _ffn(xall, mod5, layer, g, wa, wg, cw_r, cb_r, wd_p):
    tm = TM_FFN
    hb = tm // SUBLANES
    last = T // SUBLANES - 1
    col_map = lambda b, i, k: (layer, 0, k)
    return pl.pallas_call(
        _ffn_kernel,
        grid=(B, T // tm, NK),
        in_specs=[
            pl.BlockSpec((None, tm, D), lambda b, i, k: (b, i, 0)),
            pl.BlockSpec((None, SUBLANES, D), lambda b, i, k: (b, jnp.maximum(i * hb - 1, 0), 0)),
            pl.BlockSpec((None, SUBLANES, D), lambda b, i, k: (b, jnp.minimum((i + 1) * hb, last), 0)),
            pl.BlockSpec((None, 1, D), lambda b, i, k: (layer, 0, 0)),
            _mod_spec(layer, 3, False), _mod_spec(layer, 4, False), _mod_spec(layer, 5, False),
            _mod_spec(layer, 3, True), _mod_spec(layer, 4, True), _mod_spec(layer, 5, True),
            pl.BlockSpec((None, D, TH), col_map),
            pl.BlockSpec((None, D, TH), col_map),
            pl.BlockSpec((None, 3, 2 * TH), col_map),
            pl.BlockSpec((None, 1, 2 * TH), col_map),
            pl.BlockSpec((None, TH, D), lambda b, i, k: (layer, k, 0)),
        ],
        out_specs=pl.BlockSpec((None, tm, D), lambda b, i, k: (b, i, 0)),
        out_shape=jax.ShapeDtypeStruct((B, T, D), F32),
        scratch_shapes=[
            pltpu.VMEM((HALO + tm, D), BF16),
            pltpu.VMEM((HALO + tm + SUBLANES, 2 * TH), F32),
            pltpu.VMEM((tm, TH), BF16),
        ],
        compiler_params=_cparams(("parallel", "parallel", "arbitrary"), VMEM_BIG),
        name="ffn",
    )(xall, xall, xall, g, mod5, mod5, mod5, mod5, mod5, mod5, wa, wg, cw_r, cb_r, wd_p)


def _final_kernel(x_ref, g_ref, o_ref):
    x = x_ref[...]
    o_ref[...] = x * lax.rsqrt(jnp.mean(x * x, axis=-1, keepdims=True) + EPS) * g_ref[...]


def _final_norm(xall, g):
    tm = CTX
    return pl.pallas_call(
        _final_kernel,
        grid=(B, SEQ // tm),
        in_specs=[
            pl.BlockSpec((None, tm, D), lambda b, i: (b, i + 1, 0)),
            pl.BlockSpec((1, D), lambda b, i: (0, 0)),
        ],
        out_specs=pl.BlockSpec((None, tm, D), lambda b, i: (b, i, 0)),
        out_shape=jax.ShapeDtypeStruct((B, SEQ, D), F32),
        compiler_params=_cparams(("parallel", "parallel")),
        name="final_norm",
    )(xall, g.reshape(1, D))


def _rope_tables():
    t = jnp.arange(SEQ)
    pos = jnp.stack([t // GRID_W, t % GRID_W], axis=-1).astype(F32)
    inv = ROPE_BASE ** (-jnp.arange(ROPE_F, dtype=F32) / ROPE_F)
    ang = pos[:, :, None] * inv
    cos, sin = jnp.cos(ang), jnp.sin(ang)
    cos_h = jnp.concatenate([cos[:, 0], cos[:, 0], cos[:, 1], cos[:, 1]], axis=-1)
    sin_h = jnp.concatenate([-sin[:, 0], sin[:, 0], -sin[:, 1], sin[:, 1]], axis=-1)
    reps = LANES // HEAD
    return jnp.tile(cos_h, (1, reps)), jnp.tile(sin_h, (1, reps))


def _bias_tables(rpb):
    nl = rpb.shape[0]
    nr, nc = 2 * NA_ROWS - 1, 2 * NA_KC - 1
    pairs = rpb.reshape(nl, NA_HEADS // 2, 2, nr, nc) * LOG2E
    pairs = jnp.pad(pairs, ((0, 0), (0, 0), (0, 0), (0, 2 * NA_ROWS - nr), (0, LANES - nc)))
    return pl.pallas_call(
        _bias_kernel,
        grid=(nl, NA_HEADS // 2),
        in_specs=[pl.BlockSpec((None, None, 2, 2 * NA_ROWS, LANES), lambda l, p: (l, p, 0, 0, 0))],
        out_specs=pl.BlockSpec((None, None, NA_ROWS, 2 * GRID_W, NA_ROWS * GRID_W), lambda l, p: (l, p, 0, 0, 0)),
        out_shape=jax.ShapeDtypeStruct((nl, NA_HEADS // 2, NA_ROWS, 2 * GRID_W, NA_ROWS * GRID_W), F32),
        compiler_params=_cparams(("parallel", "parallel")),
        name="bias_table",
    )(pairs)


def _bias_kernel(rp_ref, o_ref):
    c = lax.broadcasted_iota(jnp.int32, (GRID_W, LANES), 0)
    lane = lax.broadcasted_iota(jnp.int32, (GRID_W, LANES), 1)
    kc = lane % GRID_W
    col_start = jnp.clip(c - NA_KC // 2, 0, GRID_W - NA_KC)
    in_win = jnp.logical_and(kc >= col_start, kc < col_start + NA_KC)
    first = lane < GRID_W
    per_blk = LANES // GRID_W
    for cls in range(NA_ROWS):
        for hd in range(2):
            for blk in range(NA_ROWS // per_blk):
                parts = []
                for h in range(per_blk):
                    m = blk * per_blk + h - cls + NA_ROWS - 1
                    row = jnp.broadcast_to(rp_ref[hd, m:m + 1, :], (GRID_W, LANES))
                    shift = (LANES - (NA_KC - 1) + h * GRID_W) % LANES
                    parts.append(pltpu.roll(row, shift, 1, stride=1, stride_axis=0))
                tile = jnp.where(in_win, jnp.where(first, parts[0], parts[1]), NEG)
                o_ref[cls, hd * GRID_W:(hd + 1) * GRID_W, blk * LANES:(blk + 1) * LANES] = tile


def _gate_blockdiag(gate_w):
    nl = gate_w.shape[0]
    nh = LRU_W // LRU_HALF
    gper = LRU_HALF // HEAD
    w = gate_w.reshape(nl, 4, nh, gper, HEAD, HEAD)
    eye = jnp.eye(gper, dtype=gate_w.dtype)
    bd = jnp.einsum('lkhgio,gj->lkhgijo', w, eye).reshape(nl, 4, nh, LRU_HALF, LRU_HALF)
    return jnp.transpose(bd, (0, 2, 1, 3, 4)).astype(BF16)


def _ffn_layout(w_up, conv_w, conv_b, w_down):
    pad = FFN_HP - FFN_H
    nl = w_up.shape[0]

    def inter(m):
        r = m.shape[1]
        a = jnp.pad(m[..., :FFN_H], ((0, 0), (0, 0), (0, pad))).reshape(nl, r, NK, 1, TH)
        g = jnp.pad(m[..., FFN_H:], ((0, 0), (0, 0), (0, pad))).reshape(nl, r, NK, 1, TH)
        return jnp.concatenate([a, g], axis=3).reshape(nl, r, NK * 2 * TH)

    zc = jnp.zeros((nl, D, pad), BF16)
    wa = jnp.concatenate([w_up[..., :FFN_H].astype(BF16), zc], axis=2)
    wg = jnp.concatenate([w_up[..., FFN_H:].astype(BF16), zc], axis=2)
    wd = jnp.concatenate([w_down.astype(BF16), jnp.zeros((nl, pad, D), BF16)], axis=1)
    return wa, wg, inter(conv_w), inter(conv_b[:, None, :]), wd


def kernel(x, c, ctx, c_ctx, w_ada, b_ada, norm_mix_g, norm_ffn_g, w_in, lru_conv_w, lru_conv_b, lru_gate_w, lru_gate_b, lru_lambda, na_rpb, sgu_ln_g, sgu_ln_b, sgu_w, sgu_b, w_out, ffn_up, ffn_conv_w, ffn_conv_b, ffn_down, final_norm_g):
    xall = jnp.concatenate([ctx, x], axis=1)
    cc = jnp.concatenate([c, c_ctx[None], jnp.zeros((MOD_ROWS - B - 1, D), F32)], axis=0)
    mod5 = _adaln(cc, w_ada, b_ada).reshape(DEPTH, 6, MOD_ROWS, 1, D)
    cos_t, sin_t = _rope_tables()

    g_mix = norm_mix_g.reshape(DEPTH, 1, D)
    g_ffn = norm_ffn_g.reshape(DEPTH, 1, D)
    w_in_bf = w_in.astype(BF16)
    w_out_bf = w_out.astype(BF16)
    lru_cb = lru_conv_b.reshape(DEPTH, 1, LRU_W)
    lru_gw = _gate_blockdiag(lru_gate_w)
    lru_gb = lru_gate_b.reshape(DEPTH, 4, LRU_W)
    bias = _bias_tables(na_rpb)
    sgu_g = sgu_ln_g.reshape(DEPTH, 1, SGU_W)
    sgu_bb = sgu_ln_b.reshape(DEPTH, 1, SGU_W)
    sgu_w_bf = sgu_w.astype(BF16)
    sgu_bias = jnp.repeat(jnp.swapaxes(sgu_b, 1, 2), HEAD, axis=2)
    ffn_params = _ffn_layout(ffn_up, ffn_conv_w, ffn_conv_b, ffn_down)

    for l in range(DEPTH):
        z = _inproj(xall, mod5, l, g_mix, w_in_bf)
        oa = _lru(z, l, lru_conv_w, lru_cb, lru_gw, lru_gb, lru_lambda)
        ob = _attn(z, l, cos_t, sin_t, bias)
        oc = _sgu(z, l, sgu_g, sgu_bb, sgu_w_bf, sgu_bias)
        xall = _outproj(xall, oa, ob, oc, mod5, l, w_out_bf)
        xall = _ffn(xall, mod5, l, g_ffn, *ffn_params)
    return _final_norm(xall, final_norm_g)
```

```python
import functools

import jax
import jax.numpy as jnp
from jax import lax
from jax.experimental import pallas as pl
from jax.experimental.pallas import tpu as pltpu

F32 = jnp.float32
BF16 = jnp.bfloat16

D = 2048
B = 8
SEQ = 2048
DEPTH = 4
GRID_W = 64
GRID_H = SEQ // GRID_W
CTX = 256
T = CTX + SEQ
HEAD = 64
LRU_W = 512
LRU_TAPS = 4
LRU_C = 8.0
NA_W = 1024
NA_HEADS = NA_W // HEAD
NA_ROWS = 8
NA_KC = 16
SGU_W = 512
SGU_G = SGU_W // HEAD
SGU_CHUNK = 128
PROJ = 2 * LRU_W + 3 * NA_W + 2 * SGU_W
FFN_H = 5504
EPS = 1e-6
NEG = -1e30
LOG2E = 1.4426950408889634
ROPE_BASE = 10000.0
ROPE_F = HEAD // 4
MOD_ROWS = 16
CTX_ROW = B

LANES = 128
SUBLANES = 8
VMEM_BIG = 56 * 1024 * 1024
VMEM_MID = 40 * 1024 * 1024

TM = 768
NT = T // TM
HALO = 16
TH = 512
FFN_HP = 5632
NK = FFN_HP // TH
LRU_HALF = 512
LRU_RB = 128
ATT_RB = 256
NORM_RB = 16
NORM_UNROLL = 4
FFN_RB = 128
TM_FFN = 768
TM_SGU = 1152
FFN_SLAB = 384
ATT_UNROLL = 4
ATT_GROUP = 16


def _cparams(sem, vmem=VMEM_MID):
    return pltpu.CompilerParams(dimension_semantics=sem, vmem_limit_bytes=vmem)


def _dot(a, b):
    return jnp.dot(a, b, preferred_element_type=F32)


def _dot_nt(a, b):
    return lax.dot_general(a, b, (((1,), (1,)), ((), ())), preferred_element_type=F32)


def _mod_spec(layer, which, ctx, width=D, col=None):
    def imap(*ids):
        b = ids[0]
        row = CTX_ROW if ctx else b
        c = 0 if col is None else ids[col]
        return (layer, which, row, 0, c)
    return pl.BlockSpec((None, None, None, 1, width), imap)


def _per_segment(i, tm, fn, latent_arg, ctx_arg):
    @pl.when(i == 0)
    def _():
        fn(0, CTX, ctx_arg)
        fn(CTX, tm, latent_arg)

    @pl.when(i != 0)
    def _():
        fn(0, tm, latent_arg)


def _norm_rows(x_ref, dst_ref, dst_off, lo, hi, gain, shift):
    def body(j, _):
        r = pl.multiple_of(lo + j * NORM_RB, NORM_RB)
        x = x_ref[pl.ds(r, NORM_RB), :]
        inv = lax.rsqrt(jnp.mean(x * x, axis=-1, keepdims=True) + EPS)
        dst_ref[pl.ds(dst_off + r, NORM_RB), :] = ((x * inv) * gain + shift).astype(BF16)
        return 0
    n = (hi - lo) // NORM_RB
    lax.fori_loop(0, n, body, 0, unroll=min(n, NORM_UNROLL))


def _norm_tile(i, x_ref, dst_ref, dst_off, g_ref, shb, scb, shc, scc):
    g = g_ref[...]

    def run(lo, hi, mod):
        _norm_rows(x_ref, dst_ref, dst_off, lo, hi, g * (1.0 + mod[0]), mod[1])
    _per_segment(i, x_ref.shape[0], run, (scb[...], shb[...]), (scc[...], shc[...]))


def _adaln_kernel(c_ref, w_ref, b_ref, o_ref):
    c = c_ref[...]
    s = (c * jax.nn.sigmoid(c)).astype(BF16)
    o_ref[...] = _dot(s, w_ref[...].astype(BF16)) + b_ref[...]


def _adaln(cc, w_ada, b_ada):
    tn = D // 2
    nj = D // tn
    return pl.pallas_call(
        _adaln_kernel,
        grid=(DEPTH, 6 * nj),
        in_specs=[
            pl.BlockSpec((MOD_ROWS, D), lambda l, j: (0, 0)),
            pl.BlockSpec((None, D, tn), lambda l, j: (l, 0, j)),
            pl.BlockSpec((None, 1, tn), lambda l, j: (l, 0, j)),
        ],
        out_specs=pl.BlockSpec((None, None, MOD_ROWS, tn), lambda l, j: (l, j // nj, 0, j % nj)),
        out_shape=jax.ShapeDtypeStruct((DEPTH, 6, MOD_ROWS, D), F32),
        compiler_params=_cparams(("parallel", "parallel")),
        name="adaln",
    )(cc, w_ada, b_ada.reshape(DEPTH, 1, 6 * D))


def _inproj_kernel(x_ref, g_ref, shb, scb, shc, scc, w_ref, o_ref, hx_s):
    i = pl.program_id(1)

    @pl.when(pl.program_id(2) == 0)
    def _():
        _norm_tile(i, x_ref, hx_s, 0, g_ref, shb, scb, shc, scc)

    o_ref[...] = _dot(hx_s[...], w_ref[...])


def _inproj(xall, mod5, layer, g_all, w_all):
    tn = PROJ // 2
    return pl.pallas_call(
        _inproj_kernel,
        grid=(B, NT, PROJ // tn),
        in_specs=[
            pl.BlockSpec((None, TM, D), lambda b, i, j: (b, i, 0)),
            pl.BlockSpec((None, 1, D), lambda b, i, j: (layer, 0, 0)),
            _mod_spec(layer, 0, False), _mod_spec(layer, 1, False),
            _mod_spec(layer, 0, True), _mod_spec(layer, 1, True),
            pl.BlockSpec((None, D, tn), lambda b, i, j: (layer, 0, j)),
        ],
        out_specs=pl.BlockSpec((None, TM, tn), lambda b, i, j: (b, i, j)),
        out_shape=jax.ShapeDtypeStruct((B, T, PROJ), F32),
        scratch_shapes=[pltpu.VMEM((TM, D), BF16)],
        compiler_params=_cparams(("parallel", "parallel", "arbitrary"), VMEM_BIG),
        name="inproj",
    )(xall, g_all, mod5, mod5, mod5, mod5, w_all)


def _tile_scan(a, b, carry, rev):
    rid = lax.broadcasted_iota(jnp.int32, a.shape, 0)
    for d in (1, 2, 4):
        sh = SUBLANES - d if rev else d
        keep = (rid < SUBLANES - d) if rev else (rid >= d)
        a_s = jnp.where(keep, pltpu.roll(a, sh, 0), 1.0)
        b_s = jnp.where(keep, pltpu.roll(b, sh, 0), 0.0)
        b = a * b_s + b
        a = a * a_s
    h = a * carry + b
    return h, (h[0:1] if rev else h[SUBLANES - 1:SUBLANES])


def _lru_kernel(ax_ref, ay_ref, cw_ref, cb_ref, gw_ref, gb_ref, lam_ref, o_ref, af, bf, ab, bb):
    W = LRU_HALF
    cw = cw_ref[...]
    cb = cb_ref[...]
    gb = gb_ref[...]
    nl = -lam_ref[...]
    softplus = jnp.maximum(nl, 0.0) + jnp.log(1.0 + jnp.exp(-jnp.abs(nl)))
    coef = -LRU_C * softplus
    zeros8 = jnp.zeros((SUBLANES, W), F32)
    n_ext = LRU_RB + 2 * SUBLANES

    for blk in range(T // LRU_RB):
        r0 = blk * LRU_RB
        r1 = r0 + LRU_RB
        prev = zeros8 if r0 in (0, CTX) else ax_ref[r0 - SUBLANES:r0, :]
        nxt = zeros8 if r1 in (CTX, T) else ax_ref[r1:r1 + SUBLANES, :]
        ext = jnp.concatenate([prev, ax_ref[r0:r1, :], nxt], axis=0)
        lo, hi = SUBLANES, SUBLANES + LRU_RB
        xc = cb + pltpu.roll(ext, 2, 0)[lo:hi] * cw[0:1]
        xc = xc + pltpu.roll(ext, 1, 0)[lo:hi] * cw[1:2]
        xc = xc + ext[lo:hi] * cw[2:3]
        xc = xc + pltpu.roll(ext, n_ext - 1, 0)[lo:hi] * cw[3:4]
        xb = xc.astype(BF16)
        for d, (a_s, b_s) in enumerate(((af, bf), (ab, bb))):
            r = jax.nn.sigmoid(_dot(xb, gw_ref[2 * d]) + gb[2 * d:2 * d + 1])
            gi = jax.nn.sigmoid(_dot(xb, gw_ref[2 * d + 1]) + gb[2 * d + 1:2 * d + 2])
            log_a = coef[d:d + 1] * r
            a_s[r0:r1, :] = jnp.exp(log_a)
            th = jnp.tanh(log_a)
            b_s[r0:r1, :] = jnp.sqrt(-2.0 * th / (1.0 - th)) * (gi * xc)

    def make_body(f_base, b_top):
        def body(k, carry):
            cf, cr = carry
            rf = pl.multiple_of((f_base + k) * SUBLANES, SUBLANES)
            hf, cf = _tile_scan(af[pl.ds(rf, SUBLANES), :], bf[pl.ds(rf, SUBLANES), :], cf, False)
            bf[pl.ds(rf, SUBLANES), :] = hf
            rb = pl.multiple_of((b_top - k) * SUBLANES, SUBLANES)
            hb, cr = _tile_scan(ab[pl.ds(rb, SUBLANES), :], bb[pl.ds(rb, SUBLANES), :], cr, True)
            bb[pl.ds(rb, SUBLANES), :] = hb
            return cf, cr
        return body

    z1 = jnp.zeros((1, W), F32)
    n_ctx = CTX // SUBLANES
    n_all = T // SUBLANES
    carry = lax.fori_loop(0, n_ctx, make_body(0, n_ctx - 1), (z1, z1))
    lax.fori_loop(0, n_all - n_ctx, make_body(n_ctx, n_all - 1), carry)

    for blk in range(T // LRU_RB):
        r0 = blk * LRU_RB
        r1 = r0 + LRU_RB
        h = bf[r0:r1, :] + bb[r0:r1, :]
        o_ref[r0:r1, :] = (h * jax.nn.gelu(ay_ref[r0:r1, :])).astype(BF16)


def _lru(z, layer, cw, cb, gw_bd, gb, lam):
    W = LRU_HALF
    nh = LRU_W // W
    return pl.pallas_call(
        _lru_kernel,
        grid=(B, nh),
        in_specs=[
            pl.BlockSpec((None, T, W), lambda b, c: (b, 0, c)),
            pl.BlockSpec((None, T, W), lambda b, c: (b, 0, nh + c)),
            pl.BlockSpec((None, LRU_TAPS, W), lambda b, c: (layer, 0, c)),
            pl.BlockSpec((None, 1, W), lambda b, c: (layer, 0, c)),
            pl.BlockSpec((None, None, 4, W, W), lambda b, c: (layer, c, 0, 0, 0)),
            pl.BlockSpec((None, 4, W), lambda b, c: (layer, 0, c)),
            pl.BlockSpec((None, 2, W), lambda b, c: (layer, 0, c)),
        ],
        out_specs=pl.BlockSpec((None, T, W), lambda b, c: (b, 0, c)),
        out_shape=jax.ShapeDtypeStruct((B, T, LRU_W), BF16),
        scratch_shapes=[pltpu.VMEM((T, W), F32)] * 4,
        compiler_params=_cparams(("parallel", "parallel"), VMEM_BIG),
        name="lru",
    )(z, z, cw, cb, gw_bd, gb, lam)


def _attn_kernel(q_ref, k_ref, v_ref, cos_ref, sin_ref, bias_ref, o_ref,
                 qraw_s, qrot_s, qctx_s, krot_s, kc_s, v_s, sc_s, pc_s, ow_s):
    lane = lax.broadcasted_iota(jnp.int32, (1, LANES), 1)
    head0 = lane < HEAD
    second16 = ((lane // ROPE_F) % 2) == 1
    scale = HEAD ** -0.5 * LOG2E
    W2 = 2 * GRID_W

    def rope(x, c, s):
        partner = jnp.where(second16, pltpu.roll(x, ROPE_F, 1), pltpu.roll(x, LANES - ROPE_F, 1))
        return x * c + partner * s

    def stack_heads(x):
        return jnp.concatenate([jnp.where(head0, x, 0.0), jnp.where(head0, 0.0, x)], axis=0).astype(BF16)

    def unstack_heads(y):
        n = y.shape[0] // 2
        return jnp.where(head0, y[:n], y[n:])

    qctx_s[...] = stack_heads(q_ref[0:CTX, :] * scale)
    kc_s[...] = k_ref[0:CTX, :].astype(BF16)
    v_s[0:CTX, :] = v_ref[0:CTX, :].astype(BF16)
    rows_per_blk = ATT_RB // GRID_W
    for blk in range(SEQ // ATT_RB):
        r0 = blk * ATT_RB
        r1 = r0 + ATT_RB
        c = cos_ref[r0:r1, :]
        s = sin_ref[r0:r1, :]
        q = q_ref[CTX + r0:CTX + r1, :]
        qs = q * scale
        qr = rope(q, c, s) * scale
        for j in range(rows_per_blk):
            qraw_s[blk * rows_per_blk + j] = stack_heads(qs[j * GRID_W:(j + 1) * GRID_W])
            qrot_s[blk * rows_per_blk + j] = stack_heads(qr[j * GRID_W:(j + 1) * GRID_W])
        krot_s[r0:r1, :] = rope(k_ref[CTX + r0:CTX + r1, :], c, s).astype(BF16)
        v_s[CTX + r0:CTX + r1, :] = v_ref[CTX + r0:CTX + r1, :].astype(BF16)

    s = _dot_nt(qctx_s[...], kc_s[...])
    e = jnp.exp2(s - jnp.max(s, axis=-1, keepdims=True))
    p = (e / jnp.sum(e, axis=-1, keepdims=True)).astype(BF16)
    o_ref[0:CTX, :] = unstack_heads(_dot(p, v_s[0:CTX, :])).astype(BF16)

    for g in range(GRID_H // ATT_GROUP):
        g0, g1 = g * ATT_GROUP, (g + 1) * ATT_GROUP
        sc = _dot_nt(qraw_s[g0:g1].reshape(ATT_GROUP * W2, LANES), kc_s[...])
        sc_s[g0:g1] = sc.reshape(ATT_GROUP, W2, CTX)

    win = NA_ROWS * GRID_W

    def lane_fold(op, *xs):
        cols = [x[:, c:c + LANES] for x in xs for c in range(0, x.shape[1], LANES)]
        acc = cols[0]
        for col in cols[1:]:
            acc = op(acc, col)
        return acc

    def body(t, _):
        rows = [t * ATT_UNROLL + j for j in range(ATT_UNROLL)]
        starts = [jnp.clip(r - NA_ROWS // 2, 0, GRID_H - NA_ROWS) for r in rows]
        sw = [_dot_nt(qrot_s[r], krot_s[pl.ds(pl.multiple_of(rs * GRID_W, GRID_W), win), :]) + bias_ref[r - rs]
              for r, rs in zip(rows, starts)]
        sc = [sc_s[r] for r in rows]
        m = [jnp.max(lane_fold(jnp.maximum, a, c), axis=-1, keepdims=True) for a, c in zip(sw, sc)]
        ew = [jnp.exp2(a - mm) for a, mm in zip(sw, m)]
        ec = [jnp.exp2(c - mm) for c, mm in zip(sc, m)]
        inv = [1.0 / jnp.sum(lane_fold(jnp.add, a, c), axis=-1, keepdims=True) for a, c in zip(ew, ec)]
        for r, rs, a, c, iv in zip(rows, starts, ew, ec, inv):
            kx = pl.multiple_of(CTX + rs * GRID_W, GRID_W)
            ow_s[r] = _dot(a.astype(BF16), v_s[pl.ds(kx, win), :]) * iv
            pc_s[r] = (c * iv).astype(BF16)
        return 0

    lax.fori_loop(0, GRID_H // ATT_UNROLL, body, 0)

    for g in range(GRID_H // ATT_GROUP):
        g0, g1 = g * ATT_GROUP, (g + 1) * ATT_GROUP
        oc = _dot(pc_s[g0:g1].reshape(ATT_GROUP * W2, CTX), v_s[0:CTX, :]).reshape(ATT_GROUP, W2, LANES)
        tot = oc + ow_s[g0:g1]
        for j in range(ATT_GROUP):
            row = CTX + (g0 + j) * GRID_W
            o_ref[row:row + GRID_W, :] = unstack_heads(tot[j]).astype(BF16)


def _attn(z, layer, cos_t, sin_t, biasmask):
    nhp = NA_W // LANES
    qb = 2 * LRU_W // LANES
    return pl.pallas_call(
        _attn_kernel,
        grid=(nhp, B),
        in_specs=[
            pl.BlockSpec((None, T, LANES), lambda p, b: (b, 0, qb + p)),
            pl.BlockSpec((None, T, LANES), lambda p, b: (b, 0, qb + nhp + p)),
            pl.BlockSpec((None, T, LANES), lambda p, b: (b, 0, qb + 2 * nhp + p)),
            pl.BlockSpec((SEQ, LANES), lambda p, b: (0, 0)),
            pl.BlockSpec((SEQ, LANES), lambda p, b: (0, 0)),
            pl.BlockSpec((None, None, NA_ROWS, 2 * GRID_W, NA_ROWS * GRID_W), lambda p, b: (layer, p, 0, 0, 0)),
        ],
        out_specs=pl.BlockSpec((None, T, LANES), lambda p, b: (b, 0, p)),
        out_shape=jax.ShapeDtypeStruct((B, T, NA_W), BF16),
        scratch_shapes=[
            pltpu.VMEM((GRID_H, 2 * GRID_W, LANES), BF16),
            pltpu.VMEM((GRID_H, 2 * GRID_W, LANES), BF16),
            pltpu.VMEM((2 * CTX, LANES), BF16),
            pltpu.VMEM((SEQ, LANES), BF16),
            pltpu.VMEM((CTX, LANES), BF16),
            pltpu.VMEM((T, LANES), BF16),
            pltpu.VMEM((GRID_H, 2 * GRID_W, CTX), F32),
            pltpu.VMEM((GRID_H, 2 * GRID_W, CTX), BF16),
            pltpu.VMEM((GRID_H, 2 * GRID_W, LANES), F32),
        ],
        compiler_params=_cparams(("parallel", "parallel")),
        name="attn",
    )(z, z, z, cos_t, sin_t, biasmask)


def _sgu_kernel(u_ref, v_ref, g_ref, b_ref, ws_ref, bs_ref, o_ref):
    lane = lax.broadcasted_iota(jnp.int32, (1, LANES), 1)
    first = lane < HEAD
    for n in range(u_ref.shape[0] // SGU_CHUNK):
        r0 = n * SGU_CHUNK
        r1 = r0 + SGU_CHUNK
        v = jax.nn.gelu(v_ref[r0:r1, :])
        mu = jnp.mean(v, axis=-1, keepdims=True)
        var = jnp.mean(jnp.square(v - mu), axis=-1, keepdims=True)
        vn = ((v - mu) * lax.rsqrt(var + EPS) * g_ref[...] + b_ref[...]).astype(BF16)
        for j in range(SGU_W // LANES):
            c0 = j * LANES
            c1 = c0 + LANES
            vp = vn[:, c0:c1]
            mixed = jnp.where(first, _dot(ws_ref[2 * j], vp), _dot(ws_ref[2 * j + 1], vp)) + bs_ref[:, c0:c1]
            o_ref[r0:r1, c0:c1] = (jax.nn.gelu(u_ref[r0:r1, c0:c1]) * mixed).astype(BF16)


def _sgu(z, layer, ln_g, ln_b, ws_bf, bs_full):
    ub = (2 * LRU_W + 3 * NA_W) // SGU_W
    return pl.pallas_call(
        _sgu_kernel,
        grid=(B, T // TM_SGU),
        in_specs=[
            pl.BlockSpec((None, TM_SGU, SGU_W), lambda b, i: (b, i, ub)),
            pl.BlockSpec((None, TM_SGU, SGU_W), lambda b, i: (b, i, ub + 1)),
            pl.BlockSpec((None, 1, SGU_W), lambda b, i: (layer, 0, 0)),
            pl.BlockSpec((None, 1, SGU_W), lambda b, i: (layer, 0, 0)),
            pl.BlockSpec((None, SGU_G, SGU_CHUNK, SGU_CHUNK), lambda b, i: (layer, 0, 0, 0)),
            pl.BlockSpec((None, SGU_CHUNK, SGU_W), lambda b, i: (layer, 0, 0)),
        ],
        out_specs=pl.BlockSpec((None, TM_SGU, SGU_W), lambda b, i: (b, i, 0)),
        out_shape=jax.ShapeDtypeStruct((B, T, SGU_W), BF16),
        compiler_params=_cparams(("parallel", "parallel")),
        name="sgu",
    )(z, z, ln_g, ln_b, ws_bf, bs_full)


def _outproj_kernel(x_ref, a_ref, b_ref, c_ref, gtb, gtc, w_ref, o_ref):
    i = pl.program_id(1)
    acc = _dot(a_ref[...], w_ref[0:LRU_W, :])
    acc = acc + _dot(b_ref[...], w_ref[LRU_W:LRU_W + NA_W, :])
    acc = acc + _dot(c_ref[...], w_ref[LRU_W + NA_W:D, :])
    is_ctx = (lax.broadcasted_iota(jnp.int32, (TM, 1), 0) + i * TM) < CTX
    o_ref[...] = x_ref[...] + jnp.where(is_ctx, gtc[...], gtb[...]) * acc


def _outproj(xall, oa, ob, oc, mod5, layer, w_all):
    tn = D
    return pl.pallas_call(
        _outproj_kernel,
        grid=(B, NT, D // tn),
        in_specs=[
            pl.BlockSpec((None, TM, tn), lambda b, i, j: (b, i, j)),
            pl.BlockSpec((None, TM, LRU_W), lambda b, i, j: (b, i, 0)),
            pl.BlockSpec((None, TM, NA_W), lambda b, i, j: (b, i, 0)),
            pl.BlockSpec((None, TM, SGU_W), lambda b, i, j: (b, i, 0)),
            _mod_spec(layer, 2, False, tn, 2), _mod_spec(layer, 2, True, tn, 2),
            pl.BlockSpec((None, D, tn), lambda b, i, j: (layer, 0, j), pipeline_mode=pl.Buffered(1)),
        ],
        out_specs=pl.BlockSpec((None, TM, tn), lambda b, i, j: (b, i, j)),
        out_shape=jax.ShapeDtypeStruct((B, T, D), F32),
        compiler_params=_cparams(("parallel", "parallel", "parallel"), VMEM_BIG),
        name="outproj",
    )(xall, oa, ob, oc, mod5, mod5, w_all)


def _ffn_kernel(x_ref, xp_ref, xn_ref, g_ref, shb, scb, gtb, shc, scc, gtc, wa_ref, wg_ref, cw_ref, cb_ref, wd_ref,
                o_ref, hx_s, u_s, act_s):
    i = pl.program_id(1)
    k = pl.program_id(2)
    tm = x_ref.shape[0]
    rb = FFN_RB
    left_blocks = (0, CTX // rb)
    right_blocks = (CTX // rb - 1, tm // rb - 1)

    @pl.when(k == 0)
    def _():
        gain_b = g_ref[...] * (1.0 + scb[...])

        def normed(x):
            return (x * lax.rsqrt(jnp.mean(x * x, axis=-1, keepdims=True) + EPS)) * gain_b + shb[...]
        hx_s[0:HALO, :] = jnp.concatenate([normed(xn_ref[...]), normed(xp_ref[...])], axis=0).astype(BF16)
        _norm_tile(i, x_ref, hx_s, HALO, g_ref, shb, scb, shc, scc)
        o_ref[...] = jnp.zeros_like(o_ref)

    slab = FFN_SLAB
    for h in range(tm // slab):
        q0 = 0 if h == 0 else h * slab + 2 * HALO
        q1 = min((h + 1) * slab + 2 * HALO, HALO + tm)
        u_s[q0:q1, :TH] = _dot(hx_s[q0:q1, :], wa_ref[...])
        u_s[q0:q1, TH:] = _dot(hx_s[q0:q1, :], wg_ref[...])
    u_s[HALO + tm:HALO + tm + SUBLANES, :] = u_s[0:SUBLANES, :]
    cw = cw_ref[...]
    cb = cb_ref[...]
    for blk in range(tm // rb):
        r0 = HALO + blk * rb
        rows = lax.broadcasted_iota(jnp.int32, (rb, 1), 0) + (i * tm + blk * rb)
        ul = u_s[r0 - 1:r0 - 1 + rb, :]
        ur = u_s[r0 + 1:r0 + 1 + rb, :]
        if blk in left_blocks:
            ul = jnp.where(jnp.logical_and(rows != 0, rows != CTX), ul, 0.0)
        if blk in right_blocks:
            ur = jnp.where(jnp.logical_and(rows != CTX - 1, rows != T - 1), ur, 0.0)
        y = cb + ul * cw[0:1]
        y = y + u_s[r0:r0 + rb, :] * cw[1:2]
        y = y + ur * cw[2:3]
        a = y[:, :TH]
        gg = y[:, TH:]
        act_s[blk * rb:(blk + 1) * rb, :] = (gg * jax.nn.sigmoid(gg) * a).astype(BF16)
        if (blk + 1) * rb % slab == 0:
            s0 = (blk + 1) * rb - slab
            o_ref[s0:s0 + slab, :] += _dot(act_s[s0:s0 + slab, :], wd_ref[...])

    @pl.when(k == NK - 1)
    def _():
        def finish(lo, hi, gate):
            o_ref[lo:hi, :] = x_ref[lo:hi, :] + gate * o_ref[lo:hi, :]
        _per_segment(i, tm, finish, gtb[...], gtc[...])


def _ffn(xall, mod5, layer, g, wa, wg, cw_r, cb_r, wd_p):
    tm = TM_FFN
    hb = tm // SUBLANES
    last = T // SUBLANES - 1
    col_map = lambda b, i, k: (layer, 0, k)
    return pl.pallas_call(
        _ffn_kernel,
        grid=(B, T // tm, NK),
        in_specs=[
            pl.BlockSpec((None, tm, D), lambda b, i, k: (b, i, 0)),
            pl.BlockSpec((None, SUBLANES, D), lambda b, i, k: (b, jnp.maximum(i * hb - 1, 0), 0)),
            pl.BlockSpec((None, SUBLANES, D), lambda b, i, k: (b, jnp.minimum((i + 1) * hb, last), 0)),
            pl.BlockSpec((None, 1, D), lambda b, i, k: (layer, 0, 0)),
            _mod_spec(layer, 3, False), _mod_spec(layer, 4, False), _mod_spec(layer, 5, False),
            _mod_spec(layer, 3, True), _mod_spec(layer, 4, True), _mod_spec(layer, 5, True),
            pl.BlockSpec((None, D, TH), col_map),
            pl.BlockSpec((None, D, TH), col_map),
            pl.BlockSpec((None, 3, 2 * TH), col_map),
            pl.BlockSpec((None, 1, 2 * TH), col_map),
            pl.BlockSpec((None, TH, D), lambda b, i, k: (layer, k, 0)),
        ],
        out_specs=pl.BlockSpec((None, tm, D), lambda b, i, k: (b, i, 0)),
        out_shape=jax.ShapeDtypeStruct((B, T, D), F32),
        scratch_shapes=[
            pltpu.VMEM((HALO + tm, D), BF16),
            pltpu.VMEM((HALO + tm + SUBLANES, 2 * TH), F32),
            pltpu.VMEM((tm, TH), BF16),
        ],
        compiler_params=_cparams(("parallel", "parallel", "arbitrary"), VMEM_BIG),
        name="ffn",
    )(xall, xall, xall, g, mod5, mod5, mod5, mod5, mod5, mod5, wa, wg, cw_r, cb_r, wd_p)


def _final_kernel(x_ref, g_ref, o_ref):
    x = x_ref[...]
    o_ref[...] = x * lax.rsqrt(jnp.mean(x * x, axis=-1, keepdims=True) + EPS) * g_ref[...]


def _final_norm(xall, g):
    tm = CTX
    return pl.pallas_call(
        _final_kernel,
        grid=(B, SEQ // tm),
        in_specs=[
            pl.BlockSpec((None, tm, D), lambda b, i: (b, i + 1, 0)),
            pl.BlockSpec((1, D), lambda b, i: (0, 0)),
        ],
        out_specs=pl.BlockSpec((None, tm, D), lambda b, i: (b, i, 0)),
        out_shape=jax.ShapeDtypeStruct((B, SEQ, D), F32),
        compiler_params=_cparams(("parallel", "parallel")),
        name="final_norm",
    )(xall, g.reshape(1, D))


def _rope_tables():
    t = jnp.arange(SEQ)
    pos = jnp.stack([t // GRID_W, t % GRID_W], axis=-1).astype(F32)
    inv = ROPE_BASE ** (-jnp.arange(ROPE_F, dtype=F32) / ROPE_F)
    ang = pos[:, :, None] * inv
    cos, sin = jnp.cos(ang), jnp.sin(ang)
    cos_h = jnp.concatenate([cos[:, 0], cos[:, 0], cos[:, 1], cos[:, 1]], axis=-1)
    sin_h = jnp.concatenate([-sin[:, 0], sin[:, 0], -sin[:, 1], sin[:, 1]], axis=-1)
    reps = LANES // HEAD
    return jnp.tile(cos_h, (1, reps)), jnp.tile(sin_h, (1, reps))


def _bias_tables(rpb):
    nl = rpb.shape[0]
    nr, nc = 2 * NA_ROWS - 1, 2 * NA_KC - 1
    pairs = rpb.reshape(nl, NA_HEADS // 2, 2, nr, nc) * LOG2E
    pairs = jnp.pad(pairs, ((0, 0), (0, 0), (0, 0), (0, 2 * NA_ROWS - nr), (0, LANES - nc)))
    return pl.pallas_call(
        _bias_kernel,
        grid=(nl, NA_HEADS // 2),
        in_specs=[pl.BlockSpec((None, None, 2, 2 * NA_ROWS, LANES), lambda l, p: (l, p, 0, 0, 0))],
        out_specs=pl.BlockSpec((None, None, NA_ROWS, 2 * GRID_W, NA_ROWS * GRID_W), lambda l, p: (l, p, 0, 0, 0)),
        out_shape=jax.ShapeDtypeStruct((nl, NA_HEADS // 2, NA_ROWS, 2 * GRID_W, NA_ROWS * GRID_W), F32),
        compiler_params=_cparams(("parallel", "parallel")),
        name="bias_table",
    )(pairs)


def _bias_kernel(rp_ref, o_ref):
    c = lax.broadcasted_iota(jnp.int32, (GRID_W, LANES), 0)
    lane = lax.broadcasted_iota(jnp.int32, (GRID_W, LANES), 1)
    kc = lane % GRID_W
    col_start = jnp.clip(c - NA_KC // 2, 0, GRID_W - NA_KC)
    in_win = jnp.logical_and(kc >= col_start, kc < col_start + NA_KC)
    first = lane < GRID_W
    per_blk = LANES // GRID_W
    for cls in range(NA_ROWS):
        for hd in range(2):
            for blk in range(NA_ROWS // per_blk):
                parts = []
                for h in range(per_blk):
                    m = blk * per_blk + h - cls + NA_ROWS - 1
                    row = jnp.broadcast_to(rp_ref[hd, m:m + 1, :], (GRID_W, LANES))
                    shift = (LANES - (NA_KC - 1) + h * GRID_W) % LANES
                    parts.append(pltpu.roll(row, shift, 1, stride=1, stride_axis=0))
                tile = jnp.where(in_win, jnp.where(first, parts[0], parts[1]), NEG)
                o_ref[cls, hd * GRID_W:(hd + 1) * GRID_W, blk * LANES:(blk + 1) * LANES] = tile


def _gate_blockdiag(gate_w):
    nl = gate_w.shape[0]
    nh = LRU_W // LRU_HALF
    gper = LRU_HALF // HEAD
    w = gate_w.reshape(nl, 4, nh, gper, HEAD, HEAD)
    eye = jnp.eye(gper, dtype=gate_w.dtype)
    bd = jnp.einsum('lkhgio,gj->lkhgijo', w, eye).reshape(nl, 4, nh, LRU_HALF, LRU_HALF)
    return jnp.transpose(bd, (0, 2, 1, 3, 4)).astype(BF16)


def _ffn_layout(w_up, conv_w, conv_b, w_down):
    pad = FFN_HP - FFN_H
    nl = w_up.shape[0]

    def inter(m):
        r = m.shape[1]
        a = jnp.pad(m[..., :FFN_H], ((0, 0), (0, 0), (0, pad))).reshape(nl, r, NK, 1, TH)
        g = jnp.pad(m[..., FFN_H:], ((0, 0), (0, 0), (0, pad))).reshape(nl, r, NK, 1, TH)
        return jnp.concatenate([a, g], axis=3).reshape(nl, r, NK * 2 * TH)

    zc = jnp.zeros((nl, D, pad), BF16)
    wa = jnp.concatenate([w_up[..., :FFN_H].astype(BF16), zc], axis=2)
    wg = jnp.concatenate([w_up[..., FFN_H:].astype(BF16), zc], axis=2)
    wd = jnp.concatenate([w_down.astype(BF16), jnp.zeros((nl, pad, D), BF16)], axis=1)
    return wa, wg, inter(conv_w), inter(conv_b[:, None, :]), wd


def kernel(x, c, ctx, c_ctx, w_ada, b_ada, norm_mix_g, norm_ffn_g, w_in, lru_conv_w, lru_conv_b, lru_gate_w, lru_gate_b, lru_lambda, na_rpb, sgu_ln_g, sgu_ln_b, sgu_w, sgu_b, w_out, ffn_up, ffn_conv_w, ffn_conv_b, ffn_down, final_norm_g):
    xall = jnp.concatenate([ctx, x], axis=1)
    cc = jnp.concatenate([c, c_ctx[None], jnp.zeros((MOD_ROWS - B - 1, D), F32)], axis=0)
    mod5 = _adaln(cc, w_ada, b_ada).reshape(DEPTH, 6, MOD_ROWS, 1, D)
    cos_t, sin_t = _rope_tables()

    g_mix = norm_mix_g.reshape(DEPTH, 1, D)
    g_ffn = norm_ffn_g.reshape(DEPTH, 1, D)
    w_in_bf = w_in.astype(BF16)
    w_out_bf = w_out.astype(BF16)
    lru_cb = lru_conv_b.reshape(DEPTH, 1, LRU_W)
    lru_gw = _gate_blockdiag(lru_gate_w)
    lru_gb = lru_gate_b.reshape(DEPTH, 4, LRU_W)
    bias = _bias_tables(na_rpb)
    sgu_g = sgu_ln_g.reshape(DEPTH, 1, SGU_W)
    sgu_bb = sgu_ln_b.reshape(DEPTH, 1, SGU_W)
    sgu_w_bf = sgu_w.astype(BF16)
    sgu_bias = jnp.repeat(jnp.swapaxes(sgu_b, 1, 2), HEAD, axis=2)
    ffn_params = _ffn_layout(ffn_up, ffn_conv_w, ffn_conv_b, ffn_down)

    for l in range(DEPTH):
        z = _inproj(xall, mod5, l, g_mix, w_in_bf)
        oa = _lru(z, l, lru_conv_w, lru_cb, lru_gw, lru_gb, lru_lambda)
        ob = _attn(z, l, cos_t, sin_t, bias)
        oc = _sgu(z, l, sgu_g, sgu_bb, sgu_w_bf, sgu_bias)
        xall = _outproj(xall, oa, ob, oc, mod5, l, w_out_bf)
        xall = _ffn(xall, mod5, l, g_ffn, *ffn_params)
    return _final_norm(xall, final_norm_g)
```
